```python
import math
import jax, jax.numpy as jnp
from jax import lax
import numpy as np

D_MODEL = 1024
BATCH = 4
SEQ = 8192
DEPTH = 1

CHUNK = 64
D_MIX = 1024
EPS = 1e-6
GMLP_GROUPS = 4
GMLP_GROUP_DIM = 128
GMLP_WIDTH = GMLP_GROUPS * GMLP_GROUP_DIM
GMLP_BLOCK = 128
DSA_HEADS = 8
DSA_V_DIM = 64
DSA_WIDTH = DSA_HEADS * DSA_V_DIM
DSA_NOPE_DIM = 64
DSA_ROPE_DIM = 32
DSA_Q_RANK = 256
DSA_KV_RANK = 128
IDX_HEADS = 8
IDX_DIM = 64
TOPK_MAX = 256
Q_BLOCK = 128
ROPE_THETA = 10000.0

IN_SIZES = (2 * GMLP_WIDTH, GMLP_WIDTH, DSA_Q_RANK, DSA_KV_RANK, DSA_ROPE_DIM, IDX_DIM, IDX_HEADS, DSA_WIDTH)
D_IN = sum(IN_SIZES)
IN_SPLITS = tuple(int(v) for v in np.cumsum(IN_SIZES)[:-1])

kernel_name = "hymba_gmlp_dsa_streaming_block"


def rmsnorm(x, g):
    x32 = x.astype(jnp.float32)
    r = x32 * lax.rsqrt(jnp.mean(x32 * x32, axis=-1, keepdims=True) + EPS)
    return (r * g.astype(jnp.float32)).astype(x.dtype)


def layernorm(x, g, b):
    x32 = x.astype(jnp.float32)
    mu = jnp.mean(x32, axis=-1, keepdims=True)
    var = jnp.mean(jnp.square(x32 - mu), axis=-1, keepdims=True)
    r = (x32 - mu) * lax.rsqrt(var + EPS)
    return (r * g.astype(jnp.float32) + b.astype(jnp.float32)).astype(x.dtype)


def rope_tables(seq_len):
    pos = jnp.arange(seq_len, dtype=jnp.float32)
    inv_freq = ROPE_THETA ** (-jnp.arange(0, DSA_ROPE_DIM, 2, dtype=jnp.float32) / DSA_ROPE_DIM)
    ang = pos[:, None] * inv_freq[None, :]
    return jnp.cos(ang), jnp.sin(ang)


def apply_rope(x, cos, sin):
    cos = cos.astype(x.dtype)
    sin = sin.astype(x.dtype)
    x1, x2 = jnp.split(x, 2, axis=-1)
    return jnp.concatenate([x1 * cos - x2 * sin, x2 * cos + x1 * sin], axis=-1)


def gmlp_branch(uv, z, ln_g, ln_b, w_s, b_s):
    B, S, _ = uv.shape
    uv = jax.nn.gelu(uv, approximate=False)
    u, v = jnp.split(uv, 2, axis=-1)
    v = layernorm(v, ln_g, ln_b)
    nb = S // GMLP_BLOCK
    v = v.reshape(B, nb, GMLP_BLOCK, GMLP_GROUPS, GMLP_GROUP_DIM)
    p = jnp.arange(GMLP_BLOCK)
    mask = (p[None, :] // CHUNK) <= (p[:, None] // CHUNK)
    w = jnp.where(mask[None], w_s, 0.0).astype(v.dtype)
    y = jnp.einsum('gts,bnsgc->bntgc', w, v) + b_s.T.astype(v.dtype)[None, None, :, :, None]
    y = y.reshape(B, S, GMLP_WIDTH)
    return u * y * jax.nn.silu(z)


def dsa_branch(c_q, c_kv, k_rope, k_idx, w_idx, z, q_norm_g, kv_norm_g, w_uq, w_uk, w_uv, w_q_idx):
    B, S, _ = c_q.shape
    cos, sin = rope_tables(S)
    c_q = rmsnorm(c_q, q_norm_g)
    c_kv = rmsnorm(c_kv, kv_norm_g)
    q = (c_q @ w_uq).reshape(B, S, DSA_HEADS, DSA_NOPE_DIM + DSA_ROPE_DIM)
    q_nope, q_rope = q[..., :DSA_NOPE_DIM], q[..., DSA_NOPE_DIM:]
    q_rope = apply_rope(q_rope, cos[:, None, :], sin[:, None, :])
    k_rope = apply_rope(k_rope, cos, sin)
    q_lat = jnp.einsum('bshd,chd->bshc', q_nope, w_uk)
    q_idx = (c_q @ w_q_idx).reshape(B, S, IDX_HEADS, IDX_DIM)
    w_idx = w_idx * (IDX_HEADS ** -0.5 * IDX_DIM ** -0.5)
    top_k = min(TOPK_MAX, S // 4)
    scale = 1.0 / math.sqrt(DSA_NOPE_DIM + DSA_ROPE_DIM)
    nqb = S // Q_BLOCK
    key_chunk = jnp.arange(S) // CHUNK
    b_ix = jnp.arange(B)[:, None, None]

    def to_blocks(a):
        return a.reshape((B, nqb, Q_BLOCK) + a.shape[2:]).swapaxes(0, 1)

    def block(args):
        ql, qr, qi, wi, nblk = args
        t = nblk * Q_BLOCK + jnp.arange(Q_BLOCK)
        logits = jnp.einsum('bqhd,bsd->bqhs', qi, k_idx)
        score = jnp.einsum('bqh,bqhs->bqs', wi, jax.nn.relu(logits)).astype(jnp.float32)
        admissible = key_chunk[None, :] <= (t // CHUNK)[:, None]
        score = jnp.where(admissible[None], score, -jnp.inf)
        top_val, top_idx = lax.top_k(score, top_k)
        valid = jnp.isfinite(top_val)
        c_sel = c_kv[b_ix, top_idx]
        kr_sel = k_rope[b_ix, top_idx]
        s = (jnp.einsum('bqhc,bqkc->bqhk', ql, c_sel)
             + jnp.einsum('bqhr,bqkr->bqhk', qr, kr_sel)).astype(jnp.float32) * scale
        s = jnp.where(valid[:, :, None, :], s, -jnp.inf)
        p = jax.nn.softmax(s, axis=-1).astype(c_sel.dtype)
        return jnp.einsum('bqhk,bqkc->bqhc', p, c_sel)

    xs = (to_blocks(q_lat), to_blocks(q_rope), to_blocks(q_idx), to_blocks(w_idx), jnp.arange(nqb))
    o_lat = lax.map(block, xs)
    o_lat = o_lat.swapaxes(0, 1).reshape(B, S, DSA_HEADS, DSA_KV_RANK)
    o = jnp.einsum('bshc,chd->bshd', o_lat, w_uv).reshape(B, S, DSA_WIDTH)
    return o * jax.nn.silu(z)


def setup_inputs(seed: int = 0) -> dict:
    key = jax.random.key(seed)
    ks = jax.random.split(key, 16)
    f32 = jnp.float32
    nrm = lambda k, shape, s: jax.random.normal(k, shape, f32) * s
    return {
        "x": jax.random.normal(ks[0], (BATCH, SEQ, D_MODEL), f32),
        "w_in": nrm(ks[1], (DEPTH, D_MODEL, D_IN), D_MODEL ** -0.5),
        "pre_norm_g": 1.0 + nrm(ks[2], (DEPTH, D_MODEL), 0.02),
        "post_norm_g": 1.0 + nrm(ks[3], (DEPTH, D_MODEL), 0.02),
        "gmlp_ln_g": 1.0 + nrm(ks[4], (DEPTH, GMLP_WIDTH), 0.02),
        "gmlp_ln_b": nrm(ks[5], (DEPTH, GMLP_WIDTH), 0.02),
        "gmlp_w_s": nrm(ks[6], (DEPTH, GMLP_GROUPS, GMLP_BLOCK, GMLP_BLOCK), GMLP_BLOCK ** -0.5),
        "gmlp_b_s": 1.0 + nrm(ks[7], (DEPTH, GMLP_GROUPS, GMLP_BLOCK), 0.1),
        "dsa_q_norm_g": 1.0 + nrm(ks[8], (DEPTH, DSA_Q_RANK), 0.02),
        "dsa_kv_norm_g": 1.0 + nrm(ks[9], (DEPTH, DSA_KV_RANK), 0.02),
        "dsa_w_uq": nrm(ks[10], (DEPTH, DSA_Q_RANK, DSA_HEADS * (DSA_NOPE_DIM + DSA_ROPE_DIM)), DSA_Q_RANK ** -0.5),
        "dsa_w_uk": nrm(ks[11], (DEPTH, DSA_KV_RANK, DSA_HEADS, DSA_NOPE_DIM), DSA_NOPE_DIM ** -0.5),
        "dsa_w_uv": nrm(ks[12], (DEPTH, DSA_KV_RANK, DSA_HEADS, DSA_V_DIM), DSA_KV_RANK ** -0.5),
        "dsa_w_q_idx": nrm(ks[13], (DEPTH, DSA_Q_RANK, IDX_HEADS * IDX_DIM), DSA_Q_RANK ** -0.5),
        "w_out": nrm(ks[14], (DEPTH, D_MIX, D_MODEL), D_MIX ** -0.5),
    }


def reference(x, w_in, pre_norm_g, post_norm_g, gmlp_ln_g, gmlp_ln_b, gmlp_w_s, gmlp_b_s,
              dsa_q_norm_g, dsa_kv_norm_g, dsa_w_uq, dsa_w_uk, dsa_w_uv, dsa_w_q_idx, w_out):
    for l in range(DEPTH):
        h = rmsnorm(x, pre_norm_g[l])
        proj = h @ w_in[l]
        uv_a, z_a, c_q, c_kv, k_rope, k_idx, w_idx, z_b = jnp.split(proj, IN_SPLITS, axis=-1)
        out_a = gmlp_branch(uv_a, z_a, gmlp_ln_g[l], gmlp_ln_b[l], gmlp_w_s[l], gmlp_b_s[l])
        out_b = dsa_branch(c_q, c_kv, k_rope, k_idx, w_idx, z_b, dsa_q_norm_g[l], dsa_kv_norm_g[l],
                           dsa_w_uq[l], dsa_w_uk[l], dsa_w_uv[l], dsa_w_q_idx[l])
        y = jnp.concatenate([out_a, out_b], axis=-1) @ w_out[l]
        x = x + rmsnorm(y, post_norm_g[l])
    return x
```

```python
import functools
import math

import jax
import jax.numpy as jnp
import numpy as np
from jax import lax
from jax.experimental import pallas as pl
from jax.experimental.pallas import tpu as pltpu

D_MODEL = 1024
CHUNK = 64
EPS = 1e-6
GMLP_GROUPS = 4
GMLP_GROUP_DIM = 128
GMLP_WIDTH = GMLP_GROUPS * GMLP_GROUP_DIM
GMLP_BLOCK = 128
DSA_HEADS = 8
DSA_V_DIM = 64
DSA_WIDTH = DSA_HEADS * DSA_V_DIM
DSA_NOPE_DIM = 64
DSA_ROPE_DIM = 32
DSA_Q_RANK = 256
DSA_KV_RANK = 128
IDX_HEADS = 8
IDX_DIM = 64
TOPK_MAX = 256
ROPE_THETA = 10000.0

LANES = 128
ROPE_HALF = DSA_ROPE_DIM // 2
ROPE_LANE_GROUPS = LANES // DSA_ROPE_DIM
Q_CAT = DSA_KV_RANK + LANES
W_IDX_LANE = IDX_DIM

TM = 256
TQ = 128
KB = 512
NEG_BIG = -1e30
F32_LOWEST = float(np.finfo(np.float32).min)
MAX_BISECT = 256

_C_UV, _C_ZA, _C_CQ, _C_CKV = 0, 1024, 1536, 1792
_C_KR, _C_KRS, _C_SLAB, _C_ZB, _C_END = 1920, 2048, 2176, 2304, 2816
_Q_NOPE, _Q_ROPE, _Q_ROPES, _Q_IDX, _Q_END = 0, 1024, 1280, 1536, 2560

_NT = (((1,), (1,)), ((), ()))


def _bf16(a):
    return a.astype(jnp.bfloat16)


def _dot(a, b):
    return jnp.dot(a, b, preferred_element_type=jnp.float32)


def _dot_nt(a, b):
    return lax.dot_general(a, b, _NT, preferred_element_type=jnp.float32)


def _chunk_of(pos):
    return lax.shift_right_logical(pos, jnp.int32(CHUNK.bit_length() - 1))


def _silu(z):
    return z / (1.0 + jnp.exp(-z))


def _gelu_exact(a):
    return 0.5 * a * (1.0 + lax.erf(a * np.float32(math.sqrt(0.5))))


def _proj_kernel(x_ref, pre_g_ref, w_all_ref, ln_g_ref, ln_b_ref, w_s_ref, b_s_ref,
                 qn_g_ref, kvn_g_ref, w_q_ref, w_uk_ref, cos_ref, sin_ref,
                 out_a_ref, gate_b_ref, qidx_ref, widx_ref, qcat_ref, kidx_ref, kcat_ref, vcat_ref):
    x = x_ref[0]
    h = x * lax.rsqrt(jnp.mean(x * x, axis=-1, keepdims=True) + EPS) * pre_g_ref[...]
    proj = _dot(_bf16(h), w_all_ref[...])

    uv = _gelu_exact(proj[:, _C_UV:_C_ZA])
    u, v = uv[:, :GMLP_WIDTH], uv[:, GMLP_WIDTH:]
    mu = jnp.mean(v, axis=-1, keepdims=True)
    vc = v - mu
    var = jnp.mean(vc * vc, axis=-1, keepdims=True)
    vn = _bf16(vc * lax.rsqrt(var + EPS) * ln_g_ref[...] + ln_b_ref[...])
    gate_a = _silu(proj[:, _C_ZA:_C_CQ])
    t_chunk = _chunk_of(lax.broadcasted_iota(jnp.int32, (GMLP_BLOCK, GMLP_BLOCK), 0))
    s_chunk = _chunk_of(lax.broadcasted_iota(jnp.int32, (GMLP_BLOCK, GMLP_BLOCK), 1))
    for g in range(GMLP_GROUPS):
        w_g = _bf16(jnp.where(s_chunk <= t_chunk, w_s_ref[g], 0.0))
        cols = slice(g * GMLP_GROUP_DIM, (g + 1) * GMLP_GROUP_DIM)
        for r in range(TM // GMLP_BLOCK):
            rows = slice(r * GMLP_BLOCK, (r + 1) * GMLP_BLOCK)
            y = _dot(w_g, vn[rows, cols]) + b_s_ref[g]
            out_a_ref[0, rows, cols] = _bf16(u[rows, cols] * y * gate_a[rows, cols])

    gate_b_ref[0] = _silu(proj[:, _C_ZB:_C_END])

    c_q = proj[:, _C_CQ:_C_CKV]
    c_q = c_q * lax.rsqrt(jnp.mean(c_q * c_q, axis=-1, keepdims=True) + EPS) * qn_g_ref[...]
    c_kv = proj[:, _C_CKV:_C_KR]
    c_kv = _bf16(c_kv * lax.rsqrt(jnp.mean(c_kv * c_kv, axis=-1, keepdims=True) + EPS) * kvn_g_ref[...])
    cos = cos_ref[...]
    sin = sin_ref[...]
    k_rope = proj[:, _C_KR:_C_KRS] * cos[:, :LANES] + proj[:, _C_KRS:_C_SLAB] * sin[:, :LANES]
    slab = proj[:, _C_SLAB:_C_ZB]
    kidx_ref[0] = _bf16(slab)
    widx_ref[0] = slab
    kcat_ref[0, :, :DSA_KV_RANK] = c_kv
    kcat_ref[0, :, DSA_KV_RANK:] = _bf16(k_rope)
    lane = lax.broadcasted_iota(jnp.int32, (TM, LANES), 1)
    vcat_ref[0, :, :DSA_KV_RANK] = c_kv
    vcat_ref[0, :, DSA_KV_RANK:] = jnp.where(lane == 0, 1.0, 0.0).astype(jnp.bfloat16)

    q_all = _dot(_bf16(c_q), w_q_ref[...])
    q_rope = q_all[:, _Q_ROPE:_Q_ROPES] * cos + q_all[:, _Q_ROPES:_Q_IDX] * sin
    scale = np.float32(1.0 / math.sqrt(DSA_NOPE_DIM + DSA_ROPE_DIM))
    for hd in range(DSA_HEADS):
        cols = slice(_Q_NOPE + hd * LANES, _Q_NOPE + (hd + 1) * LANES)
        q_lat = _dot(_bf16(q_all[:, cols]), w_uk_ref[hd])
        qcat_ref[0, hd, :, :DSA_KV_RANK] = _bf16(q_lat * scale)
        grp, sub = divmod(hd, ROPE_LANE_GROUPS)
        own = (lane >= sub * DSA_ROPE_DIM) & (lane < (sub + 1) * DSA_ROPE_DIM)
        q_r = q_rope[:, grp * LANES:(grp + 1) * LANES]
        qcat_ref[0, hd, :, DSA_KV_RANK:] = _bf16(jnp.where(own, q_r * scale, 0.0))
        icol = slice(_Q_IDX + hd * LANES, _Q_IDX + (hd + 1) * LANES)
        qidx_ref[0, hd] = _bf16(q_all[:, icol])


def _dsa_kernel(qidx_ref, widx_ref, qcat_ref, kidx_ref, kcat_ref, vcat_ref, o_ref,
                sc_ref, m_ref, acc_ref, *, top_k):
    i = pl.program_id(1)
    n_blocks = (i * TQ) // KB + 1
    n_chunks = KB // LANES
    idx_scale = np.float32(IDX_HEADS ** -0.5 * IDX_DIM ** -0.5)

    row = lax.broadcasted_iota(jnp.int32, (TQ, LANES), 0)
    lane = lax.broadcasted_iota(jnp.int32, (TQ, LANES), 1)
    q_chunk = _chunk_of(i * TQ + row)
    n_adm = (q_chunk + 1) * CHUNK

    w = widx_ref[0] * idx_scale
    w_rep = [jnp.broadcast_to(w[:, W_IDX_LANE + hd:W_IDX_LANE + hd + 1], (TQ, LANES))
             for hd in range(IDX_HEADS)]

    def score_block(kb, carry):
        row_max, row_min = carry
        k = kidx_ref[0, pl.ds(pl.multiple_of(kb * KB, KB), KB), :]
        acc = [jnp.zeros((TQ, LANES), jnp.float32) for _ in range(n_chunks)]
        for hd in range(IDX_HEADS):
            logit = _dot_nt(qidx_ref[0, hd], k)
            for c in range(n_chunks):
                acc[c] = acc[c] + w_rep[hd] * jnp.maximum(logit[:, c * LANES:(c + 1) * LANES], 0.0)
        for c in range(n_chunks):
            key = kb * KB + c * LANES + lane
            adm = _chunk_of(key) <= q_chunk
            col = pl.ds(pl.multiple_of(kb * KB + c * LANES, LANES), LANES)
            sc_ref[:, col] = jnp.where(adm, acc[c], -jnp.inf)
            row_max = jnp.maximum(row_max, jnp.where(adm, acc[c], -jnp.inf))
            row_min = jnp.minimum(row_min, jnp.where(adm, acc[c], jnp.inf))
        return row_max, row_min

    row_max, row_min = lax.fori_loop(
        0, n_blocks, score_block,
        (jnp.full((TQ, LANES), -jnp.inf, jnp.float32), jnp.full((TQ, LANES), jnp.inf, jnp.float32)))
    row_max = jnp.broadcast_to(jnp.max(row_max, axis=1, keepdims=True), (TQ, LANES))
    row_min = jnp.broadcast_to(jnp.min(row_min, axis=1, keepdims=True), (TQ, LANES))

    ones_kb = jnp.ones((KB, LANES), jnp.bfloat16)
    kf = np.float32(top_k)

    def count_if(pred):
        def body(kb, cnt):
            hit = []
            for c in range(n_chunks):
                col = pl.ds(pl.multiple_of(kb * KB + c * LANES, LANES), LANES)
                key = kb * KB + c * LANES + lane
                hit.append(jnp.where(pred(sc_ref[:, col], key), 1.0, 0.0).astype(jnp.bfloat16))
            return cnt + _dot(jnp.concatenate(hit, axis=1), ones_kb)
        return lax.fori_loop(0, n_blocks, body, jnp.zeros((TQ, LANES), jnp.float32))

    select_all = n_adm <= top_k
    lo0 = jnp.where(select_all, F32_LOWEST, row_min)
    hi0 = jnp.where(select_all, F32_LOWEST,
                    row_max + (jnp.abs(row_max) + (row_max - row_min)) * 1e-3 + 1e-30)
    active0 = jnp.where(select_all, 0.0, 1.0)

    def bisect_cond(state):
        it, _, _, active = state
        return jnp.logical_and(it < MAX_BISECT, jnp.max(active) > 0.0)

    def bisect_body(state):
        it, lo, hi, active = state
        mid = 0.5 * (lo + hi)
        cnt = count_if(lambda s, key: s >= mid)
        on = active > 0.0
        stuck = (mid <= lo) | (mid >= hi)
        exact = cnt == kf
        ge = cnt >= kf
        new_lo = jnp.where(exact | ge, mid, lo)
        new_hi = jnp.where(exact | ~ge, mid, hi)
        upd = on & (exact | ~stuck)
        lo = jnp.where(upd, new_lo, lo)
        hi = jnp.where(upd, new_hi, hi)
        active = jnp.where(on & ~exact & ~stuck, 1.0, 0.0)
        return it + 1, lo, hi, active

    _, lo, hi, _ = lax.while_loop(bisect_cond, bisect_body, (jnp.int32(0), lo0, hi0, active0))

    tie = lo < hi

    @pl.when(jnp.max(jnp.where(tie, 1.0, 0.0)) > 0.0)
    def _():
        need = kf - count_if(lambda s, key: s >= hi)
        def in_band(s):
            return (s >= lo) & (s < hi)
        def idx_body(_, bounds):
            jlo, jhi = bounds
            jmid = jnp.floor(0.5 * (jlo + jhi))
            cnt = count_if(lambda s, key: in_band(s) & (key.astype(jnp.float32) <= jmid))
            ok = cnt >= need
            return jnp.where(ok, jlo, jmid), jnp.where(ok, jmid, jhi)
        n_keys = (n_blocks * KB).astype(jnp.float32)
        steps = int(math.ceil(math.log2(sc_ref.shape[1]))) + 1
        _, jstar = lax.fori_loop(0, steps, idx_body,
                                 (jnp.full((TQ, LANES), -1.0, jnp.float32),
                                  jnp.zeros((TQ, LANES), jnp.float32) + (n_keys - 1.0)))
        def drop_body(kb, carry):
            for c in range(n_chunks):
                col = pl.ds(pl.multiple_of(kb * KB + c * LANES, LANES), LANES)
                key = (kb * KB + c * LANES + lane).astype(jnp.float32)
                s = sc_ref[:, col]
                sc_ref[:, col] = jnp.where(tie & in_band(s) & (key > jstar), -jnp.inf, s)
            return carry
        lax.fori_loop(0, n_blocks, drop_body, 0)

    thr = lo

    m_ref[...] = jnp.full(m_ref.shape, NEG_BIG, jnp.float32)
    acc_ref[...] = jnp.zeros(acc_ref.shape, jnp.float32)

    def attn_block(kb, carry):
        rows = pl.ds(pl.multiple_of(kb * KB, KB), KB)
        kc = kcat_ref[0, rows, :]
        vc = vcat_ref[0, rows, :]
        bias = []
        for c in range(n_chunks):
            col = pl.ds(pl.multiple_of(kb * KB + c * LANES, LANES), LANES)
            bias.append(jnp.where(sc_ref[:, col] >= thr, 0.0, NEG_BIG))
        for hd in range(DSA_HEADS):
            s = _dot_nt(qcat_ref[0, hd], kc)
            sm = [s[:, c * LANES:(c + 1) * LANES] + bias[c] for c in range(n_chunks)]
            blk_max = functools.reduce(jnp.maximum, sm)
            blk_max = jnp.broadcast_to(jnp.max(blk_max, axis=1, keepdims=True), (TQ, LANES))
            m_old = m_ref[hd]
            m_new = jnp.maximum(m_old, blk_max)
            alpha = jnp.exp(m_old - m_new)
            p = jnp.concatenate([_bf16(jnp.exp(x - m_new)) for x in sm], axis=1)
            pv = _dot(p, vc)
            acc_ref[hd] = acc_ref[hd] * jnp.concatenate([alpha, alpha], axis=1) + pv
            m_ref[hd] = m_new
        return carry

    lax.fori_loop(0, n_blocks, attn_block, 0)

    for hd in range(DSA_HEADS):
        a = acc_ref[hd]
        denom = jnp.broadcast_to(a[:, DSA_KV_RANK:DSA_KV_RANK + 1], (TQ, DSA_KV_RANK))
        o_ref[0, :, hd * DSA_KV_RANK:(hd + 1) * DSA_KV_RANK] = _bf16(a[:, :DSA_KV_RANK] / denom)


def _out_kernel(x_ref, out_a_ref, o_lat_ref, gate_b_ref, w_uv_ref, w_out_ref, post_g_ref, y_ref):
    o = _dot(o_lat_ref[0], w_uv_ref[...])
    out_b = _bf16(o * gate_b_ref[0])
    y = _dot(out_a_ref[0], w_out_ref[:GMLP_WIDTH, :]) + _dot(out_b, w_out_ref[GMLP_WIDTH:, :])
    y = y * lax.rsqrt(jnp.mean(y * y, axis=-1, keepdims=True) + EPS) * post_g_ref[...]
    y_ref[0] = x_ref[0] + y


def _full(shape):
    return pl.BlockSpec(shape, lambda b, t: (0,) * len(shape))


def _layer(x, w_in, pre_g, post_g, ln_g, ln_b, w_s, b_s, qn_g, kvn_g, w_uq, w_uk, w_uv, w_q_idx, w_out):
    B, S, D = x.shape
    assert D == D_MODEL and S % KB == 0 and S % TM == 0
    top_k = min(TOPK_MAX, S // 4)
    f32, bf16 = jnp.float32, jnp.bfloat16

    kr = w_in[:, 1920:1952]
    kr_sw = jnp.concatenate([kr[:, ROPE_HALF:], kr[:, :ROPE_HALF]], axis=1)
    w_all = jnp.concatenate([
        w_in[:, :1920], jnp.tile(kr, (1, ROPE_LANE_GROUPS)), jnp.tile(kr_sw, (1, ROPE_LANE_GROUPS)),
        w_in[:, 1952:2024], jnp.zeros((D, LANES - IDX_DIM - IDX_HEADS), f32), w_in[:, 2024:]], axis=1).astype(bf16)
    wq3 = w_uq.reshape(DSA_Q_RANK, DSA_HEADS, DSA_NOPE_DIM + DSA_ROPE_DIM)
    nope = jnp.pad(wq3[:, :, :DSA_NOPE_DIM], ((0, 0), (0, 0), (0, LANES - DSA_NOPE_DIM)))
    rope = wq3[:, :, DSA_NOPE_DIM:]
    rope_sw = jnp.concatenate([rope[:, :, ROPE_HALF:], rope[:, :, :ROPE_HALF]], axis=2)
    wqi = jnp.pad(w_q_idx.reshape(DSA_Q_RANK, IDX_HEADS, IDX_DIM), ((0, 0), (0, 0), (0, LANES - IDX_DIM)))
    w_q = jnp.concatenate([nope.reshape(DSA_Q_RANK, -1), rope.reshape(DSA_Q_RANK, -1),
                           rope_sw.reshape(DSA_Q_RANK, -1), wqi.reshape(DSA_Q_RANK, -1)], axis=1).astype(bf16)
    w_uk_t = jnp.pad(jnp.transpose(w_uk, (1, 2, 0)), ((0, 0), (0, LANES - DSA_NOPE_DIM), (0, 0))).astype(bf16)
    eye = jnp.eye(DSA_HEADS, dtype=f32)
    w_uv_bd = (jnp.transpose(w_uv, (1, 0, 2))[:, :, None, :] * eye[:, None, :, None]).reshape(
        DSA_HEADS * DSA_KV_RANK, DSA_WIDTH).astype(bf16)
    b_s_b = jnp.broadcast_to(b_s[:, :, None], (GMLP_GROUPS, GMLP_BLOCK, GMLP_GROUP_DIM))

    pos = jnp.arange(S, dtype=f32)
    inv_freq = ROPE_THETA ** (-jnp.arange(0, DSA_ROPE_DIM, 2, dtype=f32) / DSA_ROPE_DIM)
    ang = pos[:, None] * inv_freq[None, :]
    cos_t = jnp.tile(jnp.concatenate([jnp.cos(ang), jnp.cos(ang)], axis=1), (1, DSA_HEADS))
    sin_t = jnp.tile(jnp.concatenate([-jnp.sin(ang), jnp.sin(ang)], axis=1), (1, DSA_HEADS))

    row2 = lambda a: a.reshape(1, -1)
    tok = lambda width: pl.BlockSpec((1, TM, width), lambda b, t: (b, t, 0))
    hm = lambda width: pl.BlockSpec((1, DSA_HEADS, TM, width), lambda b, t: (b, 0, t, 0))

    out_a, gate_b, qidx, widx, qcat, kidx, kcat, vcat = pl.pallas_call(
        _proj_kernel,
        grid=(B, S // TM),
        in_specs=[tok(D), _full((1, D)), _full((D, _C_END)), _full((1, GMLP_WIDTH)), _full((1, GMLP_WIDTH)),
                  _full((GMLP_GROUPS, GMLP_BLOCK, GMLP_BLOCK)), _full((GMLP_GROUPS, GMLP_BLOCK, GMLP_GROUP_DIM)),
                  _full((1, DSA_Q_RANK)), _full((1, DSA_KV_RANK)), _full((DSA_Q_RANK, _Q_END)),
                  _full((DSA_HEADS, LANES, DSA_KV_RANK)),
                  pl.BlockSpec((TM, 2 * LANES), lambda b, t: (t, 0)),
                  pl.BlockSpec((TM, 2 * LANES), lambda b, t: (t, 0))],
        out_specs=[tok(GMLP_WIDTH), tok(DSA_WIDTH), hm(LANES), tok(LANES), hm(Q_CAT),
                   tok(LANES), tok(Q_CAT), tok(Q_CAT)],
        out_shape=[jax.ShapeDtypeStruct((B, S, GMLP_WIDTH), bf16),
                   jax.ShapeDtypeStruct((B, S, DSA_WIDTH), f32),
                   jax.ShapeDtypeStruct((B, DSA_HEADS, S, LANES), bf16),
                   jax.ShapeDtypeStruct((B, S, LANES), f32),
                   jax.ShapeDtypeStruct((B, DSA_HEADS, S, Q_CAT), bf16),
                   jax.ShapeDtypeStruct((B, S, LANES), bf16),
                   jax.ShapeDtypeStruct((B, S, Q_CAT), bf16),
                   jax.ShapeDtypeStruct((B, S, Q_CAT), bf16)],
        compiler_params=pltpu.CompilerParams(
            dimension_semantics=("arbitrary", "arbitrary"), vmem_limit_bytes=48 * 1024 * 1024),
        name="proj_gmlp_dsa_prep",
    )(x, row2(pre_g), w_all, row2(ln_g), row2(ln_b), w_s, b_s_b, row2(qn_g), row2(kvn_g), w_q, w_uk_t,
      cos_t, sin_t)

    qt = lambda width: pl.BlockSpec((1, DSA_HEADS, TQ, width), lambda b, t: (b, 0, t, 0))
    seq = lambda width: pl.BlockSpec((1, S, width), lambda b, t: (b, 0, 0))
    o_lat = pl.pallas_call(
        functools.partial(_dsa_kernel, top_k=top_k),
        grid=(B, S // TQ),
        in_specs=[qt(LANES), pl.BlockSpec((1, TQ, LANES), lambda b, t: (b, t, 0)), qt(Q_CAT),
                  seq(LANES), seq(Q_CAT), seq(Q_CAT)],
        out_specs=pl.BlockSpec((1, TQ, DSA_HEADS * DSA_KV_RANK), lambda b, t: (b, t, 0)),
        out_shape=jax.ShapeDtypeStruct((B, S, DSA_HEADS * DSA_KV_RANK), bf16),
        scratch_shapes=[pltpu.VMEM((TQ, S), f32),
                        pltpu.VMEM((DSA_HEADS, TQ, LANES), f32),
                        pltpu.VMEM((DSA_HEADS, TQ, Q_CAT), f32)],
        compiler_params=pltpu.CompilerParams(
            dimension_semantics=("arbitrary", "arbitrary"), vmem_limit_bytes=48 * 1024 * 1024),
        name="dsa_index_select_attend",
    )(qidx, widx, qcat, kidx, kcat, vcat)

    return pl.pallas_call(
        _out_kernel,
        grid=(B, S // TM),
        in_specs=[tok(D), tok(GMLP_WIDTH), tok(DSA_HEADS * DSA_KV_RANK), tok(DSA_WIDTH),
                  _full((DSA_HEADS * DSA_KV_RANK, DSA_WIDTH)), _full((GMLP_WIDTH + DSA_WIDTH, D)), _full((1, D))],
        out_specs=tok(D),
        out_shape=jax.ShapeDtypeStruct((B, S, D), f32),
        compiler_params=pltpu.CompilerParams(
            dimension_semantics=("arbitrary", "arbitrary"), vmem_limit_bytes=32 * 1024 * 1024),
        name="out_proj_norm_residual",
    )(x, out_a, o_lat, gate_b, w_uv_bd, w_out.astype(bf16), row2(post_g))


def kernel(x, w_in, pre_norm_g, post_norm_g, gmlp_ln_g, gmlp_ln_b, gmlp_w_s, gmlp_b_s, dsa_q_norm_g, dsa_kv_norm_g, dsa_w_uq, dsa_w_uk, dsa_w_uv, dsa_w_q_idx, w_out):
    for l in range(w_in.shape[0]):
        x = _layer(x, w_in[l], pre_norm_g[l], post_norm_g[l], gmlp_ln_g[l], gmlp_ln_b[l], gmlp_w_s[l],
                   gmlp_b_s[l], dsa_q_norm_g[l], dsa_kv_norm_g[l], dsa_w_uq[l], dsa_w_uk[l], dsa_w_uv[l],
                   dsa_w_q_idx[l], w_out[l])
    return x
```

```python
import functools
import math

import jax
import jax.numpy as jnp
import numpy as np
from jax import lax
from jax.experimental import pallas as pl
from jax.experimental.pallas import tpu as pltpu

D_MODEL = 1024
CHUNK = 64
EPS = 1e-6
GMLP_GROUPS = 4
GMLP_GROUP_DIM = 128
GMLP_WIDTH = GMLP_GROUPS * GMLP_GROUP_DIM
GMLP_BLOCK = 128
DSA_HEADS = 8
DSA_V_DIM = 64
DSA_WIDTH = DSA_HEADS * DSA_V_DIM
DSA_NOPE_DIM = 64
DSA_ROPE_DIM = 32
DSA_Q_RANK = 256
DSA_KV_RANK = 128
IDX_HEADS = 8
IDX_DIM = 64
TOPK_MAX = 256
ROPE_THETA = 10000.0

LANES = 128
ROPE_HALF = DSA_ROPE_DIM // 2
ROPE_LANE_GROUPS = LANES // DSA_ROPE_DIM
Q_CAT = DSA_KV_RANK + LANES
W_IDX_LANE = IDX_DIM

TM = 256
TQ = 128
KB = 512
NEG_BIG = -1e30
KEY_NONE = -2 ** 31
MAX_BISECT = 40

_C_UV, _C_ZA, _C_CQ, _C_CKV = 0, 1024, 1536, 1792
_C_KR, _C_KRS, _C_SLAB, _C_ZB, _C_END = 1920, 2048, 2176, 2304, 2816
_Q_NOPE, _Q_ROPE, _Q_ROPES, _Q_IDX, _Q_END = 0, 1024, 1280, 1536, 2560

_NT = (((1,), (1,)), ((), ()))


def _bf16(a):
    return a.astype(jnp.bfloat16)


def _dot(a, b):
    return jnp.dot(a, b, preferred_element_type=jnp.float32)


def _dot_nt(a, b):
    return lax.dot_general(a, b, _NT, preferred_element_type=jnp.float32)


def _chunk_of(pos):
    return lax.shift_right_logical(pos, jnp.int32(CHUNK.bit_length() - 1))


def _sort_key(s):
    bits = pltpu.bitcast(s, jnp.int32)
    return jnp.where(bits < 0, bits ^ jnp.int32(0x7FFFFFFF), bits)


def _silu(z):
    return z / (1.0 + jnp.exp(-z))


def _gelu_exact(a):
    return 0.5 * a * (1.0 + lax.erf(a * np.float32(math.sqrt(0.5))))


def _proj_kernel(x_ref, pre_g_ref, w_all_ref, ln_g_ref, ln_b_ref, w_s_ref, b_s_ref,
                 qn_g_ref, kvn_g_ref, w_q_ref, w_uk_ref, cos_ref, sin_ref,
                 out_a_ref, gate_b_ref, qidx_ref, widx_ref, qcat_ref, kidx_ref, kcat_ref, vcat_ref):
    x = x_ref[0]
    h = x * lax.rsqrt(jnp.mean(x * x, axis=-1, keepdims=True) + EPS) * pre_g_ref[...]
    proj = _dot(_bf16(h), w_all_ref[...])

    uv = _gelu_exact(proj[:, _C_UV:_C_ZA])
    u, v = uv[:, :GMLP_WIDTH], uv[:, GMLP_WIDTH:]
    mu = jnp.mean(v, axis=-1, keepdims=True)
    vc = v - mu
    var = jnp.mean(vc * vc, axis=-1, keepdims=True)
    vn = _bf16(vc * lax.rsqrt(var + EPS) * ln_g_ref[...] + ln_b_ref[...])
    gate_a = _silu(proj[:, _C_ZA:_C_CQ])
    t_chunk = _chunk_of(lax.broadcasted_iota(jnp.int32, (GMLP_BLOCK, GMLP_BLOCK), 0))
    s_chunk = _chunk_of(lax.broadcasted_iota(jnp.int32, (GMLP_BLOCK, GMLP_BLOCK), 1))
    for g in range(GMLP_GROUPS):
        w_g = _bf16(jnp.where(s_chunk <= t_chunk, w_s_ref[g], 0.0))
        cols = slice(g * GMLP_GROUP_DIM, (g + 1) * GMLP_GROUP_DIM)
        for r in range(TM // GMLP_BLOCK):
            rows = slice(r * GMLP_BLOCK, (r + 1) * GMLP_BLOCK)
            y = _dot(w_g, vn[rows, cols]) + b_s_ref[g]
            out_a_ref[0, rows, cols] = _bf16(u[rows, cols] * y * gate_a[rows, cols])

    gate_b_ref[0] = _silu(proj[:, _C_ZB:_C_END])

    c_q = proj[:, _C_CQ:_C_CKV]
    c_q = c_q * lax.rsqrt(jnp.mean(c_q * c_q, axis=-1, keepdims=True) + EPS) * qn_g_ref[...]
    c_kv = proj[:, _C_CKV:_C_KR]
    c_kv = _bf16(c_kv * lax.rsqrt(jnp.mean(c_kv * c_kv, axis=-1, keepdims=True) + EPS) * kvn_g_ref[...])
    cos = cos_ref[...]
    sin = sin_ref[...]
    k_rope = proj[:, _C_KR:_C_KRS] * cos[:, :LANES] + proj[:, _C_KRS:_C_SLAB] * sin[:, :LANES]
    slab = proj[:, _C_SLAB:_C_ZB]
    kidx_ref[0] = _bf16(slab)
    widx_ref[0] = slab
    kcat_ref[0, :, :DSA_KV_RANK] = c_kv
    kcat_ref[0, :, DSA_KV_RANK:] = _bf16(k_rope)
    lane = lax.broadcasted_iota(jnp.int32, (TM, LANES), 1)
    vcat_ref[0, :, :DSA_KV_RANK] = c_kv
    vcat_ref[0, :, DSA_KV_RANK:] = jnp.where(lane == 0, 1.0, 0.0).astype(jnp.bfloat16)

    q_all = _dot(_bf16(c_q), w_q_ref[...])
    q_rope = q_all[:, _Q_ROPE:_Q_ROPES] * cos + q_all[:, _Q_ROPES:_Q_IDX] * sin
    scale = np.float32(1.0 / math.sqrt(DSA_NOPE_DIM + DSA_ROPE_DIM))
    for hd in range(DSA_HEADS):
        cols = slice(_Q_NOPE + hd * LANES, _Q_NOPE + (hd + 1) * LANES)
        q_lat = _dot(_bf16(q_all[:, cols]), w_uk_ref[hd])
        qcat_ref[0, hd, :, :DSA_KV_RANK] = _bf16(q_lat * scale)
        grp, sub = divmod(hd, ROPE_LANE_GROUPS)
        own = (lane >= sub * DSA_ROPE_DIM) & (lane < (sub + 1) * DSA_ROPE_DIM)
        q_r = q_rope[:, grp * LANES:(grp + 1) * LANES]
        qcat_ref[0, hd, :, DSA_KV_RANK:] = _bf16(jnp.where(own, q_r * scale, 0.0))
        icol = slice(_Q_IDX + hd * LANES, _Q_IDX + (hd + 1) * LANES)
        qidx_ref[0, hd] = _bf16(q_all[:, icol])


def _dsa_kernel(qidx_ref, widx_ref, qcat_ref, kidx_ref, kcat_ref, vcat_ref, o_ref,
                sc_ref, m_ref, acc_ref, *, top_k):
    i = pl.program_id(1)
    n_blocks = (i * TQ) // KB + 1
    n_chunks = KB // LANES
    idx_scale = np.float32(IDX_HEADS ** -0.5 * IDX_DIM ** -0.5)

    row = lax.broadcasted_iota(jnp.int32, (TQ, LANES), 0)
    lane = lax.broadcasted_iota(jnp.int32, (TQ, LANES), 1)
    q_chunk = _chunk_of(i * TQ + row)
    n_adm = (q_chunk + 1) * CHUNK

    w = widx_ref[0] * idx_scale
    w_rep = [jnp.broadcast_to(w[:, W_IDX_LANE + hd:W_IDX_LANE + hd + 1], (TQ, LANES))
             for hd in range(IDX_HEADS)]

    def score_block(kb, carry):
        grp_max = list(carry)
        k = kidx_ref[0, pl.ds(pl.multiple_of(kb * KB, KB), KB), :]
        acc = [jnp.zeros((TQ, LANES), jnp.float32) for _ in range(n_chunks)]
        for hd in range(IDX_HEADS):
            logit = _dot_nt(qidx_ref[0, hd], k)
            for c in range(n_chunks):
                acc[c] = acc[c] + w_rep[hd] * jnp.maximum(logit[:, c * LANES:(c + 1) * LANES], 0.0)
        for c in range(n_chunks):
            adm = _chunk_of(kb * KB + c * LANES + lane) <= q_chunk
            col = pl.ds(pl.multiple_of(kb * KB + c * LANES, LANES), LANES)
            sc_ref[:, col] = jnp.where(adm, _sort_key(acc[c]), KEY_NONE)
            grp_max[c % 2] = jnp.maximum(grp_max[c % 2], jnp.where(adm, acc[c], -jnp.inf))
        return tuple(grp_max)

    neg_inf = jnp.full((TQ, LANES), -jnp.inf, jnp.float32)
    grp_a, grp_b = lax.fori_loop(0, n_blocks, score_block, (neg_inf, neg_inf))

    def lane_all(op, a):
        return jnp.broadcast_to(op(a, axis=1, keepdims=True), (TQ, LANES))

    kf = np.float32(top_k)

    def count_if(pred):
        def body(kb, cnt):
            for c in range(n_chunks):
                col = pl.ds(pl.multiple_of(kb * KB + c * LANES, LANES), LANES)
                cnt = cnt + jnp.where(pred(sc_ref[:, col]), 1.0, 0.0)
            return cnt
        return lane_all(jnp.sum, lax.fori_loop(0, n_blocks, body, jnp.zeros((TQ, LANES), jnp.float32)))

    select_all = n_adm <= top_k
    lo0 = jnp.where(select_all, KEY_NONE + 1, _sort_key(lane_all(jnp.min, jnp.minimum(grp_a, grp_b))))
    hi0 = _sort_key(lane_all(jnp.max, jnp.maximum(grp_a, grp_b))) + 1
    active0 = jnp.where(select_all, 0.0, 1.0)

    def bisect_cond(state):
        it, _, _, active, _ = state
        return jnp.logical_and(it < MAX_BISECT, jnp.max(active) > 0.0)

    def bisect_body(state):
        it, lo, hi, active, tie = state
        mid = (lo & hi) + lax.shift_right_arithmetic(lo ^ hi, jnp.int32(1))
        cnt = count_if(lambda key: key >= mid)
        on = active > 0.0
        exact = cnt == kf
        ge = cnt >= kf
        stuck = mid == lo
        lo = jnp.where(on & ge, mid, lo)
        hi = jnp.where(on & ~ge, mid, hi)
        tie = jnp.where(on & stuck & ~exact, 1.0, tie)
        active = jnp.where(on & ~exact & ~stuck, 1.0, 0.0)
        return it + 1, lo, hi, active, tie

    _, thr, _, _, tie = lax.while_loop(
        bisect_cond, bisect_body, (jnp.int32(0), lo0, hi0, active0, jnp.zeros((TQ, LANES), jnp.float32)))

    @pl.when(jnp.max(tie) > 0.0)
    def _():
        tied_row = tie > 0.0
        need = kf - count_if(lambda key: key > thr)
        upto = (lax.broadcasted_iota(jnp.int32, (KB, KB), 0)
                <= lax.broadcasted_iota(jnp.int32, (KB, KB), 1)).astype(jnp.bfloat16)

        def drop_body(kb, seen):
            cols = [pl.ds(pl.multiple_of(kb * KB + c * LANES, LANES), LANES) for c in range(n_chunks)]
            band = [tied_row & (sc_ref[:, col] == thr) for col in cols]
            band_f = [jnp.where(b, 1.0, 0.0) for b in band]
            rank = _dot(jnp.concatenate([_bf16(b) for b in band_f], axis=1), upto)
            for c in range(n_chunks):
                late = band[c] & (rank[:, c * LANES:(c + 1) * LANES] + seen > need)
                sc_ref[:, cols[c]] = jnp.where(late, KEY_NONE, sc_ref[:, cols[c]])
            return seen + lane_all(jnp.sum, functools.reduce(jnp.add, band_f))
        lax.fori_loop(0, n_blocks, drop_body, jnp.zeros((TQ, LANES), jnp.float32))

    m_ref[...] = jnp.full(m_ref.shape, NEG_BIG, jnp.float32)
    acc_ref[...] = jnp.zeros(acc_ref.shape, jnp.float32)

    def attn_block(kb, carry):
        rows = pl.ds(pl.multiple_of(kb * KB, KB), KB)
        kc = kcat_ref[0, rows, :]
        vc = vcat_ref[0, rows, :]
        bias = []
        for c in range(n_chunks):
            col = pl.ds(pl.multiple_of(kb * KB + c * LANES, LANES), LANES)
            bias.append(jnp.where(sc_ref[:, col] >= thr, 0.0, NEG_BIG))
        for hd in range(DSA_HEADS):
            s = _dot_nt(qcat_ref[0, hd], kc)
            sm = [s[:, c * LANES:(c + 1) * LANES] + bias[c] for c in range(n_chunks)]
            blk_max = functools.reduce(jnp.maximum, sm)
            blk_max = jnp.broadcast_to(jnp.max(blk_max, axis=1, keepdims=True), (TQ, LANES))
            m_old = m_ref[hd]
            m_new = jnp.maximum(m_old, blk_max)
            alpha = jnp.exp(m_old - m_new)
            p = jnp.concatenate([_bf16(jnp.exp(x - m_new)) for x in sm], axis=1)
            pv = _dot(p, vc)
            acc_ref[hd] = acc_ref[hd] * jnp.concatenate([alpha, alpha], axis=1) + pv
            m_ref[hd] = m_new
        return carry

    lax.fori_loop(0, n_blocks, attn_block, 0)

    for hd in range(DSA_HEADS):
        a = acc_ref[hd]
        denom = jnp.broadcast_to(a[:, DSA_KV_RANK:DSA_KV_RANK + 1], (TQ, DSA_KV_RANK))
        o_ref[0, :, hd * DSA_KV_RANK:(hd + 1) * DSA_KV_RANK] = _bf16(a[:, :DSA_KV_RANK] / denom)


def _out_kernel(x_ref, out_a_ref, o_lat_ref, gate_b_ref, w_uv_ref, w_out_ref, post_g_ref, y_ref):
    o = _dot(o_lat_ref[0], w_uv_ref[...])
    out_b = _bf16(o * gate_b_ref[0])
    y = _dot(out_a_ref[0], w_out_ref[:GMLP_WIDTH, :]) + _dot(out_b, w_out_ref[GMLP_WIDTH:, :])
    y = y * lax.rsqrt(jnp.mean(y * y, axis=-1, keepdims=True) + EPS) * post_g_ref[...]
    y_ref[0] = x_ref[0] + y


def _full(shape):
    return pl.BlockSpec(shape, lambda b, t: (0,) * len(shape))


def _layer(x, w_in, pre_g, post_g, ln_g, ln_b, w_s, b_s, qn_g, kvn_g, w_uq, w_uk, w_uv, w_q_idx, w_out):
    B, S, D = x.shape
    assert D == D_MODEL and S % KB == 0 and S % TM == 0
    top_k = min(TOPK_MAX, S // 4)
    f32, bf16 = jnp.float32, jnp.bfloat16

    kr = w_in[:, 1920:1952]
    kr_sw = jnp.concatenate([kr[:, ROPE_HALF:], kr[:, :ROPE_HALF]], axis=1)
    w_all = jnp.concatenate([
        w_in[:, :1920], jnp.tile(kr, (1, ROPE_LANE_GROUPS)), jnp.tile(kr_sw, (1, ROPE_LANE_GROUPS)),
        w_in[:, 1952:2024], jnp.zeros((D, LANES - IDX_DIM - IDX_HEADS), f32), w_in[:, 2024:]], axis=1).astype(bf16)
    wq3 = w_uq.reshape(DSA_Q_RANK, DSA_HEADS, DSA_NOPE_DIM + DSA_ROPE_DIM)
    nope = jnp.pad(wq3[:, :, :DSA_NOPE_DIM], ((0, 0), (0, 0), (0, LANES - DSA_NOPE_DIM)))
    rope = wq3[:, :, DSA_NOPE_DIM:]
    rope_sw = jnp.concatenate([rope[:, :, ROPE_HALF:], rope[:, :, :ROPE_HALF]], axis=2)
    wqi = jnp.pad(w_q_idx.reshape(DSA_Q_RANK, IDX_HEADS, IDX_DIM), ((0, 0), (0, 0), (0, LANES - IDX_DIM)))
    w_q = jnp.concatenate([nope.reshape(DSA_Q_RANK, -1), rope.reshape(DSA_Q_RANK, -1),
                           rope_sw.reshape(DSA_Q_RANK, -1), wqi.reshape(DSA_Q_RANK, -1)], axis=1).astype(bf16)
    w_uk_t = jnp.pad(jnp.transpose(w_uk, (1, 2, 0)), ((0, 0), (0, LANES - DSA_NOPE_DIM), (0, 0))).astype(bf16)
    eye = jnp.eye(DSA_HEADS, dtype=f32)
    w_uv_bd = (jnp.transpose(w_uv, (1, 0, 2))[:, :, None, :] * eye[:, None, :, None]).reshape(
        DSA_HEADS * DSA_KV_RANK, DSA_WIDTH).astype(bf16)
    b_s_b = jnp.broadcast_to(b_s[:, :, None], (GMLP_GROUPS, GMLP_BLOCK, GMLP_GROUP_DIM))

    pos = jnp.arange(S, dtype=f32)
    inv_freq = ROPE_THETA ** (-jnp.arange(0, DSA_ROPE_DIM, 2, dtype=f32) / DSA_ROPE_DIM)
    ang = pos[:, None] * inv_freq[None, :]
    cos_t = jnp.tile(jnp.concatenate([jnp.cos(ang), jnp.cos(ang)], axis=1), (1, DSA_HEADS))
    sin_t = jnp.tile(jnp.concatenate([-jnp.sin(ang), jnp.sin(ang)], axis=1), (1, DSA_HEADS))

    row2 = lambda a: a.reshape(1, -1)
    tok = lambda width: pl.BlockSpec((1, TM, width), lambda b, t: (b, t, 0))
    hm = lambda width: pl.BlockSpec((1, DSA_HEADS, TM, width), lambda b, t: (b, 0, t, 0))

    out_a, gate_b, qidx, widx, qcat, kidx, kcat, vcat = pl.pallas_call(
        _proj_kernel,
        grid=(B, S // TM),
        in_specs=[tok(D), _full((1, D)), _full((D, _C_END)), _full((1, GMLP_WIDTH)), _full((1, GMLP_WIDTH)),
                  _full((GMLP_GROUPS, GMLP_BLOCK, GMLP_BLOCK)), _full((GMLP_GROUPS, GMLP_BLOCK, GMLP_GROUP_DIM)),
                  _full((1, DSA_Q_RANK)), _full((1, DSA_KV_RANK)), _full((DSA_Q_RANK, _Q_END)),
                  _full((DSA_HEADS, LANES, DSA_KV_RANK)),
                  pl.BlockSpec((TM, 2 * LANES), lambda b, t: (t, 0)),
                  pl.BlockSpec((TM, 2 * LANES), lambda b, t: (t, 0))],
        out_specs=[tok(GMLP_WIDTH), tok(DSA_WIDTH), hm(LANES), tok(LANES), hm(Q_CAT),
                   tok(LANES), tok(Q_CAT), tok(Q_CAT)],
        out_shape=[jax.ShapeDtypeStruct((B, S, GMLP_WIDTH), bf16),
                   jax.ShapeDtypeStruct((B, S, DSA_WIDTH), f32),
                   jax.ShapeDtypeStruct((B, DSA_HEADS, S, LANES), bf16),
                   jax.ShapeDtypeStruct((B, S, LANES), f32),
                   jax.ShapeDtypeStruct((B, DSA_HEADS, S, Q_CAT), bf16),
                   jax.ShapeDtypeStruct((B, S, LANES), bf16),
                   jax.ShapeDtypeStruct((B, S, Q_CAT), bf16),
                   jax.ShapeDtypeStruct((B, S, Q_CAT), bf16)],
        compiler_params=pltpu.CompilerParams(
            dimension_semantics=("arbitrary", "arbitrary"), vmem_limit_bytes=48 * 1024 * 1024),
        name="proj_gmlp_dsa_prep",
    )(x, row2(pre_g), w_all, row2(ln_g), row2(ln_b), w_s, b_s_b, row2(qn_g), row2(kvn_g), w_q, w_uk_t,
      cos_t, sin_t)

    qt = lambda width: pl.BlockSpec((1, DSA_HEADS, TQ, width), lambda b, t: (b, 0, t, 0))
    seq = lambda width: pl.BlockSpec((1, S, width), lambda b, t: (b, 0, 0))
    o_lat = pl.pallas_call(
        functools.partial(_dsa_kernel, top_k=top_k),
        grid=(B, S // TQ),
        in_specs=[qt(LANES), pl.BlockSpec((1, TQ, LANES), lambda b, t: (b, t, 0)), qt(Q_CAT),
                  seq(LANES), seq(Q_CAT), seq(Q_CAT)],
        out_specs=pl.BlockSpec((1, TQ, DSA_HEADS * DSA_KV_RANK), lambda b, t: (b, t, 0)),
        out_shape=jax.ShapeDtypeStruct((B, S, DSA_HEADS * DSA_KV_RANK), bf16),
        scratch_shapes=[pltpu.VMEM((TQ, S), jnp.int32),
                        pltpu.VMEM((DSA_HEADS, TQ, LANES), f32),
                        pltpu.VMEM((DSA_HEADS, TQ, Q_CAT), f32)],
        compiler_params=pltpu.CompilerParams(
            dimension_semantics=("arbitrary", "arbitrary"), vmem_limit_bytes=48 * 1024 * 1024),
        name="dsa_index_select_attend",
    )(qidx, widx, qcat, kidx, kcat, vcat)

    return pl.pallas_call(
        _out_kernel,
        grid=(B, S // TM),
        in_specs=[tok(D), tok(GMLP_WIDTH), tok(DSA_HEADS * DSA_KV_RANK), tok(DSA_WIDTH),
                  _full((DSA_HEADS * DSA_KV_RANK, DSA_WIDTH)), _full((GMLP_WIDTH + DSA_WIDTH, D)), _full((1, D))],
        out_specs=tok(D),
        out_shape=jax.ShapeDtypeStruct((B, S, D), f32),
        compiler_params=pltpu.CompilerParams(
            dimension_semantics=("arbitrary", "arbitrary"), vmem_limit_bytes=32 * 1024 * 1024),
        name="out_proj_norm_residual",
    )(x, out_a, o_lat, gate_b, w_uv_bd, w_out.astype(bf16), row2(post_g))


def kernel(x, w_in, pre_norm_g, post_norm_g, gmlp_ln_g, gmlp_ln_b, gmlp_w_s, gmlp_b_s, dsa_q_norm_g, dsa_kv_norm_g, dsa_w_uq, dsa_w_uk, dsa_w_uv, dsa_w_q_idx, w_out):
    for l in range(w_in.shape[0]):
        x = _layer(x, w_in[l], pre_norm_g[l], post_norm_g[l], gmlp_ln_g[l], gmlp_ln_b[l], gmlp_w_s[l],
                   gmlp_b_s[l], dsa_q_norm_g[l], dsa_kv_norm_g[l], dsa_w_uq[l], dsa_w_uk[l], dsa_w_uv[l],
                   dsa_w_q_idx[l], w_out[l])
    return x
```

```python
import functools
import math

import jax
import jax.numpy as jnp
import numpy as np
from jax import lax
from jax.experimental import pallas as pl
from jax.experimental.pallas import tpu as pltpu

D_MODEL = 1024
CHUNK = 64
EPS = 1e-6
GMLP_GROUPS = 4
GMLP_GROUP_DIM = 128
GMLP_WIDTH = GMLP_GROUPS * GMLP_GROUP_DIM
GMLP_BLOCK = 128
DSA_HEADS = 8
DSA_V_DIM = 64
DSA_WIDTH = DSA_HEADS * DSA_V_DIM
DSA_NOPE_DIM = 64
DSA_ROPE_DIM = 32
DSA_Q_RANK = 256
DSA_KV_RANK = 128
IDX_HEADS = 8
IDX_DIM = 64
TOPK_MAX = 256
ROPE_THETA = 10000.0

LANES = 128
ROPE_HALF = DSA_ROPE_DIM // 2
ROPE_LANE_GROUPS = LANES // DSA_ROPE_DIM
Q_CAT = DSA_KV_RANK + LANES
W_IDX_LANE = IDX_DIM

TM = 256
TQ = 128
KB = 512
NEG_BIG = -1e30
ROW_SUM_MIN = 1e-26
BOUND_SLACK = 1.02
HEAD_GROUP = 4
KEY_NONE = -2 ** 31
MAX_BISECT = 40

_C_UV, _C_ZA, _C_CQ, _C_CKV = 0, 1024, 1536, 1792
_C_KR, _C_KRS, _C_SLAB, _C_ZB, _C_END = 1920, 2048, 2176, 2304, 2816
_Q_NOPE, _Q_ROPE, _Q_ROPES, _Q_IDX, _Q_END = 0, 1024, 1280, 1536, 2560

_NT = (((1,), (1,)), ((), ()))


def _bf16(a):
    return a.astype(jnp.bfloat16)


def _dot(a, b):
    return jnp.dot(a, b, preferred_element_type=jnp.float32)


def _dot_nt(a, b):
    return lax.dot_general(a, b, _NT, preferred_element_type=jnp.float32)


def _chunk_of(pos):
    return lax.shift_right_logical(pos, jnp.int32(CHUNK.bit_length() - 1))


def _sort_key(s):
    bits = pltpu.bitcast(s, jnp.int32)
    return jnp.where(bits < 0, bits ^ jnp.int32(0x7FFFFFFF), bits)


def _silu(z):
    return z / (1.0 + jnp.exp(-z))


def _gelu_exact(a):
    return 0.5 * a * (1.0 + lax.erf(a * np.float32(math.sqrt(0.5))))


def _proj_kernel(x_ref, pre_g_ref, w_all_ref, ln_g_ref, ln_b_ref, w_s_ref, b_s_ref,
                 qn_g_ref, kvn_g_ref, w_q_ref, w_uk_ref, cos_ref, sin_ref,
                 out_a_ref, gate_b_ref, qidx_ref, widx_ref, qcat_ref, kidx_ref, kcat_ref, vcat_ref):
    x = x_ref[0]
    h = x * lax.rsqrt(jnp.mean(x * x, axis=-1, keepdims=True) + EPS) * pre_g_ref[...]
    proj = _dot(_bf16(h), w_all_ref[...])

    uv = _gelu_exact(proj[:, _C_UV:_C_ZA])
    u, v = uv[:, :GMLP_WIDTH], uv[:, GMLP_WIDTH:]
    mu = jnp.mean(v, axis=-1, keepdims=True)
    vc = v - mu
    var = jnp.mean(vc * vc, axis=-1, keepdims=True)
    vn = _bf16(vc * lax.rsqrt(var + EPS) * ln_g_ref[...] + ln_b_ref[...])
    gate_a = _silu(proj[:, _C_ZA:_C_CQ])
    t_chunk = _chunk_of(lax.broadcasted_iota(jnp.int32, (GMLP_BLOCK, GMLP_BLOCK), 0))
    s_chunk = _chunk_of(lax.broadcasted_iota(jnp.int32, (GMLP_BLOCK, GMLP_BLOCK), 1))
    for g in range(GMLP_GROUPS):
        w_g = _bf16(jnp.where(s_chunk <= t_chunk, w_s_ref[g], 0.0))
        cols = slice(g * GMLP_GROUP_DIM, (g + 1) * GMLP_GROUP_DIM)
        for r in range(TM // GMLP_BLOCK):
            rows = slice(r * GMLP_BLOCK, (r + 1) * GMLP_BLOCK)
            y = _dot(w_g, vn[rows, cols]) + b_s_ref[g]
            out_a_ref[0, rows, cols] = _bf16(u[rows, cols] * y * gate_a[rows, cols])

    gate_b_ref[0] = _silu(proj[:, _C_ZB:_C_END])

    c_q = proj[:, _C_CQ:_C_CKV]
    c_q = c_q * lax.rsqrt(jnp.mean(c_q * c_q, axis=-1, keepdims=True) + EPS) * qn_g_ref[...]
    c_kv = proj[:, _C_CKV:_C_KR]
    c_kv = _bf16(c_kv * lax.rsqrt(jnp.mean(c_kv * c_kv, axis=-1, keepdims=True) + EPS) * kvn_g_ref[...])
    cos = cos_ref[...]
    sin = sin_ref[...]
    k_rope = proj[:, _C_KR:_C_KRS] * cos[:, :LANES] + proj[:, _C_KRS:_C_SLAB] * sin[:, :LANES]
    slab = proj[:, _C_SLAB:_C_ZB]
    kidx_ref[0] = _bf16(slab)
    widx_ref[0] = slab
    kcat_ref[0, :, :DSA_KV_RANK] = c_kv
    kcat_ref[0, :, DSA_KV_RANK:] = _bf16(k_rope)
    lane = lax.broadcasted_iota(jnp.int32, (TM, LANES), 1)
    vcat_ref[0, :, :DSA_KV_RANK] = c_kv
    vcat_ref[0, :, DSA_KV_RANK:] = jnp.where(lane == 0, 1.0, 0.0).astype(jnp.bfloat16)

    q_all = _dot(_bf16(c_q), w_q_ref[...])
    q_rope = q_all[:, _Q_ROPE:_Q_ROPES] * cos + q_all[:, _Q_ROPES:_Q_IDX] * sin
    scale = np.float32(1.0 / math.sqrt(DSA_NOPE_DIM + DSA_ROPE_DIM))
    for hd in range(DSA_HEADS):
        cols = slice(_Q_NOPE + hd * LANES, _Q_NOPE + (hd + 1) * LANES)
        q_lat = _dot(_bf16(q_all[:, cols]), w_uk_ref[hd])
        grp, sub = divmod(hd, ROPE_LANE_GROUPS)
        own = (lane >= sub * DSA_ROPE_DIM) & (lane < (sub + 1) * DSA_ROPE_DIM)
        q_r = jnp.where(own, q_rope[:, grp * LANES:(grp + 1) * LANES] * scale, 0.0)
        for t in range(TM // TQ):
            src = slice(t * TQ, (t + 1) * TQ)
            dst = slice(hd * TQ, (hd + 1) * TQ)
            qcat_ref[0, t, dst, :DSA_KV_RANK] = _bf16(q_lat[src] * scale)
            qcat_ref[0, t, dst, DSA_KV_RANK:] = _bf16(q_r[src])
        icol = slice(_Q_IDX + hd * LANES, _Q_IDX + (hd + 1) * LANES)
        qidx_ref[0, hd] = _bf16(q_all[:, icol])


def _dsa_kernel(qidx_ref, widx_ref, qcat_ref, kidx_ref, kcat_ref, vcat_ref, o_ref,
                sc_ref, m_ref, acc_ref, kmax_ref, *, top_k):
    i = pl.program_id(1)
    n_blocks = (i * TQ) // KB + 1
    n_chunks = KB // LANES
    idx_scale = np.float32(IDX_HEADS ** -0.5 * IDX_DIM ** -0.5)

    row = lax.broadcasted_iota(jnp.int32, (TQ, LANES), 0)
    lane = lax.broadcasted_iota(jnp.int32, (TQ, LANES), 1)
    q_chunk = _chunk_of(i * TQ + row)
    n_adm = (q_chunk + 1) * CHUNK

    w = widx_ref[0] * idx_scale
    w_rep = [jnp.broadcast_to(w[:, W_IDX_LANE + hd:W_IDX_LANE + hd + 1], (TQ, LANES))
             for hd in range(IDX_HEADS)]

    def score_block(kb, carry):
        grp_max = list(carry)
        k = kidx_ref[0, pl.ds(pl.multiple_of(kb * KB, KB), KB), :]
        acc = [jnp.zeros((TQ, LANES), jnp.float32) for _ in range(n_chunks)]
        for hd in range(IDX_HEADS):
            logit = _dot_nt(qidx_ref[0, hd], k)
            for c in range(n_chunks):
                acc[c] = acc[c] + w_rep[hd] * jnp.maximum(logit[:, c * LANES:(c + 1) * LANES], 0.0)
        for c in range(n_chunks):
            adm = _chunk_of(kb * KB + c * LANES + lane) <= q_chunk
            col = pl.ds(pl.multiple_of(kb * KB + c * LANES, LANES), LANES)
            sc_ref[:, col] = jnp.where(adm, _sort_key(acc[c]), KEY_NONE)
            grp_max[c % 2] = jnp.maximum(grp_max[c % 2], jnp.where(adm, acc[c], -jnp.inf))
        return tuple(grp_max)

    neg_inf = jnp.full((TQ, LANES), -jnp.inf, jnp.float32)
    grp_a, grp_b = lax.fori_loop(0, n_blocks, score_block, (neg_inf, neg_inf))

    def lane_all(op, a):
        return jnp.broadcast_to(op(a, axis=1, keepdims=True), (TQ, LANES))

    kf = np.float32(top_k)

    def count_if(pred):
        def body(kb, cnt):
            for c in range(n_chunks):
                col = pl.ds(pl.multiple_of(kb * KB + c * LANES, LANES), LANES)
                cnt = cnt + jnp.where(pred(sc_ref[:, col]), 1.0, 0.0)
            return cnt
        return lane_all(jnp.sum, lax.fori_loop(0, n_blocks, body, jnp.zeros((TQ, LANES), jnp.float32)))

    select_all = n_adm <= top_k
    lo0 = jnp.where(select_all, KEY_NONE + 1, _sort_key(lane_all(jnp.min, jnp.minimum(grp_a, grp_b))))
    hi0 = _sort_key(lane_all(jnp.max, jnp.maximum(grp_a, grp_b))) + 1
    active0 = jnp.where(select_all, 0.0, 1.0)

    def bisect_cond(state):
        it, _, _, active, _ = state
        return jnp.logical_and(it < MAX_BISECT, jnp.max(active) > 0.0)

    def bisect_body(state):
        it, lo, hi, active, tie = state
        mid = (lo & hi) + lax.shift_right_arithmetic(lo ^ hi, jnp.int32(1))
        cnt = count_if(lambda key: key >= mid)
        on = active > 0.0
        exact = cnt == kf
        ge = cnt >= kf
        stuck = mid == lo
        lo = jnp.where(on & ge, mid, lo)
        hi = jnp.where(on & ~ge, mid, hi)
        tie = jnp.where(on & stuck & ~exact, 1.0, tie)
        active = jnp.where(on & ~exact & ~stuck, 1.0, 0.0)
        return it + 1, lo, hi, active, tie

    _, thr, _, _, tie = lax.while_loop(
        bisect_cond, bisect_body, (jnp.int32(0), lo0, hi0, active0, jnp.zeros((TQ, LANES), jnp.float32)))

    @pl.when(jnp.max(tie) > 0.0)
    def _():
        tied_row = tie > 0.0
        need = kf - count_if(lambda key: key > thr)
        upto = (lax.broadcasted_iota(jnp.int32, (KB, KB), 0)
                <= lax.broadcasted_iota(jnp.int32, (KB, KB), 1)).astype(jnp.bfloat16)

        def drop_body(kb, seen):
            cols = [pl.ds(pl.multiple_of(kb * KB + c * LANES, LANES), LANES) for c in range(n_chunks)]
            band = [tied_row & (sc_ref[:, col] == thr) for col in cols]
            band_f = [jnp.where(b, 1.0, 0.0) for b in band]
            rank = _dot(jnp.concatenate([_bf16(b) for b in band_f], axis=1), upto)
            for c in range(n_chunks):
                late = band[c] & (rank[:, c * LANES:(c + 1) * LANES] + seen > need)
                sc_ref[:, cols[c]] = jnp.where(late, KEY_NONE, sc_ref[:, cols[c]])
            return seen + lane_all(jnp.sum, functools.reduce(jnp.add, band_f))
        lax.fori_loop(0, n_blocks, drop_body, jnp.zeros((TQ, LANES), jnp.float32))

    @pl.when(i == 0)
    def _():
        def norm_block(kb, run):
            kc = kcat_ref[0, pl.ds(pl.multiple_of(kb * KB, KB), KB), :].astype(jnp.float32)
            return jnp.maximum(run, jnp.sum(kc * kc, axis=1, keepdims=True))
        run = lax.fori_loop(0, kcat_ref.shape[1] // KB, norm_block, jnp.zeros((KB, 1), jnp.float32))
        kmax_ref[...] = jnp.broadcast_to(jnp.max(run, axis=0, keepdims=True), kmax_ref.shape)

    k_norm2 = kmax_ref[0:1, :]
    q_all = qcat_ref[0, 0]
    q_norm2 = _dot(q_all * q_all, jnp.ones((Q_CAT, LANES), jnp.bfloat16))
    qk2 = jnp.maximum(q_norm2 * k_norm2, 1e-30)
    m_ref[...] = qk2 * lax.rsqrt(qk2) * BOUND_SLACK

    def key_bias(kb):
        cols = [pl.ds(pl.multiple_of(kb * KB + c * LANES, LANES), LANES) for c in range(n_chunks)]
        bias = [jnp.where(sc_ref[:, col] >= thr, 0.0, NEG_BIG) for col in cols]
        return [jnp.concatenate([b] * HEAD_GROUP, axis=0) for b in bias]

    group_rows = [slice(g * HEAD_GROUP * TQ, (g + 1) * HEAD_GROUP * TQ) for g in range(DSA_HEADS // HEAD_GROUP)]

    def attend():
        acc_ref[...] = jnp.zeros(acc_ref.shape, jnp.float32)

        def attn_block(kb, carry):
            rows = pl.ds(pl.multiple_of(kb * KB, KB), KB)
            kc = kcat_ref[0, rows, :]
            vc = vcat_ref[0, rows, :]
            bias = key_bias(kb)
            for grp in group_rows:
                s = _dot_nt(qcat_ref[0, 0, grp, :], kc)
                m = m_ref[grp, :]
                p = [_bf16(jnp.exp(s[:, c * LANES:(c + 1) * LANES] + bias[c] - m)) for c in range(n_chunks)]
                acc_ref[grp, :] += _dot(jnp.concatenate(p, axis=1), vc)
            return carry
        lax.fori_loop(0, n_blocks, attn_block, 0)
        return jnp.min(acc_ref[:, DSA_KV_RANK:DSA_KV_RANK + 1])

    @pl.when(attend() < ROW_SUM_MIN)
    def _():
        m_ref[...] = jnp.full(m_ref.shape, NEG_BIG, jnp.float32)

        def max_block(kb, carry):
            kc = kcat_ref[0, pl.ds(pl.multiple_of(kb * KB, KB), KB), :]
            bias = key_bias(kb)
            for grp in group_rows:
                s = _dot_nt(qcat_ref[0, 0, grp, :], kc)
                sm = [s[:, c * LANES:(c + 1) * LANES] + bias[c] for c in range(n_chunks)]
                m_ref[grp, :] = jnp.maximum(m_ref[grp, :], functools.reduce(jnp.maximum, sm))
            return carry
        lax.fori_loop(0, n_blocks, max_block, 0)
        m_all = m_ref[...]
        m_ref[...] = jnp.broadcast_to(jnp.max(m_all, axis=1, keepdims=True), m_all.shape)
        attend()

    for hd in range(DSA_HEADS):
        a = acc_ref[hd * TQ:(hd + 1) * TQ, :]
        denom = jnp.broadcast_to(a[:, DSA_KV_RANK:DSA_KV_RANK + 1], (TQ, DSA_KV_RANK))
        o_ref[0, :, hd * DSA_KV_RANK:(hd + 1) * DSA_KV_RANK] = _bf16(a[:, :DSA_KV_RANK] / denom)


def _out_kernel(x_ref, out_a_ref, o_lat_ref, gate_b_ref, w_uv_ref, w_out_ref, post_g_ref, y_ref):
    o = _dot(o_lat_ref[0], w_uv_ref[...])
    out_b = _bf16(o * gate_b_ref[0])
    y = _dot(out_a_ref[0], w_out_ref[:GMLP_WIDTH, :]) + _dot(out_b, w_out_ref[GMLP_WIDTH:, :])
    y = y * lax.rsqrt(jnp.mean(y * y, axis=-1, keepdims=True) + EPS) * post_g_ref[...]
    y_ref[0] = x_ref[0] + y


def _full(shape):
    return pl.BlockSpec(shape, lambda b, t: (0,) * len(shape))


def _layer(x, w_in, pre_g, post_g, ln_g, ln_b, w_s, b_s, qn_g, kvn_g, w_uq, w_uk, w_uv, w_q_idx, w_out):
    B, S, D = x.shape
    assert D == D_MODEL and S % KB == 0 and S % TM == 0
    top_k = min(TOPK_MAX, S // 4)
    f32, bf16 = jnp.float32, jnp.bfloat16

    kr = w_in[:, 1920:1952]
    kr_sw = jnp.concatenate([kr[:, ROPE_HALF:], kr[:, :ROPE_HALF]], axis=1)
    w_all = jnp.concatenate([
        w_in[:, :1920], jnp.tile(kr, (1, ROPE_LANE_GROUPS)), jnp.tile(kr_sw, (1, ROPE_LANE_GROUPS)),
        w_in[:, 1952:2024], jnp.zeros((D, LANES - IDX_DIM - IDX_HEADS), f32), w_in[:, 2024:]], axis=1).astype(bf16)
    wq3 = w_uq.reshape(DSA_Q_RANK, DSA_HEADS, DSA_NOPE_DIM + DSA_ROPE_DIM)
    nope = jnp.pad(wq3[:, :, :DSA_NOPE_DIM], ((0, 0), (0, 0), (0, LANES - DSA_NOPE_DIM)))
    rope = wq3[:, :, DSA_NOPE_DIM:]
    rope_sw = jnp.concatenate([rope[:, :, ROPE_HALF:], rope[:, :, :ROPE_HALF]], axis=2)
    wqi = jnp.pad(w_q_idx.reshape(DSA_Q_RANK, IDX_HEADS, IDX_DIM), ((0, 0), (0, 0), (0, LANES - IDX_DIM)))
    w_q = jnp.concatenate([nope.reshape(DSA_Q_RANK, -1), rope.reshape(DSA_Q_RANK, -1),
                           rope_sw.reshape(DSA_Q_RANK, -1), wqi.reshape(DSA_Q_RANK, -1)], axis=1).astype(bf16)
    w_uk_t = jnp.pad(jnp.transpose(w_uk, (1, 2, 0)), ((0, 0), (0, LANES - DSA_NOPE_DIM), (0, 0))).astype(bf16)
    eye = jnp.eye(DSA_HEADS, dtype=f32)
    w_uv_bd = (jnp.transpose(w_uv, (1, 0, 2))[:, :, None, :] * eye[:, None, :, None]).reshape(
        DSA_HEADS * DSA_KV_RANK, DSA_WIDTH).astype(bf16)
    b_s_b = jnp.broadcast_to(b_s[:, :, None], (GMLP_GROUPS, GMLP_BLOCK, GMLP_GROUP_DIM))

    pos = jnp.arange(S, dtype=f32)
    inv_freq = ROPE_THETA ** (-jnp.arange(0, DSA_ROPE_DIM, 2, dtype=f32) / DSA_ROPE_DIM)
    ang = pos[:, None] * inv_freq[None, :]
    cos_t = jnp.tile(jnp.concatenate([jnp.cos(ang), jnp.cos(ang)], axis=1), (1, DSA_HEADS))
    sin_t = jnp.tile(jnp.concatenate([-jnp.sin(ang), jnp.sin(ang)], axis=1), (1, DSA_HEADS))

    row2 = lambda a: a.reshape(1, -1)
    tok = lambda width: pl.BlockSpec((1, TM, width), lambda b, t: (b, t, 0))
    hm = lambda width: pl.BlockSpec((1, DSA_HEADS, TM, width), lambda b, t: (b, 0, t, 0))

    out_a, gate_b, qidx, widx, qcat, kidx, kcat, vcat = pl.pallas_call(
        _proj_kernel,
        grid=(B, S // TM),
        in_specs=[tok(D), _full((1, D)), _full((D, _C_END)), _full((1, GMLP_WIDTH)), _full((1, GMLP_WIDTH)),
                  _full((GMLP_GROUPS, GMLP_BLOCK, GMLP_BLOCK)), _full((GMLP_GROUPS, GMLP_BLOCK, GMLP_GROUP_DIM)),
                  _full((1, DSA_Q_RANK)), _full((1, DSA_KV_RANK)), _full((DSA_Q_RANK, _Q_END)),
                  _full((DSA_HEADS, LANES, DSA_KV_RANK)),
                  pl.BlockSpec((TM, 2 * LANES), lambda b, t: (t, 0)),
                  pl.BlockSpec((TM, 2 * LANES), lambda b, t: (t, 0))],
        out_specs=[tok(GMLP_WIDTH), tok(DSA_WIDTH), hm(LANES), tok(LANES),
                   pl.BlockSpec((1, TM // TQ, DSA_HEADS * TQ, Q_CAT), lambda b, t: (b, t, 0, 0)),
                   tok(LANES), tok(Q_CAT), tok(Q_CAT)],
        out_shape=[jax.ShapeDtypeStruct((B, S, GMLP_WIDTH), bf16),
                   jax.ShapeDtypeStruct((B, S, DSA_WIDTH), f32),
                   jax.ShapeDtypeStruct((B, DSA_HEADS, S, LANES), bf16),
                   jax.ShapeDtypeStruct((B, S, LANES), f32),
                   jax.ShapeDtypeStruct((B, S // TQ, DSA_HEADS * TQ, Q_CAT), bf16),
                   jax.ShapeDtypeStruct((B, S, LANES), bf16),
                   jax.ShapeDtypeStruct((B, S, Q_CAT), bf16),
                   jax.ShapeDtypeStruct((B, S, Q_CAT), bf16)],
        compiler_params=pltpu.CompilerParams(
            dimension_semantics=("arbitrary", "arbitrary"), vmem_limit_bytes=48 * 1024 * 1024),
        name="proj_gmlp_dsa_prep",
    )(x, row2(pre_g), w_all, row2(ln_g), row2(ln_b), w_s, b_s_b, row2(qn_g), row2(kvn_g), w_q, w_uk_t,
      cos_t, sin_t)

    qt = lambda width: pl.BlockSpec((1, DSA_HEADS, TQ, width), lambda b, t: (b, 0, t, 0))
    seq = lambda width: pl.BlockSpec((1, S, width), lambda b, t: (b, 0, 0))
    o_lat = pl.pallas_call(
        functools.partial(_dsa_kernel, top_k=top_k),
        grid=(B, S // TQ),
        in_specs=[qt(LANES), pl.BlockSpec((1, TQ, LANES), lambda b, t: (b, t, 0)),
                  pl.BlockSpec((1, 1, DSA_HEADS * TQ, Q_CAT), lambda b, t: (b, t, 0, 0)),
                  seq(LANES), seq(Q_CAT), seq(Q_CAT)],
        out_specs=pl.BlockSpec((1, TQ, DSA_HEADS * DSA_KV_RANK), lambda b, t: (b, t, 0)),
        out_shape=jax.ShapeDtypeStruct((B, S, DSA_HEADS * DSA_KV_RANK), bf16),
        scratch_shapes=[pltpu.VMEM((TQ, S), jnp.int32),
                        pltpu.VMEM((DSA_HEADS * TQ, LANES), f32),
                        pltpu.VMEM((DSA_HEADS * TQ, Q_CAT), f32),
                        pltpu.VMEM((8, LANES), f32)],
        compiler_params=pltpu.CompilerParams(
            dimension_semantics=("arbitrary", "arbitrary"), vmem_limit_bytes=48 * 1024 * 1024),
        name="dsa_index_select_attend",
    )(qidx, widx, qcat, kidx, kcat, vcat)

    return pl.pallas_call(
        _out_kernel,
        grid=(B, S // TM),
        in_specs=[tok(D), tok(GMLP_WIDTH), tok(DSA_HEADS * DSA_KV_RANK), tok(DSA_WIDTH),
                  _full((DSA_HEADS * DSA_KV_RANK, DSA_WIDTH)), _full((GMLP_WIDTH + DSA_WIDTH, D)), _full((1, D))],
        out_specs=tok(D),
        out_shape=jax.ShapeDtypeStruct((B, S, D), f32),
        compiler_params=pltpu.CompilerParams(
            dimension_semantics=("arbitrary", "arbitrary"), vmem_limit_bytes=32 * 1024 * 1024),
        name="out_proj_norm_residual",
    )(x, out_a, o_lat, gate_b, w_uv_bd, w_out.astype(bf16), row2(post_g))


def kernel(x, w_in, pre_norm_g, post_norm_g, gmlp_ln_g, gmlp_ln_b, gmlp_w_s, gmlp_b_s, dsa_q_norm_g, dsa_kv_norm_g, dsa_w_uq, dsa_w_uk, dsa_w_uv, dsa_w_q_idx, w_out):
    for l in range(w_in.shape[0]):
        x = _layer(x, w_in[l], pre_norm_g[l], post_norm_g[l], gmlp_ln_g[l], gmlp_ln_b[l], gmlp_w_s[l],
                   gmlp_b_s[l], dsa_q_norm_g[l], dsa_kv_norm_g[l], dsa_w_uq[l], dsa_w_uk[l], dsa_w_uv[l],
                   dsa_w_q_idx[l], w_out[l])
    return x
```

```python
import functools
import math

import jax
import jax.numpy as jnp
import numpy as np
from jax import lax
from jax.experimental import pallas as pl
from jax.experimental.pallas import tpu as pltpu

D_MODEL = 1024
CHUNK = 64
EPS = 1e-6
GMLP_GROUPS = 4
GMLP_GROUP_DIM = 128
GMLP_WIDTH = GMLP_GROUPS * GMLP_GROUP_DIM
GMLP_BLOCK = 128
DSA_HEADS = 8
DSA_V_DIM = 64
DSA_WIDTH = DSA_HEADS * DSA_V_DIM
DSA_NOPE_DIM = 64
DSA_ROPE_DIM = 32
DSA_Q_RANK = 256
DSA_KV_RANK = 128
IDX_HEADS = 8
IDX_DIM = 64
TOPK_MAX = 256
ROPE_THETA = 10000.0

LANES = 128
ROPE_HALF = DSA_ROPE_DIM // 2
ROPE_LANE_GROUPS = LANES // DSA_ROPE_DIM
Q_CAT = DSA_KV_RANK + LANES
W_IDX_LANE = IDX_DIM

TM = 256
TQ = 256
ROW_TILE = 128
KB = 512
HEAD_GROUP = 1
NEG_BIG = -1e30
ROW_SUM_MIN = 1e-26
BOUND_SLACK = 1.02
KEY_NONE = -2 ** 31
MAX_BISECT = 40

_C_UV, _C_ZA, _C_CQ, _C_CKV = 0, 1024, 1536, 1792
_C_KR, _C_KRS, _C_SLAB, _C_ZB, _C_END = 1920, 2048, 2176, 2304, 2816
_Q_NOPE, _Q_ROPE, _Q_ROPES, _Q_IDX, _Q_END = 0, 1024, 1280, 1536, 2560

_NT = (((1,), (1,)), ((), ()))


def _bf16(a):
    return a.astype(jnp.bfloat16)


def _dot(a, b):
    return jnp.dot(a, b, preferred_element_type=jnp.float32)


def _dot_nt(a, b):
    return lax.dot_general(a, b, _NT, preferred_element_type=jnp.float32)


def _chunk_of(pos):
    return lax.shift_right_logical(pos, jnp.int32(CHUNK.bit_length() - 1))


def _sort_key(s):
    bits = pltpu.bitcast(s, jnp.int32)
    return jnp.where(bits < 0, bits ^ jnp.int32(0x7FFFFFFF), bits)


def _silu(z):
    return z / (1.0 + jnp.exp(-z))


def _gelu_exact(a):
    return 0.5 * a * (1.0 + lax.erf(a * np.float32(math.sqrt(0.5))))


def _proj_kernel(x_ref, pre_g_ref, w_all_ref, ln_g_ref, ln_b_ref, w_s_ref, b_s_ref,
                 qn_g_ref, kvn_g_ref, w_q_ref, w_uk_ref, cos_ref, sin_ref,
                 out_a_ref, gate_b_ref, qidx_ref, widx_ref, qcat_ref, kidx_ref, kcat_ref, vcat_ref):
    x = x_ref[0]
    h = x * lax.rsqrt(jnp.mean(x * x, axis=-1, keepdims=True) + EPS) * pre_g_ref[...]
    proj = _dot(_bf16(h), w_all_ref[...])

    uv = _gelu_exact(proj[:, _C_UV:_C_ZA])
    u, v = uv[:, :GMLP_WIDTH], uv[:, GMLP_WIDTH:]
    mu = jnp.mean(v, axis=-1, keepdims=True)
    vc = v - mu
    var = jnp.mean(vc * vc, axis=-1, keepdims=True)
    vn = _bf16(vc * lax.rsqrt(var + EPS) * ln_g_ref[...] + ln_b_ref[...])
    gate_a = _silu(proj[:, _C_ZA:_C_CQ])
    t_chunk = _chunk_of(lax.broadcasted_iota(jnp.int32, (GMLP_BLOCK, GMLP_BLOCK), 0))
    s_chunk = _chunk_of(lax.broadcasted_iota(jnp.int32, (GMLP_BLOCK, GMLP_BLOCK), 1))
    for g in range(GMLP_GROUPS):
        w_g = _bf16(jnp.where(s_chunk <= t_chunk, w_s_ref[g], 0.0))
        cols = slice(g * GMLP_GROUP_DIM, (g + 1) * GMLP_GROUP_DIM)
        for r in range(TM // GMLP_BLOCK):
            rows = slice(r * GMLP_BLOCK, (r + 1) * GMLP_BLOCK)
            y = _dot(w_g, vn[rows, cols]) + b_s_ref[g]
            out_a_ref[0, rows, cols] = _bf16(u[rows, cols] * y * gate_a[rows, cols])

    gate_b_ref[0] = _silu(proj[:, _C_ZB:_C_END])

    c_q = proj[:, _C_CQ:_C_CKV]
    c_q = c_q * lax.rsqrt(jnp.mean(c_q * c_q, axis=-1, keepdims=True) + EPS) * qn_g_ref[...]
    c_kv = proj[:, _C_CKV:_C_KR]
    c_kv = _bf16(c_kv * lax.rsqrt(jnp.mean(c_kv * c_kv, axis=-1, keepdims=True) + EPS) * kvn_g_ref[...])
    cos = cos_ref[...]
    sin = sin_ref[...]
    k_rope = proj[:, _C_KR:_C_KRS] * cos[:, :LANES] + proj[:, _C_KRS:_C_SLAB] * sin[:, :LANES]
    slab = proj[:, _C_SLAB:_C_ZB]
    kidx_ref[0] = _bf16(slab)
    widx_ref[0] = slab
    kcat_ref[0, :, :DSA_KV_RANK] = c_kv
    kcat_ref[0, :, DSA_KV_RANK:] = _bf16(k_rope)
    lane = lax.broadcasted_iota(jnp.int32, (TM, LANES), 1)
    vcat_ref[0, :, :DSA_KV_RANK] = c_kv
    vcat_ref[0, :, DSA_KV_RANK:] = jnp.where(lane == 0, 1.0, 0.0).astype(jnp.bfloat16)

    q_all = _dot(_bf16(c_q), w_q_ref[...])
    q_rope = q_all[:, _Q_ROPE:_Q_ROPES] * cos + q_all[:, _Q_ROPES:_Q_IDX] * sin
    scale = np.float32(1.0 / math.sqrt(DSA_NOPE_DIM + DSA_ROPE_DIM))
    for hd in range(DSA_HEADS):
        cols = slice(_Q_NOPE + hd * LANES, _Q_NOPE + (hd + 1) * LANES)
        q_lat = _dot(_bf16(q_all[:, cols]), w_uk_ref[hd])
        grp, sub = divmod(hd, ROPE_LANE_GROUPS)
        own = (lane >= sub * DSA_ROPE_DIM) & (lane < (sub + 1) * DSA_ROPE_DIM)
        q_r = jnp.where(own, q_rope[:, grp * LANES:(grp + 1) * LANES] * scale, 0.0)
        for t in range(TM // TQ):
            src = slice(t * TQ, (t + 1) * TQ)
            dst = slice(hd * TQ, (hd + 1) * TQ)
            qcat_ref[0, t, dst, :DSA_KV_RANK] = _bf16(q_lat[src] * scale)
            qcat_ref[0, t, dst, DSA_KV_RANK:] = _bf16(q_r[src])
        icol = slice(_Q_IDX + hd * LANES, _Q_IDX + (hd + 1) * LANES)
        qidx_ref[0, hd] = _bf16(q_all[:, icol])


def _dsa_kernel(qidx_ref, widx_ref, qcat_ref, kidx_ref, kcat_ref, vcat_ref, o_ref,
                sc_ref, wrep_ref, bias_ref, m_ref, acc_ref, kmax_ref, *, top_k):
    i = pl.program_id(1)
    n_blocks = (i * TQ) // KB + 1
    chunks = [slice(c * LANES, (c + 1) * LANES) for c in range(KB // LANES)]
    idx_scale = np.float32(IDX_HEADS ** -0.5 * IDX_DIM ** -0.5)

    row = lax.broadcasted_iota(jnp.int32, (TQ, LANES), 0)
    row_t = lax.broadcasted_iota(jnp.int32, (ROW_TILE, LANES), 0)
    lane_t = lax.broadcasted_iota(jnp.int32, (ROW_TILE, LANES), 1)
    n_adm = (_chunk_of(i * TQ + row) + 1) * CHUNK

    def lane_all(op, a):
        return jnp.broadcast_to(op(a, axis=1, keepdims=True), a.shape)

    w = widx_ref[0] * idx_scale
    for hd in range(IDX_HEADS):
        wrep_ref[hd] = jnp.broadcast_to(w[:, W_IDX_LANE + hd:W_IDX_LANE + hd + 1], (TQ, LANES))

    def score_block(kb, carry):
        k = kidx_ref[0, pl.ds(pl.multiple_of(kb * KB, KB), KB), :]
        grp_max = [[], []]
        for r in range(TQ // ROW_TILE):
            rows = slice(r * ROW_TILE, (r + 1) * ROW_TILE)
            acc = [jnp.zeros((ROW_TILE, LANES), jnp.float32) for _ in chunks]
            for hd in range(IDX_HEADS):
                logit = _dot_nt(qidx_ref[0, hd, rows, :], k)
                w_hd = wrep_ref[hd, rows, :]
                for c, cols in enumerate(chunks):
                    acc[c] = acc[c] + w_hd * jnp.maximum(logit[:, cols], 0.0)
            part = [carry[0][rows], carry[1][rows]]
            q_chunk_r = _chunk_of(i * TQ + r * ROW_TILE + row_t)
            for c, cols in enumerate(chunks):
                adm = _chunk_of(kb * KB + c * LANES + lane_t) <= q_chunk_r
                sc_ref[kb, rows, cols] = jnp.where(adm, _sort_key(acc[c]), KEY_NONE)
                part[c % 2] = jnp.maximum(part[c % 2], jnp.where(adm, acc[c], -jnp.inf))
            grp_max[0].append(part[0])
            grp_max[1].append(part[1])
        return jnp.concatenate(grp_max[0], axis=0), jnp.concatenate(grp_max[1], axis=0)

    neg_inf = jnp.full((TQ, LANES), -jnp.inf, jnp.float32)
    grp_a, grp_b = lax.fori_loop(0, n_blocks, score_block, (neg_inf, neg_inf))

    kf = np.float32(top_k)

    def count_ge(mid):
        parts = []
        for r in range(TQ // ROW_TILE):
            rows = slice(r * ROW_TILE, (r + 1) * ROW_TILE)
            mid_r = mid[rows]

            def body(kb, cnt, rows=rows, mid_r=mid_r):
                for cols in chunks:
                    cnt = cnt + jnp.where(sc_ref[kb, rows, cols] >= mid_r, 1.0, 0.0)
                return cnt
            parts.append(lax.fori_loop(0, n_blocks, body, jnp.zeros((ROW_TILE, LANES), jnp.float32)))
        return lane_all(jnp.sum, jnp.concatenate(parts, axis=0))

    def pivot(lo, hi):
        mean = (lo & hi) + lax.shift_right_arithmetic(lo ^ hi, jnp.int32(1))
        return jnp.where((lo <= 0) & (hi > 1), 1, jnp.where((lo < 0) & (hi == 1), 0, mean))

    select_all = n_adm <= top_k
    lo0 = jnp.where(select_all, KEY_NONE + 1, _sort_key(lane_all(jnp.min, jnp.minimum(grp_a, grp_b))))
    hi0 = jnp.where(select_all, KEY_NONE + 1, _sort_key(lane_all(jnp.max, jnp.maximum(grp_a, grp_b))) + 1)

    def bisect_cond(state):
        it, lo, _, mid = state
        return jnp.logical_and(it < MAX_BISECT, jnp.max(jnp.where(mid != lo, 1.0, 0.0)) > 0.0)

    def bisect_body(state):
        it, lo, hi, mid = state
        cnt = count_ge(mid)
        ge = cnt >= kf
        lo = jnp.where(ge, mid, lo)
        hi = jnp.where(cnt == kf, mid, jnp.where(ge, hi, mid))
        return it + 1, lo, hi, pivot(lo, hi)

    _, thr, hi, _ = lax.while_loop(bisect_cond, bisect_body, (jnp.int32(0), lo0, hi0, pivot(lo0, hi0)))
    tie = hi != thr

    @pl.when(jnp.max(jnp.where(tie, 1.0, 0.0)) > 0.0)
    def _():
        need = kf - count_ge(thr + 1)
        upto = (lax.broadcasted_iota(jnp.int32, (KB, KB), 0)
                <= lax.broadcasted_iota(jnp.int32, (KB, KB), 1)).astype(jnp.bfloat16)

        def drop_body(kb, seen):
            keys = [sc_ref[kb, :, cols] for cols in chunks]
            band = [tie & (key == thr) for key in keys]
            band_f = [jnp.where(b, 1.0, 0.0) for b in band]
            rank = _dot(jnp.concatenate([_bf16(b) for b in band_f], axis=1), upto)
            for c, cols in enumerate(chunks):
                late = band[c] & (rank[:, cols] + seen > need)
                sc_ref[kb, :, cols] = jnp.where(late, KEY_NONE, keys[c])
            return seen + lane_all(jnp.sum, functools.reduce(jnp.add, band_f))
        lax.fori_loop(0, n_blocks, drop_body, jnp.zeros((TQ, LANES), jnp.float32))

    @pl.when(i == 0)
    def _():
        def norm_block(kb, run):
            kc = kcat_ref[0, pl.ds(pl.multiple_of(kb * KB, KB), KB), :].astype(jnp.float32)
            return jnp.maximum(run, jnp.sum(kc * kc, axis=1, keepdims=True))
        run = lax.fori_loop(0, kcat_ref.shape[1] // KB, norm_block, jnp.zeros((KB, 1), jnp.float32))
        kmax_ref[...] = jnp.broadcast_to(jnp.max(run, axis=0, keepdims=True), kmax_ref.shape)

    k_norm2 = kmax_ref[0:1, :]
    q_all = qcat_ref[0, 0]
    q_norm2 = _dot(q_all * q_all, jnp.ones((Q_CAT, LANES), jnp.bfloat16))
    qk2 = jnp.maximum(q_norm2 * k_norm2, 1e-30)
    m_ref[...] = qk2 * lax.rsqrt(qk2) * BOUND_SLACK

    def set_bias(kb):
        for cols in chunks:
            bias_ref[:, cols] = jnp.where(sc_ref[kb, :, cols] >= thr, 0.0, NEG_BIG)

    def bias_rows(cols):
        return jnp.concatenate([bias_ref[:, cols]] * HEAD_GROUP, axis=0)

    group_rows = [slice(g * HEAD_GROUP * TQ, (g + 1) * HEAD_GROUP * TQ) for g in range(DSA_HEADS // HEAD_GROUP)]

    def attend():
        acc_ref[...] = jnp.zeros(acc_ref.shape, jnp.float32)

        def attn_block(kb, carry):
            rows = pl.ds(pl.multiple_of(kb * KB, KB), KB)
            kc = kcat_ref[0, rows, :]
            vc = vcat_ref[0, rows, :]
            set_bias(kb)
            for grp in group_rows:
                s = _dot_nt(qcat_ref[0, 0, grp, :], kc)
                m = m_ref[grp, :]
                p = [_bf16(jnp.exp(s[:, cols] + bias_rows(cols) - m)) for cols in chunks]
                acc_ref[grp, :] += _dot(jnp.concatenate(p, axis=1), vc)
            return carry
        lax.fori_loop(0, n_blocks, attn_block, 0)
        return jnp.min(acc_ref[:, DSA_KV_RANK:DSA_KV_RANK + 1])

    @pl.when(attend() < ROW_SUM_MIN)
    def _():
        m_ref[...] = jnp.full(m_ref.shape, NEG_BIG, jnp.float32)

        def max_block(kb, carry):
            kc = kcat_ref[0, pl.ds(pl.multiple_of(kb * KB, KB), KB), :]
            set_bias(kb)
            for grp in group_rows:
                s = _dot_nt(qcat_ref[0, 0, grp, :], kc)
                sm = [s[:, cols] + bias_rows(cols) for cols in chunks]
                m_ref[grp, :] = jnp.maximum(m_ref[grp, :], functools.reduce(jnp.maximum, sm))
            return carry
        lax.fori_loop(0, n_blocks, max_block, 0)
        m_ref[...] = lane_all(jnp.max, m_ref[...])
        attend()

    for hd in range(DSA_HEADS):
        a = acc_ref[hd * TQ:(hd + 1) * TQ, :]
        denom = jnp.broadcast_to(a[:, DSA_KV_RANK:DSA_KV_RANK + 1], (TQ, DSA_KV_RANK))
        o_ref[0, :, hd * DSA_KV_RANK:(hd + 1) * DSA_KV_RANK] = _bf16(a[:, :DSA_KV_RANK] / denom)


def _out_kernel(x_ref, out_a_ref, o_lat_ref, gate_b_ref, w_uv_ref, w_out_ref, post_g_ref, y_ref):
    o = _dot(o_lat_ref[0], w_uv_ref[...])
    out_b = _bf16(o * gate_b_ref[0])
    y = _dot(out_a_ref[0], w_out_ref[:GMLP_WIDTH, :]) + _dot(out_b, w_out_ref[GMLP_WIDTH:, :])
    y = y * lax.rsqrt(jnp.mean(y * y, axis=-1, keepdims=True) + EPS) * post_g_ref[...]
    y_ref[0] = x_ref[0] + y


def _full(shape):
    return pl.BlockSpec(shape, lambda b, t: (0,) * len(shape))


def _layer(x, w_in, pre_g, post_g, ln_g, ln_b, w_s, b_s, qn_g, kvn_g, w_uq, w_uk, w_uv, w_q_idx, w_out):
    B, S, D = x.shape
    assert D == D_MODEL and S % KB == 0 and S % TM == 0 and TM % TQ == 0 and KB % TQ == 0
    top_k = min(TOPK_MAX, S // 4)
    assert top_k <= 2 * LANES
    f32, bf16 = jnp.float32, jnp.bfloat16

    kr = w_in[:, 1920:1952]
    kr_sw = jnp.concatenate([kr[:, ROPE_HALF:], kr[:, :ROPE_HALF]], axis=1)
    w_all = jnp.concatenate([
        w_in[:, :1920], jnp.tile(kr, (1, ROPE_LANE_GROUPS)), jnp.tile(kr_sw, (1, ROPE_LANE_GROUPS)),
        w_in[:, 1952:2024], jnp.zeros((D, LANES - IDX_DIM - IDX_HEADS), f32), w_in[:, 2024:]], axis=1).astype(bf16)
    wq3 = w_uq.reshape(DSA_Q_RANK, DSA_HEADS, DSA_NOPE_DIM + DSA_ROPE_DIM)
    nope = jnp.pad(wq3[:, :, :DSA_NOPE_DIM], ((0, 0), (0, 0), (0, LANES - DSA_NOPE_DIM)))
    rope = wq3[:, :, DSA_NOPE_DIM:]
    rope_sw = jnp.concatenate([rope[:, :, ROPE_HALF:], rope[:, :, :ROPE_HALF]], axis=2)
    wqi = jnp.pad(w_q_idx.reshape(DSA_Q_RANK, IDX_HEADS, IDX_DIM), ((0, 0), (0, 0), (0, LANES - IDX_DIM)))
    w_q = jnp.concatenate([nope.reshape(DSA_Q_RANK, -1), rope.reshape(DSA_Q_RANK, -1),
                           rope_sw.reshape(DSA_Q_RANK, -1), wqi.reshape(DSA_Q_RANK, -1)], axis=1).astype(bf16)
    w_uk_t = jnp.pad(jnp.transpose(w_uk, (1, 2, 0)), ((0, 0), (0, LANES - DSA_NOPE_DIM), (0, 0))).astype(bf16)
    eye = jnp.eye(DSA_HEADS, dtype=f32)
    w_uv_bd = (jnp.transpose(w_uv, (1, 0, 2))[:, :, None, :] * eye[:, None, :, None]).reshape(
        DSA_HEADS * DSA_KV_RANK, DSA_WIDTH).astype(bf16)
    b_s_b = jnp.broadcast_to(b_s[:, :, None], (GMLP_GROUPS, GMLP_BLOCK, GMLP_GROUP_DIM))

    pos = jnp.arange(S, dtype=f32)
    inv_freq = ROPE_THETA ** (-jnp.arange(0, DSA_ROPE_DIM, 2, dtype=f32) / DSA_ROPE_DIM)
    ang = pos[:, None] * inv_freq[None, :]
    cos_t = jnp.tile(jnp.concatenate([jnp.cos(ang), jnp.cos(ang)], axis=1), (1, DSA_HEADS))
    sin_t = jnp.tile(jnp.concatenate([-jnp.sin(ang), jnp.sin(ang)], axis=1), (1, DSA_HEADS))

    row2 = lambda a: a.reshape(1, -1)
    tok = lambda width: pl.BlockSpec((1, TM, width), lambda b, t: (b, t, 0))
    hm = lambda width: pl.BlockSpec((1, DSA_HEADS, TM, width), lambda b, t: (b, 0, t, 0))

    out_a, gate_b, qidx, widx, qcat, kidx, kcat, vcat = pl.pallas_call(
        _proj_kernel,
        grid=(B, S // TM),
        in_specs=[tok(D), _full((1, D)), _full((D, _C_END)), _full((1, GMLP_WIDTH)), _full((1, GMLP_WIDTH)),
                  _full((GMLP_GROUPS, GMLP_BLOCK, GMLP_BLOCK)), _full((GMLP_GROUPS, GMLP_BLOCK, GMLP_GROUP_DIM)),
                  _full((1, DSA_Q_RANK)), _full((1, DSA_KV_RANK)), _full((DSA_Q_RANK, _Q_END)),
                  _full((DSA_HEADS, LANES, DSA_KV_RANK)),
                  pl.BlockSpec((TM, 2 * LANES), lambda b, t: (t, 0)),
                  pl.BlockSpec((TM, 2 * LANES), lambda b, t: (t, 0))],
        out_specs=[tok(GMLP_WIDTH), tok(DSA_WIDTH), hm(LANES), tok(LANES),
                   pl.BlockSpec((1, TM // TQ, DSA_HEADS * TQ, Q_CAT), lambda b, t: (b, t, 0, 0)),
                   tok(LANES), tok(Q_CAT), tok(Q_CAT)],
        out_shape=[jax.ShapeDtypeStruct((B, S, GMLP_WIDTH), bf16),
                   jax.ShapeDtypeStruct((B, S, DSA_WIDTH), f32),
                   jax.ShapeDtypeStruct((B, DSA_HEADS, S, LANES), bf16),
                   jax.ShapeDtypeStruct((B, S, LANES), f32),
                   jax.ShapeDtypeStruct((B, S // TQ, DSA_HEADS * TQ, Q_CAT), bf16),
                   jax.ShapeDtypeStruct((B, S, LANES), bf16),
                   jax.ShapeDtypeStruct((B, S, Q_CAT), bf16),
                   jax.ShapeDtypeStruct((B, S, Q_CAT), bf16)],
        compiler_params=pltpu.CompilerParams(
            dimension_semantics=("arbitrary", "arbitrary"), vmem_limit_bytes=48 * 1024 * 1024),
        name="proj_gmlp_dsa_prep",
    )(x, row2(pre_g), w_all, row2(ln_g), row2(ln_b), w_s, b_s_b, row2(qn_g), row2(kvn_g), w_q, w_uk_t,
      cos_t, sin_t)

    qt = lambda width: pl.BlockSpec((1, DSA_HEADS, TQ, width), lambda b, t: (b, 0, t, 0))
    seq = lambda width: pl.BlockSpec((1, S, width), lambda b, t: (b, 0, 0))
    o_lat = pl.pallas_call(
        functools.partial(_dsa_kernel, top_k=top_k),
        grid=(B, S // TQ),
        in_specs=[qt(LANES), pl.BlockSpec((1, TQ, LANES), lambda b, t: (b, t, 0)),
                  pl.BlockSpec((1, 1, DSA_HEADS * TQ, Q_CAT), lambda b, t: (b, t, 0, 0)),
                  seq(LANES), seq(Q_CAT), seq(Q_CAT)],
        out_specs=pl.BlockSpec((1, TQ, DSA_HEADS * DSA_KV_RANK), lambda b, t: (b, t, 0)),
        out_shape=jax.ShapeDtypeStruct((B, S, DSA_HEADS * DSA_KV_RANK), bf16),
        scratch_shapes=[pltpu.VMEM((S // KB, TQ, KB), jnp.int32),
                        pltpu.VMEM((IDX_HEADS, TQ, LANES), f32),
                        pltpu.VMEM((TQ, KB), f32),
                        pltpu.VMEM((DSA_HEADS * TQ, LANES), f32),
                        pltpu.VMEM((DSA_HEADS * TQ, Q_CAT), f32),
                        pltpu.VMEM((8, LANES), f32)],
        compiler_params=pltpu.CompilerParams(
            dimension_semantics=("arbitrary", "arbitrary"), vmem_limit_bytes=56 * 1024 * 1024),
        name="dsa_index_select_attend",
    )(qidx, widx, qcat, kidx, kcat, vcat)

    return pl.pallas_call(
        _out_kernel,
        grid=(B, S // TM),
        in_specs=[tok(D), tok(GMLP_WIDTH), tok(DSA_HEADS * DSA_KV_RANK), tok(DSA_WIDTH),
                  _full((DSA_HEADS * DSA_KV_RANK, DSA_WIDTH)), _full((GMLP_WIDTH + DSA_WIDTH, D)), _full((1, D))],
        out_specs=tok(D),
        out_shape=jax.ShapeDtypeStruct((B, S, D), f32),
        compiler_params=pltpu.CompilerParams(
            dimension_semantics=("arbitrary", "arbitrary"), vmem_limit_bytes=32 * 1024 * 1024),
        name="out_proj_norm_residual",
    )(x, out_a, o_lat, gate_b, w_uv_bd, w_out.astype(bf16), row2(post_g))


def kernel(x, w_in, pre_norm_g, post_norm_g, gmlp_ln_g, gmlp_ln_b, gmlp_w_s, gmlp_b_s, dsa_q_norm_g, dsa_kv_norm_g, dsa_w_uq, dsa_w_uk, dsa_w_uv, dsa_w_q_idx, w_out):
    for l in range(w_in.shape[0]):
        x = _layer(x, w_in[l], pre_norm_g[l], post_norm_g[l], gmlp_ln_g[l], gmlp_ln_b[l], gmlp_w_s[l],
                   gmlp_b_s[l], dsa_q_norm_g[l], dsa_kv_norm_g[l], dsa_w_uq[l], dsa_w_uk[l], dsa_w_uv[l],
                   dsa_w_q_idx[l], w_out[l])
    return x
```

```python
import functools
import math

import jax
import jax.numpy as jnp
import numpy as np
from jax import lax
from jax.experimental import pallas as pl
from jax.experimental.pallas import tpu as pltpu

D_MODEL = 1024
CHUNK = 64
EPS = 1e-6
GMLP_GROUPS = 4
GMLP_GROUP_DIM = 128
GMLP_WIDTH = GMLP_GROUPS * GMLP_GROUP_DIM
GMLP_BLOCK = 128
DSA_HEADS = 8
DSA_V_DIM = 64
DSA_WIDTH = DSA_HEADS * DSA_V_DIM
DSA_NOPE_DIM = 64
DSA_ROPE_DIM = 32
DSA_Q_RANK = 256
DSA_KV_RANK = 128
IDX_HEADS = 8
IDX_DIM = 64
TOPK_MAX = 256
ROPE_THETA = 10000.0

LANES = 128
ROPE_HALF = DSA_ROPE_DIM // 2
ROPE_LANE_GROUPS = LANES // DSA_ROPE_DIM
Q_CAT = DSA_KV_RANK + LANES
W_IDX_LANE = IDX_DIM

TM = 512
TQ = 512
ROW_TILE = 128
KB = 512
HEAD_GROUP = 1
NEG_BIG = -1e30
ROW_SUM_MIN = 1e-26
BOUND_SLACK = 1.02
KEY_NONE = -2 ** 31
MAX_BISECT = 40

_C_UV, _C_ZA, _C_CQ, _C_CKV = 0, 1024, 1536, 1792
_C_KR, _C_KRS, _C_SLAB, _C_ZB, _C_END = 1920, 2048, 2176, 2304, 2816
_Q_NOPE, _Q_ROPE, _Q_ROPES, _Q_IDX, _Q_END = 0, 1024, 1280, 1536, 2560

_NT = (((1,), (1,)), ((), ()))


def _bf16(a):
    return a.astype(jnp.bfloat16)


def _dot(a, b):
    return jnp.dot(a, b, preferred_element_type=jnp.float32)


def _dot_nt(a, b):
    return lax.dot_general(a, b, _NT, preferred_element_type=jnp.float32)


def _chunk_of(pos):
    return lax.shift_right_logical(pos, jnp.int32(CHUNK.bit_length() - 1))


def _sort_key(s):
    bits = pltpu.bitcast(s, jnp.int32)
    return jnp.where(bits < 0, bits ^ jnp.int32(0x7FFFFFFF), bits)


def _silu(z):
    return z / (1.0 + jnp.exp(-z))


def _gelu_exact(a):
    return 0.5 * a * (1.0 + lax.erf(a * np.float32(math.sqrt(0.5))))


def _proj_kernel(x_ref, pre_g_ref, w_all_ref, ln_g_ref, ln_b_ref, w_s_ref, b_s_ref,
                 qn_g_ref, kvn_g_ref, w_q_ref, w_uk_ref, cos_ref, sin_ref,
                 out_a_ref, gate_b_ref, qidx_ref, widx_ref, qcat_ref, kidx_ref, kcat_ref, vcat_ref):
    x = x_ref[0]
    h = x * lax.rsqrt(jnp.mean(x * x, axis=-1, keepdims=True) + EPS) * pre_g_ref[...]
    proj = _dot(_bf16(h), w_all_ref[...])

    uv = _gelu_exact(proj[:, _C_UV:_C_ZA])
    u, v = uv[:, :GMLP_WIDTH], uv[:, GMLP_WIDTH:]
    mu = jnp.mean(v, axis=-1, keepdims=True)
    vc = v - mu
    var = jnp.mean(vc * vc, axis=-1, keepdims=True)
    vn = _bf16(vc * lax.rsqrt(var + EPS) * ln_g_ref[...] + ln_b_ref[...])
    gate_a = _silu(proj[:, _C_ZA:_C_CQ])
    t_chunk = _chunk_of(lax.broadcasted_iota(jnp.int32, (GMLP_BLOCK, GMLP_BLOCK), 0))
    s_chunk = _chunk_of(lax.broadcasted_iota(jnp.int32, (GMLP_BLOCK, GMLP_BLOCK), 1))
    for g in range(GMLP_GROUPS):
        w_g = _bf16(jnp.where(s_chunk <= t_chunk, w_s_ref[g], 0.0))
        cols = slice(g * GMLP_GROUP_DIM, (g + 1) * GMLP_GROUP_DIM)
        for r in range(TM // GMLP_BLOCK):
            rows = slice(r * GMLP_BLOCK, (r + 1) * GMLP_BLOCK)
            y = _dot(w_g, vn[rows, cols]) + b_s_ref[g]
            out_a_ref[0, rows, cols] = _bf16(u[rows, cols] * y * gate_a[rows, cols])

    gate_b_ref[0] = _silu(proj[:, _C_ZB:_C_END])

    c_q = proj[:, _C_CQ:_C_CKV]
    c_q = c_q * lax.rsqrt(jnp.mean(c_q * c_q, axis=-1, keepdims=True) + EPS) * qn_g_ref[...]
    c_kv = proj[:, _C_CKV:_C_KR]
    c_kv = _bf16(c_kv * lax.rsqrt(jnp.mean(c_kv * c_kv, axis=-1, keepdims=True) + EPS) * kvn_g_ref[...])
    cos = cos_ref[...]
    sin = sin_ref[...]
    k_rope = proj[:, _C_KR:_C_KRS] * cos[:, :LANES] + proj[:, _C_KRS:_C_SLAB] * sin[:, :LANES]
    slab = proj[:, _C_SLAB:_C_ZB]
    kidx_ref[0] = _bf16(slab)
    widx_ref[0] = slab
    kcat_ref[0, :, :DSA_KV_RANK] = c_kv
    kcat_ref[0, :, DSA_KV_RANK:] = _bf16(k_rope)
    lane = lax.broadcasted_iota(jnp.int32, (TM, LANES), 1)
    vcat_ref[0, :, :DSA_KV_RANK] = c_kv
    vcat_ref[0, :, DSA_KV_RANK:] = jnp.where(lane == 0, 1.0, 0.0).astype(jnp.bfloat16)

    q_all = _dot(_bf16(c_q), w_q_ref[...])
    q_rope = q_all[:, _Q_ROPE:_Q_ROPES] * cos + q_all[:, _Q_ROPES:_Q_IDX] * sin
    scale = np.float32(1.0 / math.sqrt(DSA_NOPE_DIM + DSA_ROPE_DIM))
    for hd in range(DSA_HEADS):
        cols = slice(_Q_NOPE + hd * LANES, _Q_NOPE + (hd + 1) * LANES)
        q_lat = _dot(_bf16(q_all[:, cols]), w_uk_ref[hd])
        grp, sub = divmod(hd, ROPE_LANE_GROUPS)
        own = (lane >= sub * DSA_ROPE_DIM) & (lane < (sub + 1) * DSA_ROPE_DIM)
        q_r = jnp.where(own, q_rope[:, grp * LANES:(grp + 1) * LANES] * scale, 0.0)
        for t in range(TM // TQ):
            src = slice(t * TQ, (t + 1) * TQ)
            dst = slice(hd * TQ, (hd + 1) * TQ)
            qcat_ref[0, t, dst, :DSA_KV_RANK] = _bf16(q_lat[src] * scale)
            qcat_ref[0, t, dst, DSA_KV_RANK:] = _bf16(q_r[src])
        icol = slice(_Q_IDX + hd * LANES, _Q_IDX + (hd + 1) * LANES)
        qidx_ref[0, hd] = _bf16(q_all[:, icol])


def _dsa_kernel(qidx_ref, widx_ref, qcat_ref, kidx_ref, kcat_ref, vcat_ref, o_ref,
                sc_ref, wrep_ref, bias_ref, m_ref, acc_ref, kmax_ref, *, top_k):
    i = pl.program_id(1)
    n_blocks = (i * TQ) // KB + 1
    chunks = [slice(c * LANES, (c + 1) * LANES) for c in range(KB // LANES)]
    idx_scale = np.float32(IDX_HEADS ** -0.5 * IDX_DIM ** -0.5)

    row = lax.broadcasted_iota(jnp.int32, (TQ, LANES), 0)
    row_t = lax.broadcasted_iota(jnp.int32, (ROW_TILE, LANES), 0)
    lane_t = lax.broadcasted_iota(jnp.int32, (ROW_TILE, LANES), 1)
    n_adm = (_chunk_of(i * TQ + row) + 1) * CHUNK

    def lane_all(op, a):
        return jnp.broadcast_to(op(a, axis=1, keepdims=True), a.shape)

    w = widx_ref[0] * idx_scale
    for hd in range(IDX_HEADS):
        wrep_ref[hd] = jnp.broadcast_to(w[:, W_IDX_LANE + hd:W_IDX_LANE + hd + 1], (TQ, LANES))

    def score_block(kb, carry):
        k = kidx_ref[0, pl.ds(pl.multiple_of(kb * KB, KB), KB), :]
        grp_max = [[], []]
        for r in range(TQ // ROW_TILE):
            rows = slice(r * ROW_TILE, (r + 1) * ROW_TILE)
            acc = [jnp.zeros((ROW_TILE, LANES), jnp.float32) for _ in chunks]
            for hd in range(IDX_HEADS):
                logit = _dot_nt(qidx_ref[0, hd, rows, :], k)
                w_hd = wrep_ref[hd, rows, :]
                for c, cols in enumerate(chunks):
                    acc[c] = acc[c] + w_hd * jnp.maximum(logit[:, cols], 0.0)
            part = [carry[0][rows], carry[1][rows]]
            q_chunk_r = _chunk_of(i * TQ + r * ROW_TILE + row_t)
            for c, cols in enumerate(chunks):
                adm = _chunk_of(kb * KB + c * LANES + lane_t) <= q_chunk_r
                sc_ref[kb, rows, cols] = jnp.where(adm, _sort_key(acc[c]), KEY_NONE)
                part[c % 2] = jnp.maximum(part[c % 2], jnp.where(adm, acc[c], -jnp.inf))
            grp_max[0].append(part[0])
            grp_max[1].append(part[1])
        return jnp.concatenate(grp_max[0], axis=0), jnp.concatenate(grp_max[1], axis=0)

    neg_inf = jnp.full((TQ, LANES), -jnp.inf, jnp.float32)
    grp_a, grp_b = lax.fori_loop(0, n_blocks, score_block, (neg_inf, neg_inf))

    kf = np.float32(top_k)

    def count_ge(mid):
        parts = []
        for r in range(TQ // ROW_TILE):
            rows = slice(r * ROW_TILE, (r + 1) * ROW_TILE)
            mid_r = mid[rows]

            def body(kb, cnt, rows=rows, mid_r=mid_r):
                for cols in chunks:
                    cnt = cnt + jnp.where(sc_ref[kb, rows, cols] >= mid_r, 1.0, 0.0)
                return cnt
            parts.append(lax.fori_loop(0, n_blocks, body, jnp.zeros((ROW_TILE, LANES), jnp.float32)))
        return lane_all(jnp.sum, jnp.concatenate(parts, axis=0))

    def pivot(lo, hi):
        mean = (lo & hi) + lax.shift_right_arithmetic(lo ^ hi, jnp.int32(1))
        return jnp.where((lo <= 0) & (hi > 1), 1, jnp.where((lo < 0) & (hi == 1), 0, mean))

    select_all = n_adm <= top_k
    lo0 = jnp.where(select_all, KEY_NONE + 1, _sort_key(lane_all(jnp.min, jnp.minimum(grp_a, grp_b))))
    hi0 = jnp.where(select_all, KEY_NONE + 1, _sort_key(lane_all(jnp.max, jnp.maximum(grp_a, grp_b))) + 1)

    def bisect_cond(state):
        it, lo, _, mid = state
        return jnp.logical_and(it < MAX_BISECT, jnp.max(jnp.where(mid != lo, 1.0, 0.0)) > 0.0)

    def bisect_body(state):
        it, lo, hi, mid = state
        cnt = count_ge(mid)
        ge = cnt >= kf
        lo = jnp.where(ge, mid, lo)
        hi = jnp.where(cnt == kf, mid, jnp.where(ge, hi, mid))
        return it + 1, lo, hi, pivot(lo, hi)

    _, thr, hi, _ = lax.while_loop(bisect_cond, bisect_body, (jnp.int32(0), lo0, hi0, pivot(lo0, hi0)))
    tie = hi != thr

    @pl.when(jnp.max(jnp.where(tie, 1.0, 0.0)) > 0.0)
    def _():
        need = kf - count_ge(thr + 1)
        upto = (lax.broadcasted_iota(jnp.int32, (KB, KB), 0)
                <= lax.broadcasted_iota(jnp.int32, (KB, KB), 1)).astype(jnp.bfloat16)

        def drop_body(kb, seen):
            keys = [sc_ref[kb, :, cols] for cols in chunks]
            band = [tie & (key == thr) for key in keys]
            band_f = [jnp.where(b, 1.0, 0.0) for b in band]
            rank = _dot(jnp.concatenate([_bf16(b) for b in band_f], axis=1), upto)
            for c, cols in enumerate(chunks):
                late = band[c] & (rank[:, cols] + seen > need)
                sc_ref[kb, :, cols] = jnp.where(late, KEY_NONE, keys[c])
            return seen + lane_all(jnp.sum, functools.reduce(jnp.add, band_f))
        lax.fori_loop(0, n_blocks, drop_body, jnp.zeros((TQ, LANES), jnp.float32))

    @pl.when(i == 0)
    def _():
        def norm_block(kb, run):
            kc = kcat_ref[0, pl.ds(pl.multiple_of(kb * KB, KB), KB), :].astype(jnp.float32)
            return jnp.maximum(run, jnp.sum(kc * kc, axis=1, keepdims=True))
        run = lax.fori_loop(0, kcat_ref.shape[1] // KB, norm_block, jnp.zeros((KB, 1), jnp.float32))
        kmax_ref[...] = jnp.broadcast_to(jnp.max(run, axis=0, keepdims=True), kmax_ref.shape)

    k_norm2 = kmax_ref[0:1, :]
    q_all = qcat_ref[0, 0]
    q_norm2 = _dot(q_all * q_all, jnp.ones((Q_CAT, LANES), jnp.bfloat16))
    qk2 = jnp.maximum(q_norm2 * k_norm2, 1e-30)
    m_ref[...] = qk2 * lax.rsqrt(qk2) * BOUND_SLACK

    def set_bias(kb):
        for cols in chunks:
            bias_ref[:, cols] = jnp.where(sc_ref[kb, :, cols] >= thr, 0.0, NEG_BIG)

    def bias_rows(cols):
        return jnp.concatenate([bias_ref[:, cols]] * HEAD_GROUP, axis=0)

    group_rows = [slice(g * HEAD_GROUP * TQ, (g + 1) * HEAD_GROUP * TQ) for g in range(DSA_HEADS // HEAD_GROUP)]

    def attend():
        acc_ref[...] = jnp.zeros(acc_ref.shape, jnp.float32)

        def attn_block(kb, carry):
            rows = pl.ds(pl.multiple_of(kb * KB, KB), KB)
            kc = kcat_ref[0, rows, :]
            vc = vcat_ref[0, rows, :]
            set_bias(kb)
            for grp in group_rows:
                s = _dot_nt(qcat_ref[0, 0, grp, :], kc)
                m = m_ref[grp, :]
                p = [_bf16(jnp.exp(s[:, cols] + bias_rows(cols) - m)) for cols in chunks]
                acc_ref[grp, :] += _dot(jnp.concatenate(p, axis=1), vc)
            return carry
        lax.fori_loop(0, n_blocks, attn_block, 0)
        return jnp.min(acc_ref[:, DSA_KV_RANK:DSA_KV_RANK + 1])

    @pl.when(attend() < ROW_SUM_MIN)
    def _():
        m_ref[...] = jnp.full(m_ref.shape, NEG_BIG, jnp.float32)

        def max_block(kb, carry):
            kc = kcat_ref[0, pl.ds(pl.multiple_of(kb * KB, KB), KB), :]
            set_bias(kb)
            for grp in group_rows:
                s = _dot_nt(qcat_ref[0, 0, grp, :], kc)
                sm = [s[:, cols] + bias_rows(cols) for cols in chunks]
                m_ref[grp, :] = jnp.maximum(m_ref[grp, :], functools.reduce(jnp.maximum, sm))
            return carry
        lax.fori_loop(0, n_blocks, max_block, 0)
        m_ref[...] = lane_all(jnp.max, m_ref[...])
        attend()

    for hd in range(DSA_HEADS):
        a = acc_ref[hd * TQ:(hd + 1) * TQ, :]
        denom = jnp.broadcast_to(a[:, DSA_KV_RANK:DSA_KV_RANK + 1], (TQ, DSA_KV_RANK))
        o_ref[0, :, hd * DSA_KV_RANK:(hd + 1) * DSA_KV_RANK] = _bf16(a[:, :DSA_KV_RANK] / denom)


def _out_kernel(x_ref, out_a_ref, o_lat_ref, gate_b_ref, w_uv_ref, w_out_ref, post_g_ref, y_ref):
    o = _dot(o_lat_ref[0], w_uv_ref[...])
    out_b = _bf16(o * gate_b_ref[0])
    y = _dot(out_a_ref[0], w_out_ref[:GMLP_WIDTH, :]) + _dot(out_b, w_out_ref[GMLP_WIDTH:, :])
    y = y * lax.rsqrt(jnp.mean(y * y, axis=-1, keepdims=True) + EPS) * post_g_ref[...]
    y_ref[0] = x_ref[0] + y


def _full(shape):
    return pl.BlockSpec(shape, lambda b, t: (0,) * len(shape))


def _layer(x, w_in, pre_g, post_g, ln_g, ln_b, w_s, b_s, qn_g, kvn_g, w_uq, w_uk, w_uv, w_q_idx, w_out):
    B, S, D = x.shape
    assert D == D_MODEL and S % KB == 0 and S % TM == 0 and TM % TQ == 0 and KB % TQ == 0
    top_k = min(TOPK_MAX, S // 4)
    assert top_k <= 2 * LANES
    f32, bf16 = jnp.float32, jnp.bfloat16

    kr = w_in[:, 1920:1952]
    kr_sw = jnp.concatenate([kr[:, ROPE_HALF:], kr[:, :ROPE_HALF]], axis=1)
    w_all = jnp.concatenate([
        w_in[:, :1920], jnp.tile(kr, (1, ROPE_LANE_GROUPS)), jnp.tile(kr_sw, (1, ROPE_LANE_GROUPS)),
        w_in[:, 1952:2024], jnp.zeros((D, LANES - IDX_DIM - IDX_HEADS), f32), w_in[:, 2024:]], axis=1).astype(bf16)
    wq3 = w_uq.reshape(DSA_Q_RANK, DSA_HEADS, DSA_NOPE_DIM + DSA_ROPE_DIM)
    nope = jnp.pad(wq3[:, :, :DSA_NOPE_DIM], ((0, 0), (0, 0), (0, LANES - DSA_NOPE_DIM)))
    rope = wq3[:, :, DSA_NOPE_DIM:]
    rope_sw = jnp.concatenate([rope[:, :, ROPE_HALF:], rope[:, :, :ROPE_HALF]], axis=2)
    wqi = jnp.pad(w_q_idx.reshape(DSA_Q_RANK, IDX_HEADS, IDX_DIM), ((0, 0), (0, 0), (0, LANES - IDX_DIM)))
    w_q = jnp.concatenate([nope.reshape(DSA_Q_RANK, -1), rope.reshape(DSA_Q_RANK, -1),
                           rope_sw.reshape(DSA_Q_RANK, -1), wqi.reshape(DSA_Q_RANK, -1)], axis=1).astype(bf16)
    w_uk_t = jnp.pad(jnp.transpose(w_uk, (1, 2, 0)), ((0, 0), (0, LANES - DSA_NOPE_DIM), (0, 0))).astype(bf16)
    eye = jnp.eye(DSA_HEADS, dtype=f32)
    w_uv_bd = (jnp.transpose(w_uv, (1, 0, 2))[:, :, None, :] * eye[:, None, :, None]).reshape(
        DSA_HEADS * DSA_KV_RANK, DSA_WIDTH).astype(bf16)
    b_s_b = jnp.broadcast_to(b_s[:, :, None], (GMLP_GROUPS, GMLP_BLOCK, GMLP_GROUP_DIM))

    pos = jnp.arange(S, dtype=f32)
    inv_freq = ROPE_THETA ** (-jnp.arange(0, DSA_ROPE_DIM, 2, dtype=f32) / DSA_ROPE_DIM)
    ang = pos[:, None] * inv_freq[None, :]
    cos_t = jnp.tile(jnp.concatenate([jnp.cos(ang), jnp.cos(ang)], axis=1), (1, DSA_HEADS))
    sin_t = jnp.tile(jnp.concatenate([-jnp.sin(ang), jnp.sin(ang)], axis=1), (1, DSA_HEADS))

    row2 = lambda a: a.reshape(1, -1)
    tok = lambda width: pl.BlockSpec((1, TM, width), lambda b, t: (b, t, 0))
    hm = lambda width: pl.BlockSpec((1, DSA_HEADS, TM, width), lambda b, t: (b, 0, t, 0))

    out_a, gate_b, qidx, widx, qcat, kidx, kcat, vcat = pl.pallas_call(
        _proj_kernel,
        grid=(B, S // TM),
        in_specs=[tok(D), _full((1, D)), _full((D, _C_END)), _full((1, GMLP_WIDTH)), _full((1, GMLP_WIDTH)),
                  _full((GMLP_GROUPS, GMLP_BLOCK, GMLP_BLOCK)), _full((GMLP_GROUPS, GMLP_BLOCK, GMLP_GROUP_DIM)),
                  _full((1, DSA_Q_RANK)), _full((1, DSA_KV_RANK)), _full((DSA_Q_RANK, _Q_END)),
                  _full((DSA_HEADS, LANES, DSA_KV_RANK)),
                  pl.BlockSpec((TM, 2 * LANES), lambda b, t: (t, 0)),
                  pl.BlockSpec((TM, 2 * LANES), lambda b, t: (t, 0))],
        out_specs=[tok(GMLP_WIDTH), tok(DSA_WIDTH), hm(LANES), tok(LANES),
                   pl.BlockSpec((1, TM // TQ, DSA_HEADS * TQ, Q_CAT), lambda b, t: (b, t, 0, 0)),
                   tok(LANES), tok(Q_CAT), tok(Q_CAT)],
        out_shape=[jax.ShapeDtypeStruct((B, S, GMLP_WIDTH), bf16),
                   jax.ShapeDtypeStruct((B, S, DSA_WIDTH), f32),
                   jax.ShapeDtypeStruct((B, DSA_HEADS, S, LANES), bf16),
                   jax.ShapeDtypeStruct((B, S, LANES), f32),
                   jax.ShapeDtypeStruct((B, S // TQ, DSA_HEADS * TQ, Q_CAT), bf16),
                   jax.ShapeDtypeStruct((B, S, LANES), bf16),
                   jax.ShapeDtypeStruct((B, S, Q_CAT), bf16),
                   jax.ShapeDtypeStruct((B, S, Q_CAT), bf16)],
        compiler_params=pltpu.CompilerParams(
            dimension_semantics=("arbitrary", "arbitrary"), vmem_limit_bytes=48 * 1024 * 1024),
        name="proj_gmlp_dsa_prep",
    )(x, row2(pre_g), w_all, row2(ln_g), row2(ln_b), w_s, b_s_b, row2(qn_g), row2(kvn_g), w_q, w_uk_t,
      cos_t, sin_t)

    qt = lambda width: pl.BlockSpec((1, DSA_HEADS, TQ, width), lambda b, t: (b, 0, t, 0))
    seq = lambda width: pl.BlockSpec((1, S, width), lambda b, t: (b, 0, 0), pipeline_mode=pl.Buffered(1))
    o_lat = pl.pallas_call(
        functools.partial(_dsa_kernel, top_k=top_k),
        grid=(B, S // TQ),
        in_specs=[qt(LANES), pl.BlockSpec((1, TQ, LANES), lambda b, t: (b, t, 0)),
                  pl.BlockSpec((1, 1, DSA_HEADS * TQ, Q_CAT), lambda b, t: (b, t, 0, 0)),
                  seq(LANES), seq(Q_CAT), seq(Q_CAT)],
        out_specs=pl.BlockSpec((1, TQ, DSA_HEADS * DSA_KV_RANK), lambda b, t: (b, t, 0)),
        out_shape=jax.ShapeDtypeStruct((B, S, DSA_HEADS * DSA_KV_RANK), bf16),
        scratch_shapes=[pltpu.VMEM((S // KB, TQ, KB), jnp.int32),
                        pltpu.VMEM((IDX_HEADS, TQ, LANES), f32),
                        pltpu.VMEM((TQ, KB), f32),
                        pltpu.VMEM((DSA_HEADS * TQ, LANES), f32),
                        pltpu.VMEM((DSA_HEADS * TQ, Q_CAT), f32),
                        pltpu.VMEM((8, LANES), f32)],
        compiler_params=pltpu.CompilerParams(
            dimension_semantics=("arbitrary", "arbitrary"), vmem_limit_bytes=56 * 1024 * 1024),
        name="dsa_index_select_attend",
    )(qidx, widx, qcat, kidx, kcat, vcat)

    return pl.pallas_call(
        _out_kernel,
        grid=(B, S // TM),
        in_specs=[tok(D), tok(GMLP_WIDTH), tok(DSA_HEADS * DSA_KV_RANK), tok(DSA_WIDTH),
                  _full((DSA_HEADS * DSA_KV_RANK, DSA_WIDTH)), _full((GMLP_WIDTH + DSA_WIDTH, D)), _full((1, D))],
        out_specs=tok(D),
        out_shape=jax.ShapeDtypeStruct((B, S, D), f32),
        compiler_params=pltpu.CompilerParams(
            dimension_semantics=("arbitrary", "arbitrary"), vmem_limit_bytes=32 * 1024 * 1024),
        name="out_proj_norm_residual",
    )(x, out_a, o_lat, gate_b, w_uv_bd, w_out.astype(bf16), row2(post_g))


def kernel(x, w_in, pre_norm_g, post_norm_g, gmlp_ln_g, gmlp_ln_b, gmlp_w_s, gmlp_b_s, dsa_q_norm_g, dsa_kv_norm_g, dsa_w_uq, dsa_w_uk, dsa_w_uv, dsa_w_q_idx, w_out):
    for l in range(w_in.shape[0]):
        x = _layer(x, w_in[l], pre_norm_g[l], post_norm_g[l], gmlp_ln_g[l], gmlp_ln_b[l], gmlp_w_s[l],
                   gmlp_b_s[l], dsa_q_norm_g[l], dsa_kv_norm_g[l], dsa_w_uq[l], dsa_w_uk[l], dsa_w_uv[l],
                   dsa_w_q_idx[l], w_out[l])
    return x
```

```python
import functools
import math

import jax
import jax.numpy as jnp
import numpy as np
from jax import lax
from jax.experimental import pallas as pl
from jax.experimental.pallas import tpu as pltpu

D_MODEL = 1024
CHUNK = 64
EPS = 1e-6
GMLP_GROUPS = 4
GMLP_GROUP_DIM = 128
GMLP_WIDTH = GMLP_GROUPS * GMLP_GROUP_DIM
GMLP_BLOCK = 128
DSA_HEADS = 8
DSA_V_DIM = 64
DSA_WIDTH = DSA_HEADS * DSA_V_DIM
DSA_NOPE_DIM = 64
DSA_ROPE_DIM = 32
DSA_Q_RANK = 256
DSA_KV_RANK = 128
IDX_HEADS = 8
IDX_DIM = 64
TOPK_MAX = 256
ROPE_THETA = 10000.0

LANES = 128
ROPE_HALF = DSA_ROPE_DIM // 2
ROPE_LANE_GROUPS = LANES // DSA_ROPE_DIM
Q_CAT = DSA_KV_RANK + LANES
W_IDX_LANE = IDX_DIM

TM = 512
TQ = 512
ROW_TILE = 128
KB = 512
HEAD_GROUP = 1
NEG_BIG = -1e30
ROW_SUM_MIN = 1e-26
BOUND_SLACK = 1.02
KEY_NONE = -2 ** 31
I16_MIN, I16_MAX = -2 ** 15, 2 ** 15 - 1
COUNT_ROWS = 256
MAX_BISECT = 20

_C_UV, _C_ZA, _C_CQ, _C_CKV = 0, 1024, 1536, 1792
_C_KR, _C_KRS, _C_SLAB, _C_ZB, _C_END = 1920, 2048, 2176, 2304, 2816
_Q_NOPE, _Q_ROPE, _Q_ROPES, _Q_IDX, _Q_END = 0, 1024, 1280, 1536, 2560

_NT = (((1,), (1,)), ((), ()))


def _bf16(a):
    return a.astype(jnp.bfloat16)


def _dot(a, b):
    return jnp.dot(a, b, preferred_element_type=jnp.float32)


def _dot_nt(a, b):
    return lax.dot_general(a, b, _NT, preferred_element_type=jnp.float32)


def _chunk_of(pos):
    return lax.shift_right_logical(pos, jnp.int32(CHUNK.bit_length() - 1))


def _sort_key(s):
    bits = pltpu.bitcast(s, jnp.int32)
    return jnp.where(bits < 0, bits ^ jnp.int32(0x7FFFFFFF), bits)


def _silu(z):
    return z / (1.0 + jnp.exp(-z))


def _gelu_exact(a):
    return 0.5 * a * (1.0 + lax.erf(a * np.float32(math.sqrt(0.5))))


def _proj_kernel(x_ref, pre_g_ref, w_all_ref, ln_g_ref, ln_b_ref, w_s_ref, b_s_ref,
                 qn_g_ref, kvn_g_ref, w_q_ref, w_uk_ref, cos_ref, sin_ref,
                 out_a_ref, gate_b_ref, qidx_ref, widx_ref, qcat_ref, kidx_ref, kcat_ref, vcat_ref):
    x = x_ref[0]
    h = x * lax.rsqrt(jnp.mean(x * x, axis=-1, keepdims=True) + EPS) * pre_g_ref[...]
    proj = _dot(_bf16(h), w_all_ref[...])

    uv = _gelu_exact(proj[:, _C_UV:_C_ZA])
    u, v = uv[:, :GMLP_WIDTH], uv[:, GMLP_WIDTH:]
    mu = jnp.mean(v, axis=-1, keepdims=True)
    vc = v - mu
    var = jnp.mean(vc * vc, axis=-1, keepdims=True)
    vn = _bf16(vc * lax.rsqrt(var + EPS) * ln_g_ref[...] + ln_b_ref[...])
    gate_a = _silu(proj[:, _C_ZA:_C_CQ])
    t_chunk = _chunk_of(lax.broadcasted_iota(jnp.int32, (GMLP_BLOCK, GMLP_BLOCK), 0))
    s_chunk = _chunk_of(lax.broadcasted_iota(jnp.int32, (GMLP_BLOCK, GMLP_BLOCK), 1))
    for g in range(GMLP_GROUPS):
        w_g = _bf16(jnp.where(s_chunk <= t_chunk, w_s_ref[g], 0.0))
        cols = slice(g * GMLP_GROUP_DIM, (g + 1) * GMLP_GROUP_DIM)
        for r in range(TM // GMLP_BLOCK):
            rows = slice(r * GMLP_BLOCK, (r + 1) * GMLP_BLOCK)
            y = _dot(w_g, vn[rows, cols]) + b_s_ref[g]
            out_a_ref[0, rows, cols] = _bf16(u[rows, cols] * y * gate_a[rows, cols])

    gate_b_ref[0] = _silu(proj[:, _C_ZB:_C_END])

    c_q = proj[:, _C_CQ:_C_CKV]
    c_q = c_q * lax.rsqrt(jnp.mean(c_q * c_q, axis=-1, keepdims=True) + EPS) * qn_g_ref[...]
    c_kv = proj[:, _C_CKV:_C_KR]
    c_kv = _bf16(c_kv * lax.rsqrt(jnp.mean(c_kv * c_kv, axis=-1, keepdims=True) + EPS) * kvn_g_ref[...])
    cos = cos_ref[...]
    sin = sin_ref[...]
    k_rope = proj[:, _C_KR:_C_KRS] * cos[:, :LANES] + proj[:, _C_KRS:_C_SLAB] * sin[:, :LANES]
    slab = proj[:, _C_SLAB:_C_ZB]
    kidx_ref[0] = _bf16(slab)
    widx_ref[0] = slab
    kcat_ref[0, :, :DSA_KV_RANK] = c_kv
    kcat_ref[0, :, DSA_KV_RANK:] = _bf16(k_rope)
    lane = lax.broadcasted_iota(jnp.int32, (TM, LANES), 1)
    vcat_ref[0, :, :DSA_KV_RANK] = c_kv
    vcat_ref[0, :, DSA_KV_RANK:] = jnp.where(lane == 0, 1.0, 0.0).astype(jnp.bfloat16)

    q_all = _dot(_bf16(c_q), w_q_ref[...])
    q_rope = q_all[:, _Q_ROPE:_Q_ROPES] * cos + q_all[:, _Q_ROPES:_Q_IDX] * sin
    scale = np.float32(1.0 / math.sqrt(DSA_NOPE_DIM + DSA_ROPE_DIM))
    for hd in range(DSA_HEADS):
        cols = slice(_Q_NOPE + hd * LANES, _Q_NOPE + (hd + 1) * LANES)
        q_lat = _dot(_bf16(q_all[:, cols]), w_uk_ref[hd])
        grp, sub = divmod(hd, ROPE_LANE_GROUPS)
        own = (lane >= sub * DSA_ROPE_DIM) & (lane < (sub + 1) * DSA_ROPE_DIM)
        q_r = jnp.where(own, q_rope[:, grp * LANES:(grp + 1) * LANES] * scale, 0.0)
        for t in range(TM // TQ):
            src = slice(t * TQ, (t + 1) * TQ)
            dst = slice(hd * TQ, (hd + 1) * TQ)
            qcat_ref[0, t, dst, :DSA_KV_RANK] = _bf16(q_lat[src] * scale)
            qcat_ref[0, t, dst, DSA_KV_RANK:] = _bf16(q_r[src])
        icol = slice(_Q_IDX + hd * LANES, _Q_IDX + (hd + 1) * LANES)
        qidx_ref[0, hd] = _bf16(q_all[:, icol])


def _dsa_kernel(qidx_ref, widx_ref, qcat_ref, kidx_ref, kcat_ref, vcat_ref, o_ref,
                hi_ref, lo_ref, wrep_ref, bias_ref, m_ref, acc_ref, kmax_ref, *, top_k):
    i = pl.program_id(1)
    n_blocks = (i * TQ) // KB + 1
    chunks = [slice(c * LANES, (c + 1) * LANES) for c in range(KB // LANES)]
    count_tiles = [slice(r * COUNT_ROWS, (r + 1) * COUNT_ROWS) for r in range(TQ // COUNT_ROWS)]
    idx_scale = np.float32(IDX_HEADS ** -0.5 * IDX_DIM ** -0.5)
    i16 = jnp.int16

    row = lax.broadcasted_iota(jnp.int32, (TQ, LANES), 0)
    row_t = lax.broadcasted_iota(jnp.int32, (ROW_TILE, LANES), 0)
    lane_t = lax.broadcasted_iota(jnp.int32, (ROW_TILE, LANES), 1)
    n_adm = (_chunk_of(i * TQ + row) + 1) * CHUNK

    def lane_all(op, a):
        return jnp.broadcast_to(op(a, axis=1, keepdims=True), a.shape)

    def half(key):
        return lax.shift_right_arithmetic(key, jnp.int32(16))

    w = widx_ref[0] * idx_scale
    for hd in range(IDX_HEADS):
        wrep_ref[hd] = jnp.broadcast_to(w[:, W_IDX_LANE + hd:W_IDX_LANE + hd + 1], (TQ, LANES))

    def score_block(kb, carry):
        k = kidx_ref[0, pl.ds(pl.multiple_of(kb * KB, KB), KB), :]
        grp_max = [[], []]
        for r in range(TQ // ROW_TILE):
            rows = slice(r * ROW_TILE, (r + 1) * ROW_TILE)
            acc = [jnp.zeros((ROW_TILE, LANES), jnp.float32) for _ in chunks]
            for hd in range(IDX_HEADS):
                logit = _dot_nt(qidx_ref[0, hd, rows, :], k)
                w_hd = wrep_ref[hd, rows, :]
                for c, cols in enumerate(chunks):
                    acc[c] = acc[c] + w_hd * jnp.maximum(logit[:, cols], 0.0)
            part = [carry[0][rows], carry[1][rows]]
            q_chunk_r = _chunk_of(i * TQ + r * ROW_TILE + row_t)
            for c, cols in enumerate(chunks):
                adm = _chunk_of(kb * KB + c * LANES + lane_t) <= q_chunk_r
                key = jnp.where(adm, _sort_key(acc[c]), KEY_NONE)
                hi_ref[kb, rows, cols] = half(key).astype(i16)
                lo_ref[kb, rows, cols] = ((key & 0xFFFF) + I16_MIN).astype(i16)
                part[c % 2] = jnp.maximum(part[c % 2], jnp.where(adm, acc[c], -jnp.inf))
            grp_max[0].append(part[0])
            grp_max[1].append(part[1])
        return jnp.concatenate(grp_max[0], axis=0), jnp.concatenate(grp_max[1], axis=0)

    neg_inf = jnp.full((TQ, LANES), -jnp.inf, jnp.float32)
    grp_a, grp_b = lax.fori_loop(0, n_blocks, score_block, (neg_inf, neg_inf))

    def count_ge(plane_ref, mid):
        parts = []
        for rows in count_tiles:
            mid_r = mid[rows].astype(i16)

            def body(kb, cnt, rows=rows, mid_r=mid_r):
                for cols in chunks:
                    cnt = cnt + jnp.where(plane_ref[kb, rows, cols] >= mid_r, i16(1), i16(0))
                return cnt
            cnt = lax.fori_loop(0, n_blocks, body, jnp.zeros((COUNT_ROWS, LANES), i16))
            parts.append(cnt.astype(jnp.float32))
        return lane_all(jnp.sum, jnp.concatenate(parts, axis=0))

    def search(plane_ref, lo, hi, want):
        def mean(lo, hi):
            return lax.shift_right_arithmetic(lo + hi, jnp.int32(1))

        def cond(state):
            it, lo, _, mid, _ = state
            return jnp.logical_and(it < MAX_BISECT, jnp.max(jnp.where(mid != lo, 1.0, 0.0)) > 0.0)

        def body(state):
            it, lo, hi, mid, c_hi = state
            cnt = count_ge(plane_ref, mid)
            ge = cnt >= want
            lo = jnp.where(ge, mid, lo)
            hi = jnp.where(cnt == want, mid, jnp.where(ge, hi, mid))
            return it + 1, lo, hi, mean(lo, hi), jnp.where(ge, c_hi, cnt)

        state = (jnp.int32(0), lo, hi, mean(lo, hi), jnp.zeros((TQ, LANES), jnp.float32))
        _, lo, hi, _, c_hi = lax.while_loop(cond, body, state)
        return lo, hi, c_hi

    kf = jnp.full((TQ, LANES), top_k, jnp.float32)
    select_all = n_adm <= top_k
    lo_key = _sort_key(lane_all(jnp.min, jnp.minimum(grp_a, grp_b)))
    max_key = _sort_key(lane_all(jnp.max, jnp.maximum(grp_a, grp_b)))
    beta, hi_a, c_above = search(hi_ref, jnp.where(select_all, I16_MIN, half(lo_key)),
                                 jnp.where(select_all, I16_MIN, half(max_key) + 1), kf)
    whole_bucket = hi_a == beta
    beta16 = beta.astype(i16)

    def rebase(kb, carry):
        for rows in count_tiles:
            for cols in chunks:
                in_bucket = hi_ref[kb, rows, cols] == beta16[rows]
                lo_ref[kb, rows, cols] = jnp.where(in_bucket, lo_ref[kb, rows, cols], i16(I16_MIN))
        return carry
    lax.fori_loop(0, n_blocks, rebase, 0)

    done = jnp.where(select_all, I16_MAX, I16_MIN)
    want_b = kf - c_above
    tau, hi_b, c_top = search(lo_ref, jnp.where(whole_bucket, done, I16_MIN),
                              jnp.where(whole_bucket, done, I16_MAX + 1), want_b)
    tau16 = tau.astype(i16)
    tie = hi_b != tau

    @pl.when(jnp.max(jnp.where(tie, 1.0, 0.0)) > 0.0)
    def _():
        need = want_b - c_top
        tie16 = jnp.where(tie, 1, 0).astype(i16)
        upto = (lax.broadcasted_iota(jnp.int32, (KB, KB), 0)
                <= lax.broadcasted_iota(jnp.int32, (KB, KB), 1)).astype(jnp.bfloat16)

        def strike(kb, seen):
            out = []
            for r in range(TQ // ROW_TILE):
                rows = slice(r * ROW_TILE, (r + 1) * ROW_TILE)
                high = [hi_ref[kb, rows, cols] for cols in chunks]
                band = [jnp.where((h == beta16[rows]) & (lo_ref[kb, rows, cols] == tau16[rows])
                                  & (tie16[rows] > 0), i16(1), i16(0)).astype(jnp.float32)
                        for h, cols in zip(high, chunks)]
                rank = _dot(jnp.concatenate([_bf16(b) for b in band], axis=1), upto)
                for c, cols in enumerate(chunks):
                    late = band[c] * jnp.where(rank[:, cols] + seen[rows] > need[rows], 1.0, 0.0)
                    late16 = late.astype(jnp.int32).astype(i16)
                    hi_ref[kb, rows, cols] = jnp.where(late16 > 0, i16(I16_MIN), high[c])
                out.append(seen[rows] + lane_all(jnp.sum, functools.reduce(jnp.add, band)))
            return jnp.concatenate(out, axis=0)
        lax.fori_loop(0, n_blocks, strike, jnp.zeros((TQ, LANES), jnp.float32))

    @pl.when(i == 0)
    def _():
        def norm_block(kb, run):
            kc = kcat_ref[0, pl.ds(pl.multiple_of(kb * KB, KB), KB), :].astype(jnp.float32)
            return jnp.maximum(run, jnp.sum(kc * kc, axis=1, keepdims=True))
        run = lax.fori_loop(0, kcat_ref.shape[1] // KB, norm_block, jnp.zeros((KB, 1), jnp.float32))
        kmax_ref[...] = jnp.broadcast_to(jnp.max(run, axis=0, keepdims=True), kmax_ref.shape)

    k_norm2 = kmax_ref[0:1, :]
    q_all = qcat_ref[0, 0]
    q_norm2 = _dot(q_all * q_all, jnp.ones((Q_CAT, LANES), jnp.bfloat16))
    qk2 = jnp.maximum(q_norm2 * k_norm2, 1e-30)
    m_ref[...] = qk2 * lax.rsqrt(qk2) * BOUND_SLACK

    def set_bias(kb):
        for cols in chunks:
            high = hi_ref[kb, :, cols]
            keep = (high > beta16) | ((high == beta16) & (lo_ref[kb, :, cols] >= tau16))
            bias_ref[:, cols] = jnp.where(keep, i16(0), i16(-1)).astype(jnp.float32) * np.float32(-NEG_BIG)

    def bias_rows(cols):
        return jnp.concatenate([bias_ref[:, cols]] * HEAD_GROUP, axis=0)

    group_rows = [slice(g * HEAD_GROUP * TQ, (g + 1) * HEAD_GROUP * TQ) for g in range(DSA_HEADS // HEAD_GROUP)]

    def attend():
        acc_ref[...] = jnp.zeros(acc_ref.shape, jnp.float32)

        def attn_block(kb, carry):
            rows = pl.ds(pl.multiple_of(kb * KB, KB), KB)
            kc = kcat_ref[0, rows, :]
            vc = vcat_ref[0, rows, :]
            set_bias(kb)
            for grp in group_rows:
                s = _dot_nt(qcat_ref[0, 0, grp, :], kc)
                m = m_ref[grp, :]
                p = [_bf16(jnp.exp(s[:, cols] + bias_rows(cols) - m)) for cols in chunks]
                acc_ref[grp, :] += _dot(jnp.concatenate(p, axis=1), vc)
            return carry
        lax.fori_loop(0, n_blocks, attn_block, 0)
        return jnp.min(acc_ref[:, DSA_KV_RANK:DSA_KV_RANK + 1])

    @pl.when(attend() < ROW_SUM_MIN)
    def _():
        m_ref[...] = jnp.full(m_ref.shape, NEG_BIG, jnp.float32)

        def max_block(kb, carry):
            kc = kcat_ref[0, pl.ds(pl.multiple_of(kb * KB, KB), KB), :]
            set_bias(kb)
            for grp in group_rows:
                s = _dot_nt(qcat_ref[0, 0, grp, :], kc)
                sm = [s[:, cols] + bias_rows(cols) for cols in chunks]
                m_ref[grp, :] = jnp.maximum(m_ref[grp, :], functools.reduce(jnp.maximum, sm))
            return carry
        lax.fori_loop(0, n_blocks, max_block, 0)
        m_ref[...] = lane_all(jnp.max, m_ref[...])
        attend()

    for hd in range(DSA_HEADS):
        a = acc_ref[hd * TQ:(hd + 1) * TQ, :]
        denom = jnp.broadcast_to(a[:, DSA_KV_RANK:DSA_KV_RANK + 1], (TQ, DSA_KV_RANK))
        o_ref[0, :, hd * DSA_KV_RANK:(hd + 1) * DSA_KV_RANK] = _bf16(a[:, :DSA_KV_RANK] / denom)


def _out_kernel(x_ref, out_a_ref, o_lat_ref, gate_b_ref, w_uv_ref, w_out_ref, post_g_ref, y_ref):
    o = _dot(o_lat_ref[0], w_uv_ref[...])
    out_b = _bf16(o * gate_b_ref[0])
    y = _dot(out_a_ref[0], w_out_ref[:GMLP_WIDTH, :]) + _dot(out_b, w_out_ref[GMLP_WIDTH:, :])
    y = y * lax.rsqrt(jnp.mean(y * y, axis=-1, keepdims=True) + EPS) * post_g_ref[...]
    y_ref[0] = x_ref[0] + y


def _full(shape):
    return pl.BlockSpec(shape, lambda b, t: (0,) * len(shape))


def _layer(x, w_in, pre_g, post_g, ln_g, ln_b, w_s, b_s, qn_g, kvn_g, w_uq, w_uk, w_uv, w_q_idx, w_out):
    B, S, D = x.shape
    assert D == D_MODEL and S % KB == 0 and S % TM == 0 and TM % TQ == 0 and KB % TQ == 0
    top_k = min(TOPK_MAX, S // 4)
    assert top_k <= 2 * LANES
    f32, bf16 = jnp.float32, jnp.bfloat16

    kr = w_in[:, 1920:1952]
    kr_sw = jnp.concatenate([kr[:, ROPE_HALF:], kr[:, :ROPE_HALF]], axis=1)
    w_all = jnp.concatenate([
        w_in[:, :1920], jnp.tile(kr, (1, ROPE_LANE_GROUPS)), jnp.tile(kr_sw, (1, ROPE_LANE_GROUPS)),
        w_in[:, 1952:2024], jnp.zeros((D, LANES - IDX_DIM - IDX_HEADS), f32), w_in[:, 2024:]], axis=1).astype(bf16)
    wq3 = w_uq.reshape(DSA_Q_RANK, DSA_HEADS, DSA_NOPE_DIM + DSA_ROPE_DIM)
    nope = jnp.pad(wq3[:, :, :DSA_NOPE_DIM], ((0, 0), (0, 0), (0, LANES - DSA_NOPE_DIM)))
    rope = wq3[:, :, DSA_NOPE_DIM:]
    rope_sw = jnp.concatenate([rope[:, :, ROPE_HALF:], rope[:, :, :ROPE_HALF]], axis=2)
    wqi = jnp.pad(w_q_idx.reshape(DSA_Q_RANK, IDX_HEADS, IDX_DIM), ((0, 0), (0, 0), (0, LANES - IDX_DIM)))
    w_q = jnp.concatenate([nope.reshape(DSA_Q_RANK, -1), rope.reshape(DSA_Q_RANK, -1),
                           rope_sw.reshape(DSA_Q_RANK, -1), wqi.reshape(DSA_Q_RANK, -1)], axis=1).astype(bf16)
    w_uk_t = jnp.pad(jnp.transpose(w_uk, (1, 2, 0)), ((0, 0), (0, LANES - DSA_NOPE_DIM), (0, 0))).astype(bf16)
    eye = jnp.eye(DSA_HEADS, dtype=f32)
    w_uv_bd = (jnp.transpose(w_uv, (1, 0, 2))[:, :, None, :] * eye[:, None, :, None]).reshape(
        DSA_HEADS * DSA_KV_RANK, DSA_WIDTH).astype(bf16)
    b_s_b = jnp.broadcast_to(b_s[:, :, None], (GMLP_GROUPS, GMLP_BLOCK, GMLP_GROUP_DIM))

    pos = jnp.arange(S, dtype=f32)
    inv_freq = ROPE_THETA ** (-jnp.arange(0, DSA_ROPE_DIM, 2, dtype=f32) / DSA_ROPE_DIM)
    ang = pos[:, None] * inv_freq[None, :]
    cos_t = jnp.tile(jnp.concatenate([jnp.cos(ang), jnp.cos(ang)], axis=1), (1, DSA_HEADS))
    sin_t = jnp.tile(jnp.concatenate([-jnp.sin(ang), jnp.sin(ang)], axis=1), (1, DSA_HEADS))

    row2 = lambda a: a.reshape(1, -1)
    tok = lambda width: pl.BlockSpec((1, TM, width), lambda b, t: (b, t, 0))
    hm = lambda width: pl.BlockSpec((1, DSA_HEADS, TM, width), lambda b, t: (b, 0, t, 0))

    out_a, gate_b, qidx, widx, qcat, kidx, kcat, vcat = pl.pallas_call(
        _proj_kernel,
        grid=(B, S // TM),
        in_specs=[tok(D), _full((1, D)), _full((D, _C_END)), _full((1, GMLP_WIDTH)), _full((1, GMLP_WIDTH)),
                  _full((GMLP_GROUPS, GMLP_BLOCK, GMLP_BLOCK)), _full((GMLP_GROUPS, GMLP_BLOCK, GMLP_GROUP_DIM)),
                  _full((1, DSA_Q_RANK)), _full((1, DSA_KV_RANK)), _full((DSA_Q_RANK, _Q_END)),
                  _full((DSA_HEADS, LANES, DSA_KV_RANK)),
                  pl.BlockSpec((TM, 2 * LANES), lambda b, t: (t, 0)),
                  pl.BlockSpec((TM, 2 * LANES), lambda b, t: (t, 0))],
        out_specs=[tok(GMLP_WIDTH), tok(DSA_WIDTH), hm(LANES), tok(LANES),
                   pl.BlockSpec((1, TM // TQ, DSA_HEADS * TQ, Q_CAT), lambda b, t: (b, t, 0, 0)),
                   tok(LANES), tok(Q_CAT), tok(Q_CAT)],
        out_shape=[jax.ShapeDtypeStruct((B, S, GMLP_WIDTH), bf16),
                   jax.ShapeDtypeStruct((B, S, DSA_WIDTH), f32),
                   jax.ShapeDtypeStruct((B, DSA_HEADS, S, LANES), bf16),
                   jax.ShapeDtypeStruct((B, S, LANES), f32),
                   jax.ShapeDtypeStruct((B, S // TQ, DSA_HEADS * TQ, Q_CAT), bf16),
                   jax.ShapeDtypeStruct((B, S, LANES), bf16),
                   jax.ShapeDtypeStruct((B, S, Q_CAT), bf16),
                   jax.ShapeDtypeStruct((B, S, Q_CAT), bf16)],
        compiler_params=pltpu.CompilerParams(
            dimension_semantics=("arbitrary", "arbitrary"), vmem_limit_bytes=48 * 1024 * 1024),
        name="proj_gmlp_dsa_prep",
    )(x, row2(pre_g), w_all, row2(ln_g), row2(ln_b), w_s, b_s_b, row2(qn_g), row2(kvn_g), w_q, w_uk_t,
      cos_t, sin_t)

    qt = lambda width: pl.BlockSpec((1, DSA_HEADS, TQ, width), lambda b, t: (b, 0, t, 0))
    seq = lambda width: pl.BlockSpec((1, S, width), lambda b, t: (b, 0, 0), pipeline_mode=pl.Buffered(1))
    o_lat = pl.pallas_call(
        functools.partial(_dsa_kernel, top_k=top_k),
        grid=(B, S // TQ),
        in_specs=[qt(LANES), pl.BlockSpec((1, TQ, LANES), lambda b, t: (b, t, 0)),
                  pl.BlockSpec((1, 1, DSA_HEADS * TQ, Q_CAT), lambda b, t: (b, t, 0, 0)),
                  seq(LANES), seq(Q_CAT), seq(Q_CAT)],
        out_specs=pl.BlockSpec((1, TQ, DSA_HEADS * DSA_KV_RANK), lambda b, t: (b, t, 0)),
        out_shape=jax.ShapeDtypeStruct((B, S, DSA_HEADS * DSA_KV_RANK), bf16),
        scratch_shapes=[pltpu.VMEM((S // KB, TQ, KB), jnp.int16),
                        pltpu.VMEM((S // KB, TQ, KB), jnp.int16),
                        pltpu.VMEM((IDX_HEADS, TQ, LANES), f32),
                        pltpu.VMEM((TQ, KB), f32),
                        pltpu.VMEM((DSA_HEADS * TQ, LANES), f32),
                        pltpu.VMEM((DSA_HEADS * TQ, Q_CAT), f32),
                        pltpu.VMEM((8, LANES), f32)],
        compiler_params=pltpu.CompilerParams(
            dimension_semantics=("arbitrary", "arbitrary"), vmem_limit_bytes=56 * 1024 * 1024),
        name="dsa_index_select_attend",
    )(qidx, widx, qcat, kidx, kcat, vcat)

    return pl.pallas_call(
        _out_kernel,
        grid=(B, S // TM),
        in_specs=[tok(D), tok(GMLP_WIDTH), tok(DSA_HEADS * DSA_KV_RANK), tok(DSA_WIDTH),
                  _full((DSA_HEADS * DSA_KV_RANK, DSA_WIDTH)), _full((GMLP_WIDTH + DSA_WIDTH, D)), _full((1, D))],
        out_specs=tok(D),
        out_shape=jax.ShapeDtypeStruct((B, S, D), f32),
        compiler_params=pltpu.CompilerParams(
            dimension_semantics=("arbitrary", "arbitrary"), vmem_limit_bytes=32 * 1024 * 1024),
        name="out_proj_norm_residual",
    )(x, out_a, o_lat, gate_b, w_uv_bd, w_out.astype(bf16), row2(post_g))


def kernel(x, w_in, pre_norm_g, post_norm_g, gmlp_ln_g, gmlp_ln_b, gmlp_w_s, gmlp_b_s, dsa_q_norm_g, dsa_kv_norm_g, dsa_w_uq, dsa_w_uk, dsa_w_uv, dsa_w_q_idx, w_out):
    for l in range(w_in.shape[0]):
        x = _layer(x, w_in[l], pre_norm_g[l], post_norm_g[l], gmlp_ln_g[l], gmlp_ln_b[l], gmlp_w_s[l],
                   gmlp_b_s[l], dsa_q_norm_g[l], dsa_kv_norm_g[l], dsa_w_uq[l], dsa_w_uk[l], dsa_w_uv[l],
                   dsa_w_q_idx[l], w_out[l])
    return x
```

```python
import functools
import math

import jax
import jax.numpy as jnp
import numpy as np
from jax import lax
from jax.experimental import pallas as pl
from jax.experimental.pallas import tpu as pltpu

D_MODEL = 1024
CHUNK = 64
EPS = 1e-6
GMLP_GROUPS = 4
GMLP_GROUP_DIM = 128
GMLP_WIDTH = GMLP_GROUPS * GMLP_GROUP_DIM
GMLP_BLOCK = 128
DSA_HEADS = 8
DSA_V_DIM = 64
DSA_WIDTH = DSA_HEADS * DSA_V_DIM
DSA_NOPE_DIM = 64
DSA_ROPE_DIM = 32
DSA_Q_RANK = 256
DSA_KV_RANK = 128
IDX_HEADS = 8
IDX_DIM = 64
TOPK_MAX = 256
ROPE_THETA = 10000.0

LANES = 128
ROPE_HALF = DSA_ROPE_DIM // 2
ROPE_LANE_GROUPS = LANES // DSA_ROPE_DIM
Q_CAT = DSA_KV_RANK + LANES
W_IDX_LANE = IDX_DIM

TM = 512
TQ = 512
ROW_TILE = 128
KB = 512
HEAD_GROUP = 1
NEG_BIG = -1e30
ROW_SUM_MIN = 1e-26
BOUND_SLACK = 1.02
KEY_NONE = -2 ** 31
I16_MIN, I16_MAX = -2 ** 15, 2 ** 15 - 1
COUNT_ROWS = 256
MAX_BISECT = 20

_C_UV, _C_ZA, _C_CQ, _C_CKV = 0, 1024, 1536, 1792
_C_KR, _C_KRS, _C_SLAB, _C_ZB, _C_END = 1920, 2048, 2176, 2304, 2816
_Q_NOPE, _Q_ROPE, _Q_ROPES, _Q_IDX, _Q_END = 0, 1024, 1280, 1536, 2560

_NT = (((1,), (1,)), ((), ()))


def _bf16(a):
    return a.astype(jnp.bfloat16)


def _dot(a, b):
    return jnp.dot(a, b, preferred_element_type=jnp.float32)


def _dot_nt(a, b):
    return lax.dot_general(a, b, _NT, preferred_element_type=jnp.float32)


def _chunk_of(pos):
    return lax.shift_right_logical(pos, jnp.int32(CHUNK.bit_length() - 1))


def _sort_key(s):
    bits = pltpu.bitcast(s, jnp.int32)
    return jnp.where(bits < 0, bits ^ jnp.int32(0x7FFFFFFF), bits)


def _silu(z):
    return z / (1.0 + jnp.exp(-z))


def _gelu_exact(a):
    return 0.5 * a * (1.0 + lax.erf(a * np.float32(math.sqrt(0.5))))


def _proj_kernel(x_ref, pre_g_ref, w_all_ref, ln_g_ref, ln_b_ref, w_s_ref, b_s_ref,
                 qn_g_ref, kvn_g_ref, w_q_ref, w_uk_ref, cos_ref, sin_ref,
                 out_a_ref, gate_b_ref, qidx_ref, widx_ref, qcat_ref, kidx_ref, kcat_ref, vcat_ref):
    x = x_ref[0]
    h = x * lax.rsqrt(jnp.mean(x * x, axis=-1, keepdims=True) + EPS) * pre_g_ref[...]
    proj = _dot(_bf16(h), w_all_ref[...])

    uv = _gelu_exact(proj[:, _C_UV:_C_ZA])
    u, v = uv[:, :GMLP_WIDTH], uv[:, GMLP_WIDTH:]
    mu = jnp.mean(v, axis=-1, keepdims=True)
    vc = v - mu
    var = jnp.mean(vc * vc, axis=-1, keepdims=True)
    vn = _bf16(vc * lax.rsqrt(var + EPS) * ln_g_ref[...] + ln_b_ref[...])
    gate_a = _silu(proj[:, _C_ZA:_C_CQ])
    t_chunk = _chunk_of(lax.broadcasted_iota(jnp.int32, (GMLP_BLOCK, GMLP_BLOCK), 0))
    s_chunk = _chunk_of(lax.broadcasted_iota(jnp.int32, (GMLP_BLOCK, GMLP_BLOCK), 1))
    for g in range(GMLP_GROUPS):
        w_g = _bf16(jnp.where(s_chunk <= t_chunk, w_s_ref[g], 0.0))
        cols = slice(g * GMLP_GROUP_DIM, (g + 1) * GMLP_GROUP_DIM)
        for r in range(TM // GMLP_BLOCK):
            rows = slice(r * GMLP_BLOCK, (r + 1) * GMLP_BLOCK)
            y = _dot(w_g, vn[rows, cols]) + b_s_ref[g]
            out_a_ref[0, rows, cols] = _bf16(u[rows, cols] * y * gate_a[rows, cols])

    gate_b_ref[0] = _silu(proj[:, _C_ZB:_C_END])

    c_q = proj[:, _C_CQ:_C_CKV]
    c_q = c_q * lax.rsqrt(jnp.mean(c_q * c_q, axis=-1, keepdims=True) + EPS) * qn_g_ref[...]
    c_kv = proj[:, _C_CKV:_C_KR]
    c_kv = _bf16(c_kv * lax.rsqrt(jnp.mean(c_kv * c_kv, axis=-1, keepdims=True) + EPS) * kvn_g_ref[...])
    cos = cos_ref[...]
    sin = sin_ref[...]
    k_rope = proj[:, _C_KR:_C_KRS] * cos[:, :LANES] + proj[:, _C_KRS:_C_SLAB] * sin[:, :LANES]
    slab = proj[:, _C_SLAB:_C_ZB]
    kidx_ref[0] = _bf16(slab)
    widx_ref[0] = slab
    kcat_ref[0, :, :DSA_KV_RANK] = c_kv
    kcat_ref[0, :, DSA_KV_RANK:] = _bf16(k_rope)
    lane = lax.broadcasted_iota(jnp.int32, (TM, LANES), 1)
    vcat_ref[0, :, :DSA_KV_RANK] = c_kv
    vcat_ref[0, :, DSA_KV_RANK:] = jnp.where(lane == 0, 1.0, 0.0).astype(jnp.bfloat16)

    q_all = _dot(_bf16(c_q), w_q_ref[...])
    q_rope = q_all[:, _Q_ROPE:_Q_ROPES] * cos + q_all[:, _Q_ROPES:_Q_IDX] * sin
    scale = np.float32(1.0 / math.sqrt(DSA_NOPE_DIM + DSA_ROPE_DIM))
    for hd in range(DSA_HEADS):
        cols = slice(_Q_NOPE + hd * LANES, _Q_NOPE + (hd + 1) * LANES)
        q_lat = _dot(_bf16(q_all[:, cols]), w_uk_ref[hd])
        grp, sub = divmod(hd, ROPE_LANE_GROUPS)
        own = (lane >= sub * DSA_ROPE_DIM) & (lane < (sub + 1) * DSA_ROPE_DIM)
        q_r = jnp.where(own, q_rope[:, grp * LANES:(grp + 1) * LANES] * scale, 0.0)
        for t in range(TM // TQ):
            src = slice(t * TQ, (t + 1) * TQ)
            dst = slice(hd * TQ, (hd + 1) * TQ)
            qcat_ref[0, t, dst, :DSA_KV_RANK] = _bf16(q_lat[src] * scale)
            qcat_ref[0, t, dst, DSA_KV_RANK:] = _bf16(q_r[src])
        icol = slice(_Q_IDX + hd * LANES, _Q_IDX + (hd + 1) * LANES)
        qidx_ref[0, hd] = _bf16(q_all[:, icol])


def _dsa_kernel(qidx_ref, widx_ref, qcat_ref, kidx_ref, kcat_ref, vcat_ref, o_ref,
                hi_ref, lo_ref, wrep_ref, bias_ref, m_ref, acc_ref, kmax_ref, *, top_k):
    i = pl.program_id(1)
    n_blocks = (i * TQ) // KB + 1
    chunks = [slice(c * LANES, (c + 1) * LANES) for c in range(KB // LANES)]
    count_tiles = [slice(r * COUNT_ROWS, (r + 1) * COUNT_ROWS) for r in range(TQ // COUNT_ROWS)]
    idx_scale = np.float32(IDX_HEADS ** -0.5 * IDX_DIM ** -0.5)
    i16 = jnp.int16

    row = lax.broadcasted_iota(jnp.int32, (TQ, LANES), 0)
    row_t = lax.broadcasted_iota(jnp.int32, (ROW_TILE, LANES), 0)
    lane_t = lax.broadcasted_iota(jnp.int32, (ROW_TILE, LANES), 1)
    n_adm = (_chunk_of(i * TQ + row) + 1) * CHUNK

    def lane_all(op, a):
        return jnp.broadcast_to(op(a, axis=1, keepdims=True), a.shape)

    def half(key):
        return lax.shift_right_arithmetic(key, jnp.int32(16))

    w = widx_ref[0] * idx_scale
    for hd in range(IDX_HEADS):
        wrep_ref[hd] = jnp.broadcast_to(w[:, W_IDX_LANE + hd:W_IDX_LANE + hd + 1], (TQ, LANES))

    def score_block(kb, carry):
        k = kidx_ref[0, pl.ds(pl.multiple_of(kb * KB, KB), KB), :]
        grp_max = [[], []]
        for r in range(TQ // ROW_TILE):
            rows = slice(r * ROW_TILE, (r + 1) * ROW_TILE)
            acc = [jnp.zeros((ROW_TILE, LANES), jnp.float32) for _ in chunks]
            for hd in range(IDX_HEADS):
                logit = _dot_nt(qidx_ref[0, hd, rows, :], k)
                w_hd = wrep_ref[hd, rows, :]
                for c, cols in enumerate(chunks):
                    acc[c] = acc[c] + w_hd * jnp.maximum(logit[:, cols], 0.0)
            part = [carry[0][rows], carry[1][rows]]
            q_chunk_r = _chunk_of(i * TQ + r * ROW_TILE + row_t)
            for c, cols in enumerate(chunks):
                adm = _chunk_of(kb * KB + c * LANES + lane_t) <= q_chunk_r
                key = jnp.where(adm, _sort_key(acc[c]), KEY_NONE)
                hi_ref[kb, c, rows, :] = half(key).astype(i16)
                lo_ref[kb, c, rows, :] = ((key & 0xFFFF) + I16_MIN).astype(i16)
                part[c % 2] = jnp.maximum(part[c % 2], jnp.where(adm, acc[c], -jnp.inf))
            grp_max[0].append(part[0])
            grp_max[1].append(part[1])
        return jnp.concatenate(grp_max[0], axis=0), jnp.concatenate(grp_max[1], axis=0)

    neg_inf = jnp.full((TQ, LANES), -jnp.inf, jnp.float32)
    grp_a, grp_b = lax.fori_loop(0, n_blocks, score_block, (neg_inf, neg_inf))

    def count_ge(plane_ref, mid):
        parts = []
        for rows in count_tiles:
            mid_r = mid[rows].astype(i16)

            def body(kb, cnt, rows=rows, mid_r=mid_r):
                for c in range(len(chunks)):
                    cnt = cnt + jnp.where(plane_ref[kb, c, rows, :] >= mid_r, i16(1), i16(0))
                return cnt
            cnt = lax.fori_loop(0, n_blocks, body, jnp.zeros((COUNT_ROWS, LANES), i16))
            parts.append(cnt.astype(jnp.float32))
        return lane_all(jnp.sum, jnp.concatenate(parts, axis=0))

    def search(plane_ref, lo, hi, want):
        def mean(lo, hi):
            return lax.shift_right_arithmetic(lo + hi, jnp.int32(1))

        def cond(state):
            it, lo, _, mid, _ = state
            return jnp.logical_and(it < MAX_BISECT, jnp.max(jnp.where(mid != lo, 1.0, 0.0)) > 0.0)

        def body(state):
            it, lo, hi, mid, c_hi = state
            cnt = count_ge(plane_ref, mid)
            ge = cnt >= want
            lo = jnp.where(ge, mid, lo)
            hi = jnp.where(cnt == want, mid, jnp.where(ge, hi, mid))
            return it + 1, lo, hi, mean(lo, hi), jnp.where(ge, c_hi, cnt)

        state = (jnp.int32(0), lo, hi, mean(lo, hi), jnp.zeros((TQ, LANES), jnp.float32))
        _, lo, hi, _, c_hi = lax.while_loop(cond, body, state)
        return lo, hi, c_hi

    kf = jnp.full((TQ, LANES), top_k, jnp.float32)
    select_all = n_adm <= top_k
    lo_key = _sort_key(lane_all(jnp.min, jnp.minimum(grp_a, grp_b)))
    max_key = _sort_key(lane_all(jnp.max, jnp.maximum(grp_a, grp_b)))
    beta, hi_a, c_above = search(hi_ref, jnp.where(select_all, I16_MIN, half(lo_key)),
                                 jnp.where(select_all, I16_MIN, half(max_key) + 1), kf)
    whole_bucket = hi_a == beta
    beta16 = beta.astype(i16)

    def rebase(kb, carry):
        for rows in count_tiles:
            for c in range(len(chunks)):
                in_bucket = hi_ref[kb, c, rows, :] == beta16[rows]
                lo_ref[kb, c, rows, :] = jnp.where(in_bucket, lo_ref[kb, c, rows, :], i16(I16_MIN))
        return carry
    lax.fori_loop(0, n_blocks, rebase, 0)

    done = jnp.where(select_all, I16_MAX, I16_MIN)
    want_b = kf - c_above
    tau, hi_b, c_top = search(lo_ref, jnp.where(whole_bucket, done, I16_MIN),
                              jnp.where(whole_bucket, done, I16_MAX + 1), want_b)
    tau16 = tau.astype(i16)
    tie = hi_b != tau

    @pl.when(jnp.max(jnp.where(tie, 1.0, 0.0)) > 0.0)
    def _():
        need = want_b - c_top
        tie16 = jnp.where(tie, 1, 0).astype(i16)
        upto = (lax.broadcasted_iota(jnp.int32, (KB, KB), 0)
                <= lax.broadcasted_iota(jnp.int32, (KB, KB), 1)).astype(jnp.bfloat16)

        def strike(kb, seen):
            out = []
            for r in range(TQ // ROW_TILE):
                rows = slice(r * ROW_TILE, (r + 1) * ROW_TILE)
                high = [hi_ref[kb, c, rows, :] for c in range(len(chunks))]
                band = [jnp.where((h == beta16[rows]) & (lo_ref[kb, c, rows, :] == tau16[rows])
                                  & (tie16[rows] > 0), i16(1), i16(0)).astype(jnp.float32)
                        for c, h in enumerate(high)]
                rank = _dot(jnp.concatenate([_bf16(b) for b in band], axis=1), upto)
                for c, cols in enumerate(chunks):
                    late = band[c] * jnp.where(rank[:, cols] + seen[rows] > need[rows], 1.0, 0.0)
                    late16 = late.astype(jnp.int32).astype(i16)
                    hi_ref[kb, c, rows, :] = jnp.where(late16 > 0, i16(I16_MIN), high[c])
                out.append(seen[rows] + lane_all(jnp.sum, functools.reduce(jnp.add, band)))
            return jnp.concatenate(out, axis=0)
        lax.fori_loop(0, n_blocks, strike, jnp.zeros((TQ, LANES), jnp.float32))

    @pl.when(i == 0)
    def _():
        def norm_block(kb, run):
            kc = kcat_ref[0, pl.ds(pl.multiple_of(kb * KB, KB), KB), :].astype(jnp.float32)
            return jnp.maximum(run, jnp.sum(kc * kc, axis=1, keepdims=True))
        run = lax.fori_loop(0, kcat_ref.shape[1] // KB, norm_block, jnp.zeros((KB, 1), jnp.float32))
        kmax_ref[...] = jnp.broadcast_to(jnp.max(run, axis=0, keepdims=True), kmax_ref.shape)

    k_norm2 = kmax_ref[0:1, :]
    q_all = qcat_ref[0, 0]
    q_norm2 = _dot(q_all * q_all, jnp.ones((Q_CAT, LANES), jnp.bfloat16))
    qk2 = jnp.maximum(q_norm2 * k_norm2, 1e-30)
    m_ref[...] = qk2 * lax.rsqrt(qk2) * BOUND_SLACK

    def set_bias(kb):
        for c, cols in enumerate(chunks):
            high = hi_ref[kb, c]
            keep = (high > beta16) | ((high == beta16) & (lo_ref[kb, c] >= tau16))
            bias_ref[:, cols] = jnp.where(keep, i16(0), i16(-1)).astype(jnp.float32) * np.float32(-NEG_BIG)

    def bias_rows(cols):
        return jnp.concatenate([bias_ref[:, cols]] * HEAD_GROUP, axis=0)

    group_rows = [slice(g * HEAD_GROUP * TQ, (g + 1) * HEAD_GROUP * TQ) for g in range(DSA_HEADS // HEAD_GROUP)]

    def attend():
        acc_ref[...] = jnp.zeros(acc_ref.shape, jnp.float32)

        def attn_block(kb, carry):
            rows = pl.ds(pl.multiple_of(kb * KB, KB), KB)
            kc = kcat_ref[0, rows, :]
            vc = vcat_ref[0, rows, :]
            set_bias(kb)
            for grp in group_rows:
                s = _dot_nt(qcat_ref[0, 0, grp, :], kc)
                m = m_ref[grp, :]
                p = [_bf16(jnp.exp(s[:, cols] + bias_rows(cols) - m)) for cols in chunks]
                acc_ref[grp, :] += _dot(jnp.concatenate(p, axis=1), vc)
            return carry
        lax.fori_loop(0, n_blocks, attn_block, 0)
        return jnp.min(acc_ref[:, DSA_KV_RANK:DSA_KV_RANK + 1])

    @pl.when(attend() < ROW_SUM_MIN)
    def _():
        m_ref[...] = jnp.full(m_ref.shape, NEG_BIG, jnp.float32)

        def max_block(kb, carry):
            kc = kcat_ref[0, pl.ds(pl.multiple_of(kb * KB, KB), KB), :]
            set_bias(kb)
            for grp in group_rows:
                s = _dot_nt(qcat_ref[0, 0, grp, :], kc)
                sm = [s[:, cols] + bias_rows(cols) for cols in chunks]
                m_ref[grp, :] = jnp.maximum(m_ref[grp, :], functools.reduce(jnp.maximum, sm))
            return carry
        lax.fori_loop(0, n_blocks, max_block, 0)
        m_ref[...] = lane_all(jnp.max, m_ref[...])
        attend()

    for hd in range(DSA_HEADS):
        a = acc_ref[hd * TQ:(hd + 1) * TQ, :]
        denom = jnp.broadcast_to(a[:, DSA_KV_RANK:DSA_KV_RANK + 1], (TQ, DSA_KV_RANK))
        o_ref[0, :, hd * DSA_KV_RANK:(hd + 1) * DSA_KV_RANK] = _bf16(a[:, :DSA_KV_RANK] / denom)


def _out_kernel(x_ref, out_a_ref, o_lat_ref, gate_b_ref, w_uv_ref, w_out_ref, post_g_ref, y_ref):
    o = _dot(o_lat_ref[0], w_uv_ref[...])
    out_b = _bf16(o * gate_b_ref[0])
    y = _dot(out_a_ref[0], w_out_ref[:GMLP_WIDTH, :]) + _dot(out_b, w_out_ref[GMLP_WIDTH:, :])
    y = y * lax.rsqrt(jnp.mean(y * y, axis=-1, keepdims=True) + EPS) * post_g_ref[...]
    y_ref[0] = x_ref[0] + y


def _full(shape):
    return pl.BlockSpec(shape, lambda b, t: (0,) * len(shape))


def _layer(x, w_in, pre_g, post_g, ln_g, ln_b, w_s, b_s, qn_g, kvn_g, w_uq, w_uk, w_uv, w_q_idx, w_out):
    B, S, D = x.shape
    assert D == D_MODEL and S % KB == 0 and S % TM == 0 and TM % TQ == 0 and KB % TQ == 0
    top_k = min(TOPK_MAX, S // 4)
    assert top_k <= 2 * LANES
    f32, bf16 = jnp.float32, jnp.bfloat16

    kr = w_in[:, 1920:1952]
    kr_sw = jnp.concatenate([kr[:, ROPE_HALF:], kr[:, :ROPE_HALF]], axis=1)
    w_all = jnp.concatenate([
        w_in[:, :1920], jnp.tile(kr, (1, ROPE_LANE_GROUPS)), jnp.tile(kr_sw, (1, ROPE_LANE_GROUPS)),
        w_in[:, 1952:2024], jnp.zeros((D, LANES - IDX_DIM - IDX_HEADS), f32), w_in[:, 2024:]], axis=1).astype(bf16)
    wq3 = w_uq.reshape(DSA_Q_RANK, DSA_HEADS, DSA_NOPE_DIM + DSA_ROPE_DIM)
    nope = jnp.pad(wq3[:, :, :DSA_NOPE_DIM], ((0, 0), (0, 0), (0, LANES - DSA_NOPE_DIM)))
    rope = wq3[:, :, DSA_NOPE_DIM:]
    rope_sw = jnp.concatenate([rope[:, :, ROPE_HALF:], rope[:, :, :ROPE_HALF]], axis=2)
    wqi = jnp.pad(w_q_idx.reshape(DSA_Q_RANK, IDX_HEADS, IDX_DIM), ((0, 0), (0, 0), (0, LANES - IDX_DIM)))
    w_q = jnp.concatenate([nope.reshape(DSA_Q_RANK, -1), rope.reshape(DSA_Q_RANK, -1),
                           rope_sw.reshape(DSA_Q_RANK, -1), wqi.reshape(DSA_Q_RANK, -1)], axis=1).astype(bf16)
    w_uk_t = jnp.pad(jnp.transpose(w_uk, (1, 2, 0)), ((0, 0), (0, LANES - DSA_NOPE_DIM), (0, 0))).astype(bf16)
    eye = jnp.eye(DSA_HEADS, dtype=f32)
    w_uv_bd = (jnp.transpose(w_uv, (1, 0, 2))[:, :, None, :] * eye[:, None, :, None]).reshape(
        DSA_HEADS * DSA_KV_RANK, DSA_WIDTH).astype(bf16)
    b_s_b = jnp.broadcast_to(b_s[:, :, None], (GMLP_GROUPS, GMLP_BLOCK, GMLP_GROUP_DIM))

    pos = jnp.arange(S, dtype=f32)
    inv_freq = ROPE_THETA ** (-jnp.arange(0, DSA_ROPE_DIM, 2, dtype=f32) / DSA_ROPE_DIM)
    ang = pos[:, None] * inv_freq[None, :]
    cos_t = jnp.tile(jnp.concatenate([jnp.cos(ang), jnp.cos(ang)], axis=1), (1, DSA_HEADS))
    sin_t = jnp.tile(jnp.concatenate([-jnp.sin(ang), jnp.sin(ang)], axis=1), (1, DSA_HEADS))

    row2 = lambda a: a.reshape(1, -1)
    tok = lambda width: pl.BlockSpec((1, TM, width), lambda b, t: (b, t, 0))
    hm = lambda width: pl.BlockSpec((1, DSA_HEADS, TM, width), lambda b, t: (b, 0, t, 0))

    out_a, gate_b, qidx, widx, qcat, kidx, kcat, vcat = pl.pallas_call(
        _proj_kernel,
        grid=(B, S // TM),
        in_specs=[tok(D), _full((1, D)), _full((D, _C_END)), _full((1, GMLP_WIDTH)), _full((1, GMLP_WIDTH)),
                  _full((GMLP_GROUPS, GMLP_BLOCK, GMLP_BLOCK)), _full((GMLP_GROUPS, GMLP_BLOCK, GMLP_GROUP_DIM)),
                  _full((1, DSA_Q_RANK)), _full((1, DSA_KV_RANK)), _full((DSA_Q_RANK, _Q_END)),
                  _full((DSA_HEADS, LANES, DSA_KV_RANK)),
                  pl.BlockSpec((TM, 2 * LANES), lambda b, t: (t, 0)),
                  pl.BlockSpec((TM, 2 * LANES), lambda b, t: (t, 0))],
        out_specs=[tok(GMLP_WIDTH), tok(DSA_WIDTH), hm(LANES), tok(LANES),
                   pl.BlockSpec((1, TM // TQ, DSA_HEADS * TQ, Q_CAT), lambda b, t: (b, t, 0, 0)),
                   tok(LANES), tok(Q_CAT), tok(Q_CAT)],
        out_shape=[jax.ShapeDtypeStruct((B, S, GMLP_WIDTH), bf16),
                   jax.ShapeDtypeStruct((B, S, DSA_WIDTH), f32),
                   jax.ShapeDtypeStruct((B, DSA_HEADS, S, LANES), bf16),
                   jax.ShapeDtypeStruct((B, S, LANES), f32),
                   jax.ShapeDtypeStruct((B, S // TQ, DSA_HEADS * TQ, Q_CAT), bf16),
                   jax.ShapeDtypeStruct((B, S, LANES), bf16),
                   jax.ShapeDtypeStruct((B, S, Q_CAT), bf16),
                   jax.ShapeDtypeStruct((B, S, Q_CAT), bf16)],
        compiler_params=pltpu.CompilerParams(
            dimension_semantics=("arbitrary", "arbitrary"), vmem_limit_bytes=48 * 1024 * 1024),
        name="proj_gmlp_dsa_prep",
    )(x, row2(pre_g), w_all, row2(ln_g), row2(ln_b), w_s, b_s_b, row2(qn_g), row2(kvn_g), w_q, w_uk_t,
      cos_t, sin_t)

    qt = lambda width: pl.BlockSpec((1, DSA_HEADS, TQ, width), lambda b, t: (b, 0, t, 0))
    seq = lambda width: pl.BlockSpec((1, S, width), lambda b, t: (b, 0, 0), pipeline_mode=pl.Buffered(1))
    o_lat = pl.pallas_call(
        functools.partial(_dsa_kernel, top_k=top_k),
        grid=(B, S // TQ),
        in_specs=[qt(LANES), pl.BlockSpec((1, TQ, LANES), lambda b, t: (b, t, 0)),
                  pl.BlockSpec((1, 1, DSA_HEADS * TQ, Q_CAT), lambda b, t: (b, t, 0, 0)),
                  seq(LANES), seq(Q_CAT), seq(Q_CAT)],
        out_specs=pl.BlockSpec((1, TQ, DSA_HEADS * DSA_KV_RANK), lambda b, t: (b, t, 0)),
        out_shape=jax.ShapeDtypeStruct((B, S, DSA_HEADS * DSA_KV_RANK), bf16),
        scratch_shapes=[pltpu.VMEM((S // KB, KB // LANES, TQ, LANES), jnp.int16),
                        pltpu.VMEM((S // KB, KB // LANES, TQ, LANES), jnp.int16),
                        pltpu.VMEM((IDX_HEADS, TQ, LANES), f32),
                        pltpu.VMEM((TQ, KB), f32),
                        pltpu.VMEM((DSA_HEADS * TQ, LANES), f32),
                        pltpu.VMEM((DSA_HEADS * TQ, Q_CAT), f32),
                        pltpu.VMEM((8, LANES), f32)],
        compiler_params=pltpu.CompilerParams(
            dimension_semantics=("arbitrary", "arbitrary"), vmem_limit_bytes=56 * 1024 * 1024),
        name="dsa_index_select_attend",
    )(qidx, widx, qcat, kidx, kcat, vcat)

    return pl.pallas_call(
        _out_kernel,
        grid=(B, S // TM),
        in_specs=[tok(D), tok(GMLP_WIDTH), tok(DSA_HEADS * DSA_KV_RANK), tok(DSA_WIDTH),
                  _full((DSA_HEADS * DSA_KV_RANK, DSA_WIDTH)), _full((GMLP_WIDTH + DSA_WIDTH, D)), _full((1, D))],
        out_specs=tok(D),
        out_shape=jax.ShapeDtypeStruct((B, S, D), f32),
        compiler_params=pltpu.CompilerParams(
            dimension_semantics=("arbitrary", "arbitrary"), vmem_limit_bytes=32 * 1024 * 1024),
        name="out_proj_norm_residual",
    )(x, out_a, o_lat, gate_b, w_uv_bd, w_out.astype(bf16), row2(post_g))


def kernel(x, w_in, pre_norm_g, post_norm_g, gmlp_ln_g, gmlp_ln_b, gmlp_w_s, gmlp_b_s, dsa_q_norm_g, dsa_kv_norm_g, dsa_w_uq, dsa_w_uk, dsa_w_uv, dsa_w_q_idx, w_out):
    for l in range(w_in.shape[0]):
        x = _layer(x, w_in[l], pre_norm_g[l], post_norm_g[l], gmlp_ln_g[l], gmlp_ln_b[l], gmlp_w_s[l],
                   gmlp_b_s[l], dsa_q_norm_g[l], dsa_kv_norm_g[l], dsa_w_uq[l], dsa_w_uk[l], dsa_w_uv[l],
                   dsa_w_q_idx[l], w_out[l])
    return x
```

```python
import functools
import math

import jax
import jax.numpy as jnp
import numpy as np
from jax import lax
from jax.experimental import pallas as pl
from jax.experimental.pallas import tpu as pltpu

D_MODEL = 1024
CHUNK = 64
EPS = 1e-6
GMLP_GROUPS = 4
GMLP_GROUP_DIM = 128
GMLP_WIDTH = GMLP_GROUPS * GMLP_GROUP_DIM
GMLP_BLOCK = 128
DSA_HEADS = 8
DSA_V_DIM = 64
DSA_WIDTH = DSA_HEADS * DSA_V_DIM
DSA_NOPE_DIM = 64
DSA_ROPE_DIM = 32
DSA_Q_RANK = 256
DSA_KV_RANK = 128
IDX_HEADS = 8
IDX_DIM = 64
TOPK_MAX = 256
ROPE_THETA = 10000.0

LANES = 128
ROPE_HALF = DSA_ROPE_DIM // 2
ROPE_LANE_GROUPS = LANES // DSA_ROPE_DIM
Q_CAT = DSA_KV_RANK + LANES
W_IDX_LANE = IDX_DIM

TM = 512
TQ = 512
ROW_TILE = 128
KB = 512
HEAD_GROUP = 1
NEG_BIG = -1e30
ROW_SUM_MIN = 1e-26
BOUND_SLACK = 1.02
KEY_NONE = -2 ** 31
MAX_BISECT = 40

_C_UV, _C_ZA, _C_CQ, _C_CKV = 0, 1024, 1536, 1792
_C_KR, _C_KRS, _C_SLAB, _C_ZB, _C_END = 1920, 2048, 2176, 2304, 2816
_Q_NOPE, _Q_ROPE, _Q_ROPES, _Q_IDX, _Q_END = 0, 1024, 1280, 1536, 2560

_NT = (((1,), (1,)), ((), ()))


def _bf16(a):
    return a.astype(jnp.bfloat16)


def _dot(a, b):
    return jnp.dot(a, b, preferred_element_type=jnp.float32)


def _dot_nt(a, b):
    return lax.dot_general(a, b, _NT, preferred_element_type=jnp.float32)


def _chunk_of(pos):
    return lax.shift_right_logical(pos, jnp.int32(CHUNK.bit_length() - 1))


def _sort_key(s):
    bits = pltpu.bitcast(s, jnp.int32)
    return jnp.where(bits < 0, bits ^ jnp.int32(0x7FFFFFFF), bits)


def _silu(z):
    return z / (1.0 + jnp.exp(-z))


def _gelu_exact(a):
    return 0.5 * a * (1.0 + lax.erf(a * np.float32(math.sqrt(0.5))))


def _proj_kernel(x_ref, pre_g_ref, w_all_ref, ln_g_ref, ln_b_ref, w_s_ref, b_s_ref,
                 qn_g_ref, kvn_g_ref, w_q_ref, w_uk_ref, cos_ref, sin_ref,
                 out_a_ref, gate_b_ref, qidx_ref, widx_ref, qcat_ref, kidx_ref, kcat_ref, vcat_ref):
    x = x_ref[0]
    h = x * lax.rsqrt(jnp.mean(x * x, axis=-1, keepdims=True) + EPS) * pre_g_ref[...]
    proj = _dot(_bf16(h), w_all_ref[...])

    uv = _gelu_exact(proj[:, _C_UV:_C_ZA])
    u, v = uv[:, :GMLP_WIDTH], uv[:, GMLP_WIDTH:]
    mu = jnp.mean(v, axis=-1, keepdims=True)
    vc = v - mu
    var = jnp.mean(vc * vc, axis=-1, keepdims=True)
    vn = _bf16(vc * lax.rsqrt(var + EPS) * ln_g_ref[...] + ln_b_ref[...])
    gate_a = _silu(proj[:, _C_ZA:_C_CQ])
    t_chunk = _chunk_of(lax.broadcasted_iota(jnp.int32, (GMLP_BLOCK, GMLP_BLOCK), 0))
    s_chunk = _chunk_of(lax.broadcasted_iota(jnp.int32, (GMLP_BLOCK, GMLP_BLOCK), 1))
    for g in range(GMLP_GROUPS):
        w_g = _bf16(jnp.where(s_chunk <= t_chunk, w_s_ref[g], 0.0))
        cols = slice(g * GMLP_GROUP_DIM, (g + 1) * GMLP_GROUP_DIM)
        for r in range(TM // GMLP_BLOCK):
            rows = slice(r * GMLP_BLOCK, (r + 1) * GMLP_BLOCK)
            y = _dot(w_g, vn[rows, cols]) + b_s_ref[g]
            out_a_ref[0, rows, cols] = _bf16(u[rows, cols] * y * gate_a[rows, cols])

    gate_b_ref[0] = _silu(proj[:, _C_ZB:_C_END])

    c_q = proj[:, _C_CQ:_C_CKV]
    c_q = c_q * lax.rsqrt(jnp.mean(c_q * c_q, axis=-1, keepdims=True) + EPS) * qn_g_ref[...]
    c_kv = proj[:, _C_CKV:_C_KR]
    c_kv = _bf16(c_kv * lax.rsqrt(jnp.mean(c_kv * c_kv, axis=-1, keepdims=True) + EPS) * kvn_g_ref[...])
    cos = cos_ref[...]
    sin = sin_ref[...]
    k_rope = proj[:, _C_KR:_C_KRS] * cos[:, :LANES] + proj[:, _C_KRS:_C_SLAB] * sin[:, :LANES]
    slab = proj[:, _C_SLAB:_C_ZB]
    kidx_ref[0] = _bf16(slab)
    widx_ref[0] = slab
    kcat_ref[0, :, :DSA_KV_RANK] = c_kv
    kcat_ref[0, :, DSA_KV_RANK:] = _bf16(k_rope)
    lane = lax.broadcasted_iota(jnp.int32, (TM, LANES), 1)
    vcat_ref[0, :, :DSA_KV_RANK] = c_kv
    vcat_ref[0, :, DSA_KV_RANK:] = jnp.where(lane == 0, 1.0, 0.0).astype(jnp.bfloat16)

    q_all = _dot(_bf16(c_q), w_q_ref[...])
    q_rope = q_all[:, _Q_ROPE:_Q_ROPES] * cos + q_all[:, _Q_ROPES:_Q_IDX] * sin
    scale = np.float32(1.0 / math.sqrt(DSA_NOPE_DIM + DSA_ROPE_DIM))
    for hd in range(DSA_HEADS):
        cols = slice(_Q_NOPE + hd * LANES, _Q_NOPE + (hd + 1) * LANES)
        q_lat = _dot(_bf16(q_all[:, cols]), w_uk_ref[hd])
        grp, sub = divmod(hd, ROPE_LANE_GROUPS)
        own = (lane >= sub * DSA_ROPE_DIM) & (lane < (sub + 1) * DSA_ROPE_DIM)
        q_r = jnp.where(own, q_rope[:, grp * LANES:(grp + 1) * LANES] * scale, 0.0)
        for t in range(TM // TQ):
            src = slice(t * TQ, (t + 1) * TQ)
            dst = slice(hd * TQ, (hd + 1) * TQ)
            qcat_ref[0, t, dst, :DSA_KV_RANK] = _bf16(q_lat[src] * scale)
            qcat_ref[0, t, dst, DSA_KV_RANK:] = _bf16(q_r[src])
        icol = slice(_Q_IDX + hd * LANES, _Q_IDX + (hd + 1) * LANES)
        qidx_ref[0, hd] = _bf16(q_all[:, icol])


def _dsa_kernel(qidx_ref, widx_ref, qcat_ref, kidx_ref, kcat_ref, vcat_ref, o_ref,
                sc_ref, wrep_ref, bias_ref, m_ref, acc_ref, kmax_ref, *, top_k):
    i = pl.program_id(1)
    n_blocks = (i * TQ) // KB + 1
    chunks = [slice(c * LANES, (c + 1) * LANES) for c in range(KB // LANES)]
    idx_scale = np.float32(IDX_HEADS ** -0.5 * IDX_DIM ** -0.5)

    row_t = lax.broadcasted_iota(jnp.int32, (ROW_TILE, LANES), 0)
    lane_t = lax.broadcasted_iota(jnp.int32, (ROW_TILE, LANES), 1)

    def lane_all(op, a):
        return jnp.broadcast_to(op(a, axis=1, keepdims=True), a.shape)

    w = widx_ref[0] * idx_scale
    for hd in range(IDX_HEADS):
        wrep_ref[hd] = jnp.broadcast_to(w[:, W_IDX_LANE + hd:W_IDX_LANE + hd + 1], (TQ, LANES))

    def score_block(kb, carry):
        k = kidx_ref[0, pl.ds(pl.multiple_of(kb * KB, KB), KB), :]
        grp_max = [[], []]
        for r in range(TQ // ROW_TILE):
            rows = slice(r * ROW_TILE, (r + 1) * ROW_TILE)
            acc = [jnp.zeros((ROW_TILE, LANES), jnp.float32) for _ in chunks]
            for hd in range(IDX_HEADS):
                logit = _dot_nt(qidx_ref[0, hd, rows, :], k)
                w_hd = wrep_ref[hd, rows, :]
                for c, cols in enumerate(chunks):
                    acc[c] = acc[c] + w_hd * jnp.maximum(logit[:, cols], 0.0)
            part = [carry[0][rows], carry[1][rows]]
            q_chunk_r = _chunk_of(i * TQ + r * ROW_TILE + row_t)
            for c, cols in enumerate(chunks):
                adm = _chunk_of(kb * KB + c * LANES + lane_t) <= q_chunk_r
                sc_ref[kb, rows, cols] = jnp.where(adm, _sort_key(acc[c]), KEY_NONE)
                part[c % 2] = jnp.maximum(part[c % 2], jnp.where(adm, acc[c], -jnp.inf))
            grp_max[0].append(part[0])
            grp_max[1].append(part[1])
        return jnp.concatenate(grp_max[0], axis=0), jnp.concatenate(grp_max[1], axis=0)

    neg_inf = jnp.full((TQ, LANES), -jnp.inf, jnp.float32)
    grp_a, grp_b = lax.fori_loop(0, n_blocks, score_block, (neg_inf, neg_inf))

    kf = np.float32(top_k)
    row_tiles = [slice(r * ROW_TILE, (r + 1) * ROW_TILE) for r in range(TQ // ROW_TILE)]
    ones_row = jnp.ones((8, LANES), jnp.bfloat16)

    def expand(dense):
        return jnp.concatenate([jnp.broadcast_to(dense[0:1, rows], (ROW_TILE, LANES)).T for rows in row_tiles],
                               axis=0)

    def compress(rep):
        return jnp.concatenate([rep[rows].T[0:8] for rows in row_tiles], axis=1)

    def count_ge(mid):
        mid_rep = expand(mid)
        parts = []
        for rows in row_tiles:
            mid_r = mid_rep[rows]

            def body(kb, cnt, rows=rows, mid_r=mid_r):
                for cols in chunks:
                    cnt = cnt + jnp.where(sc_ref[kb, rows, cols] >= mid_r, 1.0, 0.0)
                return cnt
            parts.append(_bf16(lax.fori_loop(0, n_blocks, body, jnp.zeros((ROW_TILE, LANES), jnp.float32))))
        return _dot_nt(ones_row, jnp.concatenate(parts, axis=0))

    def pivot(lo, hi):
        mean = (lo & hi) + lax.shift_right_arithmetic(lo ^ hi, jnp.int32(1))
        return jnp.where((lo <= 0) & (hi > 1), 1, jnp.where((lo < 0) & (hi == 1), 0, mean))

    q_pos = i * TQ + lax.broadcasted_iota(jnp.int32, (8, TQ), 1)
    select_all = (_chunk_of(q_pos) + 1) * CHUNK <= top_k
    lo0 = jnp.where(select_all, KEY_NONE + 1, compress(_sort_key(lane_all(jnp.min, jnp.minimum(grp_a, grp_b)))))
    hi0 = jnp.where(select_all, KEY_NONE + 1,
                    compress(_sort_key(lane_all(jnp.max, jnp.maximum(grp_a, grp_b)))) + 1)

    def bisect_cond(state):
        it, lo, _, mid = state
        return jnp.logical_and(it < MAX_BISECT, jnp.max(jnp.where(mid != lo, 1.0, 0.0)) > 0.0)

    def bisect_body(state):
        it, lo, hi, mid = state
        cnt = count_ge(mid)
        ge = cnt >= kf
        lo = jnp.where(ge, mid, lo)
        hi = jnp.where(cnt == kf, mid, jnp.where(ge, hi, mid))
        return it + 1, lo, hi, pivot(lo, hi)

    _, thr_d, hi_d, _ = lax.while_loop(bisect_cond, bisect_body, (jnp.int32(0), lo0, hi0, pivot(lo0, hi0)))
    thr = expand(thr_d)
    tie_d = hi_d != thr_d

    @pl.when(jnp.max(jnp.where(tie_d, 1.0, 0.0)) > 0.0)
    def _():
        tie = expand(jnp.where(tie_d, 1, 0)) > 0
        need = expand(kf - count_ge(thr_d + 1))
        upto = (lax.broadcasted_iota(jnp.int32, (KB, KB), 0)
                <= lax.broadcasted_iota(jnp.int32, (KB, KB), 1)).astype(jnp.bfloat16)

        def drop_body(kb, seen):
            keys = [sc_ref[kb, :, cols] for cols in chunks]
            band = [tie & (key == thr) for key in keys]
            band_f = [jnp.where(b, 1.0, 0.0) for b in band]
            rank = _dot(jnp.concatenate([_bf16(b) for b in band_f], axis=1), upto)
            for c, cols in enumerate(chunks):
                late = band[c] & (rank[:, cols] + seen > need)
                sc_ref[kb, :, cols] = jnp.where(late, KEY_NONE, keys[c])
            return seen + lane_all(jnp.sum, functools.reduce(jnp.add, band_f))
        lax.fori_loop(0, n_blocks, drop_body, jnp.zeros((TQ, LANES), jnp.float32))

    @pl.when(i == 0)
    def _():
        def norm_block(kb, run):
            kc = kcat_ref[0, pl.ds(pl.multiple_of(kb * KB, KB), KB), :].astype(jnp.float32)
            return jnp.maximum(run, jnp.sum(kc * kc, axis=1, keepdims=True))
        run = lax.fori_loop(0, kcat_ref.shape[1] // KB, norm_block, jnp.zeros((KB, 1), jnp.float32))
        kmax_ref[...] = jnp.broadcast_to(jnp.max(run, axis=0, keepdims=True), kmax_ref.shape)

    k_norm2 = kmax_ref[0:1, :]
    q_all = qcat_ref[0, 0]
    q_norm2 = _dot(q_all * q_all, jnp.ones((Q_CAT, LANES), jnp.bfloat16))
    qk2 = jnp.maximum(q_norm2 * k_norm2, 1e-30)
    m_ref[...] = qk2 * lax.rsqrt(qk2) * BOUND_SLACK

    def set_bias(kb):
        for cols in chunks:
            bias_ref[:, cols] = jnp.where(sc_ref[kb, :, cols] >= thr, 0.0, NEG_BIG)

    def bias_rows(cols):
        return jnp.concatenate([bias_ref[:, cols]] * HEAD_GROUP, axis=0)

    group_rows = [slice(g * HEAD_GROUP * TQ, (g + 1) * HEAD_GROUP * TQ) for g in range(DSA_HEADS // HEAD_GROUP)]

    def attend():
        acc_ref[...] = jnp.zeros(acc_ref.shape, jnp.float32)

        def attn_block(kb, carry):
            rows = pl.ds(pl.multiple_of(kb * KB, KB), KB)
            kc = kcat_ref[0, rows, :]
            vc = vcat_ref[0, rows, :]
            set_bias(kb)
            for grp in group_rows:
                s = _dot_nt(qcat_ref[0, 0, grp, :], kc)
                m = m_ref[grp, :]
                p = [_bf16(jnp.exp(s[:, cols] + bias_rows(cols) - m)) for cols in chunks]
                acc_ref[grp, :] += _dot(jnp.concatenate(p, axis=1), vc)
            return carry
        lax.fori_loop(0, n_blocks, attn_block, 0)
        return jnp.min(acc_ref[:, DSA_KV_RANK:DSA_KV_RANK + 1])

    @pl.when(attend() < ROW_SUM_MIN)
    def _():
        m_ref[...] = jnp.full(m_ref.shape, NEG_BIG, jnp.float32)

        def max_block(kb, carry):
            kc = kcat_ref[0, pl.ds(pl.multiple_of(kb * KB, KB), KB), :]
            set_bias(kb)
            for grp in group_rows:
                s = _dot_nt(qcat_ref[0, 0, grp, :], kc)
                sm = [s[:, cols] + bias_rows(cols) for cols in chunks]
                m_ref[grp, :] = jnp.maximum(m_ref[grp, :], functools.reduce(jnp.maximum, sm))
            return carry
        lax.fori_loop(0, n_blocks, max_block, 0)
        m_ref[...] = lane_all(jnp.max, m_ref[...])
        attend()

    for hd in range(DSA_HEADS):
        a = acc_ref[hd * TQ:(hd + 1) * TQ, :]
        denom = jnp.broadcast_to(a[:, DSA_KV_RANK:DSA_KV_RANK + 1], (TQ, DSA_KV_RANK))
        o_ref[0, :, hd * DSA_KV_RANK:(hd + 1) * DSA_KV_RANK] = _bf16(a[:, :DSA_KV_RANK] / denom)


def _out_kernel(x_ref, out_a_ref, o_lat_ref, gate_b_ref, w_uv_ref, w_out_ref, post_g_ref, y_ref):
    o = _dot(o_lat_ref[0], w_uv_ref[...])
    out_b = _bf16(o * gate_b_ref[0])
    y = _dot(out_a_ref[0], w_out_ref[:GMLP_WIDTH, :]) + _dot(out_b, w_out_ref[GMLP_WIDTH:, :])
    y = y * lax.rsqrt(jnp.mean(y * y, axis=-1, keepdims=True) + EPS) * post_g_ref[...]
    y_ref[0] = x_ref[0] + y


def _full(shape):
    return pl.BlockSpec(shape, lambda b, t: (0,) * len(shape))


def _layer(x, w_in, pre_g, post_g, ln_g, ln_b, w_s, b_s, qn_g, kvn_g, w_uq, w_uk, w_uv, w_q_idx, w_out):
    B, S, D = x.shape
    assert D == D_MODEL and S % KB == 0 and S % TM == 0 and TM % TQ == 0 and KB % TQ == 0
    top_k = min(TOPK_MAX, S // 4)
    assert top_k <= 2 * LANES
    f32, bf16 = jnp.float32, jnp.bfloat16

    kr = w_in[:, 1920:1952]
    kr_sw = jnp.concatenate([kr[:, ROPE_HALF:], kr[:, :ROPE_HALF]], axis=1)
    w_all = jnp.concatenate([
        w_in[:, :1920], jnp.tile(kr, (1, ROPE_LANE_GROUPS)), jnp.tile(kr_sw, (1, ROPE_LANE_GROUPS)),
        w_in[:, 1952:2024], jnp.zeros((D, LANES - IDX_DIM - IDX_HEADS), f32), w_in[:, 2024:]], axis=1).astype(bf16)
    wq3 = w_uq.reshape(DSA_Q_RANK, DSA_HEADS, DSA_NOPE_DIM + DSA_ROPE_DIM)
    nope = jnp.pad(wq3[:, :, :DSA_NOPE_DIM], ((0, 0), (0, 0), (0, LANES - DSA_NOPE_DIM)))
    rope = wq3[:, :, DSA_NOPE_DIM:]
    rope_sw = jnp.concatenate([rope[:, :, ROPE_HALF:], rope[:, :, :ROPE_HALF]], axis=2)
    wqi = jnp.pad(w_q_idx.reshape(DSA_Q_RANK, IDX_HEADS, IDX_DIM), ((0, 0), (0, 0), (0, LANES - IDX_DIM)))
    w_q = jnp.concatenate([nope.reshape(DSA_Q_RANK, -1), rope.reshape(DSA_Q_RANK, -1),
                           rope_sw.reshape(DSA_Q_RANK, -1), wqi.reshape(DSA_Q_RANK, -1)], axis=1).astype(bf16)
    w_uk_t = jnp.pad(jnp.transpose(w_uk, (1, 2, 0)), ((0, 0), (0, LANES - DSA_NOPE_DIM), (0, 0))).astype(bf16)
    eye = jnp.eye(DSA_HEADS, dtype=f32)
    w_uv_bd = (jnp.transpose(w_uv, (1, 0, 2))[:, :, None, :] * eye[:, None, :, None]).reshape(
        DSA_HEADS * DSA_KV_RANK, DSA_WIDTH).astype(bf16)
    b_s_b = jnp.broadcast_to(b_s[:, :, None], (GMLP_GROUPS, GMLP_BLOCK, GMLP_GROUP_DIM))

    pos = jnp.arange(S, dtype=f32)
    inv_freq = ROPE_THETA ** (-jnp.arange(0, DSA_ROPE_DIM, 2, dtype=f32) / DSA_ROPE_DIM)
    ang = pos[:, None] * inv_freq[None, :]
    cos_t = jnp.tile(jnp.concatenate([jnp.cos(ang), jnp.cos(ang)], axis=1), (1, DSA_HEADS))
    sin_t = jnp.tile(jnp.concatenate([-jnp.sin(ang), jnp.sin(ang)], axis=1), (1, DSA_HEADS))

    row2 = lambda a: a.reshape(1, -1)
    tok = lambda width: pl.BlockSpec((1, TM, width), lambda b, t: (b, t, 0))
    hm = lambda width: pl.BlockSpec((1, DSA_HEADS, TM, width), lambda b, t: (b, 0, t, 0))

    out_a, gate_b, qidx, widx, qcat, kidx, kcat, vcat = pl.pallas_call(
        _proj_kernel,
        grid=(B, S // TM),
        in_specs=[tok(D), _full((1, D)), _full((D, _C_END)), _full((1, GMLP_WIDTH)), _full((1, GMLP_WIDTH)),
                  _full((GMLP_GROUPS, GMLP_BLOCK, GMLP_BLOCK)), _full((GMLP_GROUPS, GMLP_BLOCK, GMLP_GROUP_DIM)),
                  _full((1, DSA_Q_RANK)), _full((1, DSA_KV_RANK)), _full((DSA_Q_RANK, _Q_END)),
                  _full((DSA_HEADS, LANES, DSA_KV_RANK)),
                  pl.BlockSpec((TM, 2 * LANES), lambda b, t: (t, 0)),
                  pl.BlockSpec((TM, 2 * LANES), lambda b, t: (t, 0))],
        out_specs=[tok(GMLP_WIDTH), tok(DSA_WIDTH), hm(LANES), tok(LANES),
                   pl.BlockSpec((1, TM // TQ, DSA_HEADS * TQ, Q_CAT), lambda b, t: (b, t, 0, 0)),
                   tok(LANES), tok(Q_CAT), tok(Q_CAT)],
        out_shape=[jax.ShapeDtypeStruct((B, S, GMLP_WIDTH), bf16),
                   jax.ShapeDtypeStruct((B, S, DSA_WIDTH), f32),
                   jax.ShapeDtypeStruct((B, DSA_HEADS, S, LANES), bf16),
                   jax.ShapeDtypeStruct((B, S, LANES), f32),
                   jax.ShapeDtypeStruct((B, S // TQ, DSA_HEADS * TQ, Q_CAT), bf16),
                   jax.ShapeDtypeStruct((B, S, LANES), bf16),
                   jax.ShapeDtypeStruct((B, S, Q_CAT), bf16),
                   jax.ShapeDtypeStruct((B, S, Q_CAT), bf16)],
        compiler_params=pltpu.CompilerParams(
            dimension_semantics=("arbitrary", "arbitrary"), vmem_limit_bytes=48 * 1024 * 1024),
        name="proj_gmlp_dsa_prep",
    )(x, row2(pre_g), w_all, row2(ln_g), row2(ln_b), w_s, b_s_b, row2(qn_g), row2(kvn_g), w_q, w_uk_t,
      cos_t, sin_t)

    qt = lambda width: pl.BlockSpec((1, DSA_HEADS, TQ, width), lambda b, t: (b, 0, t, 0))
    seq = lambda width: pl.BlockSpec((1, S, width), lambda b, t: (b, 0, 0), pipeline_mode=pl.Buffered(1))
    o_lat = pl.pallas_call(
        functools.partial(_dsa_kernel, top_k=top_k),
        grid=(B, S // TQ),
        in_specs=[qt(LANES), pl.BlockSpec((1, TQ, LANES), lambda b, t: (b, t, 0)),
                  pl.BlockSpec((1, 1, DSA_HEADS * TQ, Q_CAT), lambda b, t: (b, t, 0, 0)),
                  seq(LANES), seq(Q_CAT), seq(Q_CAT)],
        out_specs=pl.BlockSpec((1, TQ, DSA_HEADS * DSA_KV_RANK), lambda b, t: (b, t, 0)),
        out_shape=jax.ShapeDtypeStruct((B, S, DSA_HEADS * DSA_KV_RANK), bf16),
        scratch_shapes=[pltpu.VMEM((S // KB, TQ, KB), jnp.int32),
                        pltpu.VMEM((IDX_HEADS, TQ, LANES), f32),
                        pltpu.VMEM((TQ, KB), f32),
                        pltpu.VMEM((DSA_HEADS * TQ, LANES), f32),
                        pltpu.VMEM((DSA_HEADS * TQ, Q_CAT), f32),
                        pltpu.VMEM((8, LANES), f32)],
        compiler_params=pltpu.CompilerParams(
            dimension_semantics=("arbitrary", "arbitrary"), vmem_limit_bytes=56 * 1024 * 1024),
        name="dsa_index_select_attend",
    )(qidx, widx, qcat, kidx, kcat, vcat)

    return pl.pallas_call(
        _out_kernel,
        grid=(B, S // TM),
        in_specs=[tok(D), tok(GMLP_WIDTH), tok(DSA_HEADS * DSA_KV_RANK), tok(DSA_WIDTH),
                  _full((DSA_HEADS * DSA_KV_RANK, DSA_WIDTH)), _full((GMLP_WIDTH + DSA_WIDTH, D)), _full((1, D))],
        out_specs=tok(D),
        out_shape=jax.ShapeDtypeStruct((B, S, D), f32),
        compiler_params=pltpu.CompilerParams(
            dimension_semantics=("arbitrary", "arbitrary"), vmem_limit_bytes=32 * 1024 * 1024),
        name="out_proj_norm_residual",
    )(x, out_a, o_lat, gate_b, w_uv_bd, w_out.astype(bf16), row2(post_g))


def kernel(x, w_in, pre_norm_g, post_norm_g, gmlp_ln_g, gmlp_ln_b, gmlp_w_s, gmlp_b_s, dsa_q_norm_g, dsa_kv_norm_g, dsa_w_uq, dsa_w_uk, dsa_w_uv, dsa_w_q_idx, w_out):
    for l in range(w_in.shape[0]):
        x = _layer(x, w_in[l], pre_norm_g[l], post_norm_g[l], gmlp_ln_g[l], gmlp_ln_b[l], gmlp_w_s[l],
                   gmlp_b_s[l], dsa_q_norm_g[l], dsa_kv_norm_g[l], dsa_w_uq[l], dsa_w_uk[l], dsa_w_uv[l],
                   dsa_w_q_idx[l], w_out[l])
    return x
```

```python
import functools
import math

import jax
import jax.numpy as jnp
import numpy as np
from jax import lax
from jax.experimental import pallas as pl
from jax.experimental.pallas import tpu as pltpu

D_MODEL = 1024
CHUNK = 64
EPS = 1e-6
GMLP_GROUPS = 4
GMLP_GROUP_DIM = 128
GMLP_WIDTH = GMLP_GROUPS * GMLP_GROUP_DIM
GMLP_BLOCK = 128
DSA_HEADS = 8
DSA_V_DIM = 64
DSA_WIDTH = DSA_HEADS * DSA_V_DIM
DSA_NOPE_DIM = 64
DSA_ROPE_DIM = 32
DSA_Q_RANK = 256
DSA_KV_RANK = 128
IDX_HEADS = 8
IDX_DIM = 64
TOPK_MAX = 256
ROPE_THETA = 10000.0

LANES = 128
ROPE_HALF = DSA_ROPE_DIM // 2
ROPE_LANE_GROUPS = LANES // DSA_ROPE_DIM
Q_CAT = DSA_KV_RANK + LANES
W_IDX_LANE = IDX_DIM

TM = 512
TQ = 512
ROW_TILE = 128
KB = 512
HEAD_GROUP = 1
NEG_BIG = -1e30
ROW_SUM_MIN = 1e-26
BOUND_SLACK = 1.02
KEY_NONE = -2 ** 31
FIRST_PIVOTS = (1.0, 0.0)
MAX_BISECT = 40

_C_UV, _C_ZA, _C_CQ, _C_CKV = 0, 1024, 1536, 1792
_C_KR, _C_KRS, _C_SLAB, _C_ZB, _C_END = 1920, 2048, 2176, 2304, 2816
_Q_NOPE, _Q_ROPE, _Q_ROPES, _Q_IDX, _Q_END = 0, 1024, 1280, 1536, 2560

_NT = (((1,), (1,)), ((), ()))


def _bf16(a):
    return a.astype(jnp.bfloat16)


def _dot(a, b):
    return jnp.dot(a, b, preferred_element_type=jnp.float32)


def _dot_nt(a, b):
    return lax.dot_general(a, b, _NT, preferred_element_type=jnp.float32)


def _chunk_of(pos):
    return lax.shift_right_logical(pos, jnp.int32(CHUNK.bit_length() - 1))


def _sort_key(s):
    bits = pltpu.bitcast(s, jnp.int32)
    return jnp.where(bits < 0, bits ^ jnp.int32(0x7FFFFFFF), bits)


def _silu(z):
    return z / (1.0 + jnp.exp(-z))


def _gelu_exact(a):
    return 0.5 * a * (1.0 + lax.erf(a * np.float32(math.sqrt(0.5))))


def _proj_kernel(x_ref, pre_g_ref, w_all_ref, ln_g_ref, ln_b_ref, w_s_ref, b_s_ref,
                 qn_g_ref, kvn_g_ref, w_q_ref, w_uk_ref, cos_ref, sin_ref,
                 out_a_ref, gate_b_ref, qidx_ref, widx_ref, qcat_ref, kidx_ref, kcat_ref, vcat_ref):
    x = x_ref[0]
    h = x * lax.rsqrt(jnp.mean(x * x, axis=-1, keepdims=True) + EPS) * pre_g_ref[...]
    proj = _dot(_bf16(h), w_all_ref[...])

    uv = _gelu_exact(proj[:, _C_UV:_C_ZA])
    u, v = uv[:, :GMLP_WIDTH], uv[:, GMLP_WIDTH:]
    mu = jnp.mean(v, axis=-1, keepdims=True)
    vc = v - mu
    var = jnp.mean(vc * vc, axis=-1, keepdims=True)
    vn = _bf16(vc * lax.rsqrt(var + EPS) * ln_g_ref[...] + ln_b_ref[...])
    gate_a = _silu(proj[:, _C_ZA:_C_CQ])
    t_chunk = _chunk_of(lax.broadcasted_iota(jnp.int32, (GMLP_BLOCK, GMLP_BLOCK), 0))
    s_chunk = _chunk_of(lax.broadcasted_iota(jnp.int32, (GMLP_BLOCK, GMLP_BLOCK), 1))
    for g in range(GMLP_GROUPS):
        w_g = _bf16(jnp.where(s_chunk <= t_chunk, w_s_ref[g], 0.0))
        cols = slice(g * GMLP_GROUP_DIM, (g + 1) * GMLP_GROUP_DIM)
        for r in range(TM // GMLP_BLOCK):
            rows = slice(r * GMLP_BLOCK, (r + 1) * GMLP_BLOCK)
            y = _dot(w_g, vn[rows, cols]) + b_s_ref[g]
            out_a_ref[0, rows, cols] = _bf16(u[rows, cols] * y * gate_a[rows, cols])

    gate_b_ref[0] = _silu(proj[:, _C_ZB:_C_END])

    c_q = proj[:, _C_CQ:_C_CKV]
    c_q = c_q * lax.rsqrt(jnp.mean(c_q * c_q, axis=-1, keepdims=True) + EPS) * qn_g_ref[...]
    c_kv = proj[:, _C_CKV:_C_KR]
    c_kv = _bf16(c_kv * lax.rsqrt(jnp.mean(c_kv * c_kv, axis=-1, keepdims=True) + EPS) * kvn_g_ref[...])
    cos = cos_ref[...]
    sin = sin_ref[...]
    k_rope = proj[:, _C_KR:_C_KRS] * cos[:, :LANES] + proj[:, _C_KRS:_C_SLAB] * sin[:, :LANES]
    slab = proj[:, _C_SLAB:_C_ZB]
    kidx_ref[0] = _bf16(slab)
    widx_ref[0] = slab
    kcat_ref[0, :, :DSA_KV_RANK] = c_kv
    kcat_ref[0, :, DSA_KV_RANK:] = _bf16(k_rope)
    lane = lax.broadcasted_iota(jnp.int32, (TM, LANES), 1)
    vcat_ref[0, :, :DSA_KV_RANK] = c_kv
    vcat_ref[0, :, DSA_KV_RANK:] = jnp.where(lane == 0, 1.0, 0.0).astype(jnp.bfloat16)

    q_all = _dot(_bf16(c_q), w_q_ref[...])
    q_rope = q_all[:, _Q_ROPE:_Q_ROPES] * cos + q_all[:, _Q_ROPES:_Q_IDX] * sin
    scale = np.float32(1.0 / math.sqrt(DSA_NOPE_DIM + DSA_ROPE_DIM))
    for hd in range(DSA_HEADS):
        cols = slice(_Q_NOPE + hd * LANES, _Q_NOPE + (hd + 1) * LANES)
        q_lat = _dot(_bf16(q_all[:, cols]), w_uk_ref[hd])
        grp, sub = divmod(hd, ROPE_LANE_GROUPS)
        own = (lane >= sub * DSA_ROPE_DIM) & (lane < (sub + 1) * DSA_ROPE_DIM)
        q_r = jnp.where(own, q_rope[:, grp * LANES:(grp + 1) * LANES] * scale, 0.0)
        for t in range(TM // TQ):
            src = slice(t * TQ, (t + 1) * TQ)
            dst = slice(hd * TQ, (hd + 1) * TQ)
            qcat_ref[0, t, dst, :DSA_KV_RANK] = _bf16(q_lat[src] * scale)
            qcat_ref[0, t, dst, DSA_KV_RANK:] = _bf16(q_r[src])
        icol = slice(_Q_IDX + hd * LANES, _Q_IDX + (hd + 1) * LANES)
        qidx_ref[0, hd] = _bf16(q_all[:, icol])


def _dsa_kernel(qidx_ref, widx_ref, qcat_ref, kidx_ref, kcat_ref, vcat_ref, o_ref,
                sc_ref, wrep_ref, bias_ref, m_ref, acc_ref, kmax_ref, *, top_k):
    i = pl.program_id(1)
    n_blocks = (i * TQ) // KB + 1
    chunks = [slice(c * LANES, (c + 1) * LANES) for c in range(KB // LANES)]
    idx_scale = np.float32(IDX_HEADS ** -0.5 * IDX_DIM ** -0.5)

    row = lax.broadcasted_iota(jnp.int32, (TQ, LANES), 0)
    row_t = lax.broadcasted_iota(jnp.int32, (ROW_TILE, LANES), 0)
    lane_t = lax.broadcasted_iota(jnp.int32, (ROW_TILE, LANES), 1)
    n_adm = (_chunk_of(i * TQ + row) + 1) * CHUNK

    def lane_all(op, a):
        return jnp.broadcast_to(op(a, axis=1, keepdims=True), a.shape)

    w = widx_ref[0] * idx_scale
    for hd in range(IDX_HEADS):
        wrep_ref[hd] = jnp.broadcast_to(w[:, W_IDX_LANE + hd:W_IDX_LANE + hd + 1], (TQ, LANES))

    def score_block(kb, carry):
        k = kidx_ref[0, pl.ds(pl.multiple_of(kb * KB, KB), KB), :]
        grp_max = [[], []]
        for r in range(TQ // ROW_TILE):
            rows = slice(r * ROW_TILE, (r + 1) * ROW_TILE)
            acc = [jnp.zeros((ROW_TILE, LANES), jnp.float32) for _ in chunks]
            for hd in range(IDX_HEADS):
                logit = _dot_nt(qidx_ref[0, hd, rows, :], k)
                w_hd = wrep_ref[hd, rows, :]
                for c, cols in enumerate(chunks):
                    acc[c] = acc[c] + w_hd * jnp.maximum(logit[:, cols], 0.0)
            part = [carry[0][rows], carry[1][rows]]
            q_chunk_r = _chunk_of(i * TQ + r * ROW_TILE + row_t)
            for c, cols in enumerate(chunks):
                adm = _chunk_of(kb * KB + c * LANES + lane_t) <= q_chunk_r
                sc_ref[kb, rows, cols] = jnp.where(adm, _sort_key(acc[c]), KEY_NONE)
                part[c % 2] = jnp.maximum(part[c % 2], jnp.where(adm, acc[c], -jnp.inf))
            grp_max[0].append(part[0])
            grp_max[1].append(part[1])
        return jnp.concatenate(grp_max[0], axis=0), jnp.concatenate(grp_max[1], axis=0)

    neg_inf = jnp.full((TQ, LANES), -jnp.inf, jnp.float32)
    grp_a, grp_b = lax.fori_loop(0, n_blocks, score_block, (neg_inf, neg_inf))

    kf = np.float32(top_k)

    def count_ge(mid):
        parts = []
        for r in range(TQ // ROW_TILE):
            rows = slice(r * ROW_TILE, (r + 1) * ROW_TILE)
            mid_r = mid[rows]

            def body(kb, cnt, rows=rows, mid_r=mid_r):
                for cols in chunks:
                    cnt = cnt + jnp.where(sc_ref[kb, rows, cols] >= mid_r, 1.0, 0.0)
                return cnt
            parts.append(lax.fori_loop(0, n_blocks, body, jnp.zeros((ROW_TILE, LANES), jnp.float32)))
        return lane_all(jnp.sum, jnp.concatenate(parts, axis=0))

    def pivot(lo, hi):
        mean = (lo & hi) + lax.shift_right_arithmetic(lo ^ hi, jnp.int32(1))
        return jnp.where((lo <= 0) & (hi > 1), 1, jnp.where((lo < 0) & (hi == 1), 0, mean))

    select_all = n_adm <= top_k
    grp_min = lane_all(jnp.min, jnp.minimum(grp_a, grp_b))
    lo0 = jnp.where(select_all, KEY_NONE + 1, _sort_key(grp_min))
    hi0 = jnp.where(select_all, KEY_NONE + 1, _sort_key(lane_all(jnp.max, jnp.maximum(grp_a, grp_b))) + 1)

    def step(lo, hi, mid):
        cnt = count_ge(mid)
        ge = cnt >= kf
        return jnp.where(ge, mid, lo), jnp.where(cnt == kf, mid, jnp.where(ge, hi, mid))

    n_grp = np.float32(2 * LANES)
    grp_mean = lane_all(jnp.sum, grp_a + grp_b) / n_grp
    grp_var = lane_all(jnp.sum, grp_a * grp_a + grp_b * grp_b) / n_grp - grp_mean * grp_mean
    grp_std = jnp.sqrt(jnp.maximum(grp_var, 0.0))
    lo, hi = lo0, hi0
    for below in FIRST_PIVOTS:
        guess = _sort_key(grp_mean - np.float32(below) * grp_std)
        usable = (grp_min > -jnp.inf) & (guess > lo) & (guess < hi)
        lo, hi = step(lo, hi, jnp.where(usable, guess, pivot(lo, hi)))

    def bisect_cond(state):
        it, lo, _, mid = state
        return jnp.logical_and(it < MAX_BISECT, jnp.max(jnp.where(mid != lo, 1.0, 0.0)) > 0.0)

    def bisect_body(state):
        it, lo, hi, mid = state
        lo, hi = step(lo, hi, mid)
        return it + 1, lo, hi, pivot(lo, hi)

    _, thr, hi, _ = lax.while_loop(bisect_cond, bisect_body, (jnp.int32(0), lo, hi, pivot(lo, hi)))
    tie = hi != thr

    @pl.when(jnp.max(jnp.where(tie, 1.0, 0.0)) > 0.0)
    def _():
        need = kf - count_ge(thr + 1)
        upto = (lax.broadcasted_iota(jnp.int32, (KB, KB), 0)
                <= lax.broadcasted_iota(jnp.int32, (KB, KB), 1)).astype(jnp.bfloat16)

        def drop_body(kb, seen):
            keys = [sc_ref[kb, :, cols] for cols in chunks]
            band = [tie & (key == thr) for key in keys]
            band_f = [jnp.where(b, 1.0, 0.0) for b in band]
            rank = _dot(jnp.concatenate([_bf16(b) for b in band_f], axis=1), upto)
            for c, cols in enumerate(chunks):
                late = band[c] & (rank[:, cols] + seen > need)
                sc_ref[kb, :, cols] = jnp.where(late, KEY_NONE, keys[c])
            return seen + lane_all(jnp.sum, functools.reduce(jnp.add, band_f))
        lax.fori_loop(0, n_blocks, drop_body, jnp.zeros((TQ, LANES), jnp.float32))

    @pl.when(i == 0)
    def _():
        def norm_block(kb, run):
            kc = kcat_ref[0, pl.ds(pl.multiple_of(kb * KB, KB), KB), :].astype(jnp.float32)
            return jnp.maximum(run, jnp.sum(kc * kc, axis=1, keepdims=True))
        run = lax.fori_loop(0, kcat_ref.shape[1] // KB, norm_block, jnp.zeros((KB, 1), jnp.float32))
        kmax_ref[...] = jnp.broadcast_to(jnp.max(run, axis=0, keepdims=True), kmax_ref.shape)

    k_norm2 = kmax_ref[0:1, :]
    q_all = qcat_ref[0, 0]
    q_norm2 = _dot(q_all * q_all, jnp.ones((Q_CAT, LANES), jnp.bfloat16))
    qk2 = jnp.maximum(q_norm2 * k_norm2, 1e-30)
    m_ref[...] = qk2 * lax.rsqrt(qk2) * BOUND_SLACK

    def set_bias(kb):
        for cols in chunks:
            bias_ref[:, cols] = jnp.where(sc_ref[kb, :, cols] >= thr, 0.0, NEG_BIG)

    def bias_rows(cols):
        return jnp.concatenate([bias_ref[:, cols]] * HEAD_GROUP, axis=0)

    group_rows = [slice(g * HEAD_GROUP * TQ, (g + 1) * HEAD_GROUP * TQ) for g in range(DSA_HEADS // HEAD_GROUP)]

    def attend():
        acc_ref[...] = jnp.zeros(acc_ref.shape, jnp.float32)

        def attn_block(kb, carry):
            rows = pl.ds(pl.multiple_of(kb * KB, KB), KB)
            kc = kcat_ref[0, rows, :]
            vc = vcat_ref[0, rows, :]
            set_bias(kb)
            for grp in group_rows:
                s = _dot_nt(qcat_ref[0, 0, grp, :], kc)
                m = m_ref[grp, :]
                p = [_bf16(jnp.exp(s[:, cols] + bias_rows(cols) - m)) for cols in chunks]
                acc_ref[grp, :] += _dot(jnp.concatenate(p, axis=1), vc)
            return carry
        lax.fori_loop(0, n_blocks, attn_block, 0)
        return jnp.min(acc_ref[:, DSA_KV_RANK:DSA_KV_RANK + 1])

    @pl.when(attend() < ROW_SUM_MIN)
    def _():
        m_ref[...] = jnp.full(m_ref.shape, NEG_BIG, jnp.float32)

        def max_block(kb, carry):
            kc = kcat_ref[0, pl.ds(pl.multiple_of(kb * KB, KB), KB), :]
            set_bias(kb)
            for grp in group_rows:
                s = _dot_nt(qcat_ref[0, 0, grp, :], kc)
                sm = [s[:, cols] + bias_rows(cols) for cols in chunks]
                m_ref[grp, :] = jnp.maximum(m_ref[grp, :], functools.reduce(jnp.maximum, sm))
            return carry
        lax.fori_loop(0, n_blocks, max_block, 0)
        m_ref[...] = lane_all(jnp.max, m_ref[...])
        attend()

    for hd in range(DSA_HEADS):
        a = acc_ref[hd * TQ:(hd + 1) * TQ, :]
        denom = jnp.broadcast_to(a[:, DSA_KV_RANK:DSA_KV_RANK + 1], (TQ, DSA_KV_RANK))
        o_ref[0, :, hd * DSA_KV_RANK:(hd + 1) * DSA_KV_RANK] = _bf16(a[:, :DSA_KV_RANK] / denom)


def _out_kernel(x_ref, out_a_ref, o_lat_ref, gate_b_ref, w_uv_ref, w_out_ref, post_g_ref, y_ref):
    o = _dot(o_lat_ref[0], w_uv_ref[...])
    out_b = _bf16(o * gate_b_ref[0])
    y = _dot(out_a_ref[0], w_out_ref[:GMLP_WIDTH, :]) + _dot(out_b, w_out_ref[GMLP_WIDTH:, :])
    y = y * lax.rsqrt(jnp.mean(y * y, axis=-1, keepdims=True) + EPS) * post_g_ref[...]
    y_ref[0] = x_ref[0] + y


def _full(shape):
    return pl.BlockSpec(shape, lambda b, t: (0,) * len(shape))


def _layer(x, w_in, pre_g, post_g, ln_g, ln_b, w_s, b_s, qn_g, kvn_g, w_uq, w_uk, w_uv, w_q_idx, w_out):
    B, S, D = x.shape
    assert D == D_MODEL and S % KB == 0 and S % TM == 0 and TM % TQ == 0 and KB % TQ == 0
    top_k = min(TOPK_MAX, S // 4)
    assert top_k <= 2 * LANES
    f32, bf16 = jnp.float32, jnp.bfloat16

    kr = w_in[:, 1920:1952]
    kr_sw = jnp.concatenate([kr[:, ROPE_HALF:], kr[:, :ROPE_HALF]], axis=1)
    w_all = jnp.concatenate([
        w_in[:, :1920], jnp.tile(kr, (1, ROPE_LANE_GROUPS)), jnp.tile(kr_sw, (1, ROPE_LANE_GROUPS)),
        w_in[:, 1952:2024], jnp.zeros((D, LANES - IDX_DIM - IDX_HEADS), f32), w_in[:, 2024:]], axis=1).astype(bf16)
    wq3 = w_uq.reshape(DSA_Q_RANK, DSA_HEADS, DSA_NOPE_DIM + DSA_ROPE_DIM)
    nope = jnp.pad(wq3[:, :, :DSA_NOPE_DIM], ((0, 0), (0, 0), (0, LANES - DSA_NOPE_DIM)))
    rope = wq3[:, :, DSA_NOPE_DIM:]
    rope_sw = jnp.concatenate([rope[:, :, ROPE_HALF:], rope[:, :, :ROPE_HALF]], axis=2)
    wqi = jnp.pad(w_q_idx.reshape(DSA_Q_RANK, IDX_HEADS, IDX_DIM), ((0, 0), (0, 0), (0, LANES - IDX_DIM)))
    w_q = jnp.concatenate([nope.reshape(DSA_Q_RANK, -1), rope.reshape(DSA_Q_RANK, -1),
                           rope_sw.reshape(DSA_Q_RANK, -1), wqi.reshape(DSA_Q_RANK, -1)], axis=1).astype(bf16)
    w_uk_t = jnp.pad(jnp.transpose(w_uk, (1, 2, 0)), ((0, 0), (0, LANES - DSA_NOPE_DIM), (0, 0))).astype(bf16)
    eye = jnp.eye(DSA_HEADS, dtype=f32)
    w_uv_bd = (jnp.transpose(w_uv, (1, 0, 2))[:, :, None, :] * eye[:, None, :, None]).reshape(
        DSA_HEADS * DSA_KV_RANK, DSA_WIDTH).astype(bf16)
    b_s_b = jnp.broadcast_to(b_s[:, :, None], (GMLP_GROUPS, GMLP_BLOCK, GMLP_GROUP_DIM))

    pos = jnp.arange(S, dtype=f32)
    inv_freq = ROPE_THETA ** (-jnp.arange(0, DSA_ROPE_DIM, 2, dtype=f32) / DSA_ROPE_DIM)
    ang = pos[:, None] * inv_freq[None, :]
    cos_t = jnp.tile(jnp.concatenate([jnp.cos(ang), jnp.cos(ang)], axis=1), (1, DSA_HEADS))
    sin_t = jnp.tile(jnp.concatenate([-jnp.sin(ang), jnp.sin(ang)], axis=1), (1, DSA_HEADS))

    row2 = lambda a: a.reshape(1, -1)
    tok = lambda width: pl.BlockSpec((1, TM, width), lambda b, t: (b, t, 0))
    hm = lambda width: pl.BlockSpec((1, DSA_HEADS, TM, width), lambda b, t: (b, 0, t, 0))

    out_a, gate_b, qidx, widx, qcat, kidx, kcat, vcat = pl.pallas_call(
        _proj_kernel,
        grid=(B, S // TM),
        in_specs=[tok(D), _full((1, D)), _full((D, _C_END)), _full((1, GMLP_WIDTH)), _full((1, GMLP_WIDTH)),
                  _full((GMLP_GROUPS, GMLP_BLOCK, GMLP_BLOCK)), _full((GMLP_GROUPS, GMLP_BLOCK, GMLP_GROUP_DIM)),
                  _full((1, DSA_Q_RANK)), _full((1, DSA_KV_RANK)), _full((DSA_Q_RANK, _Q_END)),
                  _full((DSA_HEADS, LANES, DSA_KV_RANK)),
                  pl.BlockSpec((TM, 2 * LANES), lambda b, t: (t, 0)),
                  pl.BlockSpec((TM, 2 * LANES), lambda b, t: (t, 0))],
        out_specs=[tok(GMLP_WIDTH), tok(DSA_WIDTH), hm(LANES), tok(LANES),
                   pl.BlockSpec((1, TM // TQ, DSA_HEADS * TQ, Q_CAT), lambda b, t: (b, t, 0, 0)),
                   tok(LANES), tok(Q_CAT), tok(Q_CAT)],
        out_shape=[jax.ShapeDtypeStruct((B, S, GMLP_WIDTH), bf16),
                   jax.ShapeDtypeStruct((B, S, DSA_WIDTH), f32),
                   jax.ShapeDtypeStruct((B, DSA_HEADS, S, LANES), bf16),
                   jax.ShapeDtypeStruct((B, S, LANES), f32),
                   jax.ShapeDtypeStruct((B, S // TQ, DSA_HEADS * TQ, Q_CAT), bf16),
                   jax.ShapeDtypeStruct((B, S, LANES), bf16),
                   jax.ShapeDtypeStruct((B, S, Q_CAT), bf16),
                   jax.ShapeDtypeStruct((B, S, Q_CAT), bf16)],
        compiler_params=pltpu.CompilerParams(
            dimension_semantics=("arbitrary", "arbitrary"), vmem_limit_bytes=48 * 1024 * 1024),
        name="proj_gmlp_dsa_prep",
    )(x, row2(pre_g), w_all, row2(ln_g), row2(ln_b), w_s, b_s_b, row2(qn_g), row2(kvn_g), w_q, w_uk_t,
      cos_t, sin_t)

    qt = lambda width: pl.BlockSpec((1, DSA_HEADS, TQ, width), lambda b, t: (b, 0, t, 0))
    seq = lambda width: pl.BlockSpec((1, S, width), lambda b, t: (b, 0, 0), pipeline_mode=pl.Buffered(1))
    o_lat = pl.pallas_call(
        functools.partial(_dsa_kernel, top_k=top_k),
        grid=(B, S // TQ),
        in_specs=[qt(LANES), pl.BlockSpec((1, TQ, LANES), lambda b, t: (b, t, 0)),
                  pl.BlockSpec((1, 1, DSA_HEADS * TQ, Q_CAT), lambda b, t: (b, t, 0, 0)),
                  seq(LANES), seq(Q_CAT), seq(Q_CAT)],
        out_specs=pl.BlockSpec((1, TQ, DSA_HEADS * DSA_KV_RANK), lambda b, t: (b, t, 0)),
        out_shape=jax.ShapeDtypeStruct((B, S, DSA_HEADS * DSA_KV_RANK), bf16),
        scratch_shapes=[pltpu.VMEM((S // KB, TQ, KB), jnp.int32),
                        pltpu.VMEM((IDX_HEADS, TQ, LANES), f32),
                        pltpu.VMEM((TQ, KB), f32),
                        pltpu.VMEM((DSA_HEADS * TQ, LANES), f32),
                        pltpu.VMEM((DSA_HEADS * TQ, Q_CAT), f32),
                        pltpu.VMEM((8, LANES), f32)],
        compiler_params=pltpu.CompilerParams(
            dimension_semantics=("arbitrary", "arbitrary"), vmem_limit_bytes=56 * 1024 * 1024),
        name="dsa_index_select_attend",
    )(qidx, widx, qcat, kidx, kcat, vcat)

    return pl.pallas_call(
        _out_kernel,
        grid=(B, S // TM),
        in_specs=[tok(D), tok(GMLP_WIDTH), tok(DSA_HEADS * DSA_KV_RANK), tok(DSA_WIDTH),
                  _full((DSA_HEADS * DSA_KV_RANK, DSA_WIDTH)), _full((GMLP_WIDTH + DSA_WIDTH, D)), _full((1, D))],
        out_specs=tok(D),
        out_shape=jax.ShapeDtypeStruct((B, S, D), f32),
        compiler_params=pltpu.CompilerParams(
            dimension_semantics=("arbitrary", "arbitrary"), vmem_limit_bytes=32 * 1024 * 1024),
        name="out_proj_norm_residual",
    )(x, out_a, o_lat, gate_b, w_uv_bd, w_out.astype(bf16), row2(post_g))


def kernel(x, w_in, pre_norm_g, post_norm_g, gmlp_ln_g, gmlp_ln_b, gmlp_w_s, gmlp_b_s, dsa_q_norm_g, dsa_kv_norm_g, dsa_w_uq, dsa_w_uk, dsa_w_uv, dsa_w_q_idx, w_out):
    for l in range(w_in.shape[0]):
        x = _layer(x, w_in[l], pre_norm_g[l], post_norm_g[l], gmlp_ln_g[l], gmlp_ln_b[l], gmlp_w_s[l],
                   gmlp_b_s[l], dsa_q_norm_g[l], dsa_kv_norm_g[l], dsa_w_uq[l], dsa_w_uk[l], dsa_w_uv[l],
                   dsa_w_q_idx[l], w_out[l])
    return x
```

```python
import functools
import math

import jax
import jax.numpy as jnp
import numpy as np
from jax import lax
from jax.experimental import pallas as pl
from jax.experimental.pallas import tpu as pltpu

D_MODEL = 1024
CHUNK = 64
EPS = 1e-6
GMLP_GROUPS = 4
GMLP_GROUP_DIM = 128
GMLP_WIDTH = GMLP_GROUPS * GMLP_GROUP_DIM
GMLP_BLOCK = 128
DSA_HEADS = 8
DSA_V_DIM = 64
DSA_WIDTH = DSA_HEADS * DSA_V_DIM
DSA_NOPE_DIM = 64
DSA_ROPE_DIM = 32
DSA_Q_RANK = 256
DSA_KV_RANK = 128
IDX_HEADS = 8
IDX_DIM = 64
TOPK_MAX = 256
ROPE_THETA = 10000.0

LANES = 128
ROPE_HALF = DSA_ROPE_DIM // 2
ROPE_LANE_GROUPS = LANES // DSA_ROPE_DIM
Q_CAT = DSA_KV_RANK + LANES
W_IDX_LANE = IDX_DIM

TM = 512
TQ = 512
ROW_TILE = 128
KB = 512
KEY_TILE = 256
ATTN_BLOCKS = 1
HEAD_GROUP = 1
NEG_BIG = -1e30
ROW_SUM_MIN = 1e-26
BOUND_SLACK = 1.02
KEY_NONE = -2 ** 31
FIRST_PIVOTS = (1.0, 0.0)
MAX_BISECT = 40

_C_UV, _C_ZA, _C_CQ, _C_CKV = 0, 1024, 1536, 1792
_C_KR, _C_KRS, _C_SLAB, _C_ZB, _C_END = 1920, 2048, 2176, 2304, 2816
_Q_NOPE, _Q_ROPE, _Q_ROPES, _Q_IDX, _Q_END = 0, 1024, 1280, 1536, 2560

_NT = (((1,), (1,)), ((), ()))


def _bf16(a):
    return a.astype(jnp.bfloat16)


def _dot(a, b):
    return jnp.dot(a, b, preferred_element_type=jnp.float32)


def _dot_nt(a, b):
    return lax.dot_general(a, b, _NT, preferred_element_type=jnp.float32)


def _chunk_of(pos):
    return lax.shift_right_logical(pos, jnp.int32(CHUNK.bit_length() - 1))


def _sort_key(s):
    bits = pltpu.bitcast(s, jnp.int32)
    return jnp.where(bits < 0, bits ^ jnp.int32(0x7FFFFFFF), bits)


def _silu(z):
    return z / (1.0 + jnp.exp(-z))


def _gelu_exact(a):
    return 0.5 * a * (1.0 + lax.erf(a * np.float32(math.sqrt(0.5))))


def _proj_kernel(x_ref, pre_g_ref, w_all_ref, ln_g_ref, ln_b_ref, w_s_ref, b_s_ref,
                 qn_g_ref, kvn_g_ref, w_q_ref, w_uk_ref, cos_ref, sin_ref,
                 out_a_ref, gate_b_ref, qidx_ref, widx_ref, qcat_ref, kidx_ref, kcat_ref, vcat_ref):
    x = x_ref[0]
    h = x * lax.rsqrt(jnp.mean(x * x, axis=-1, keepdims=True) + EPS) * pre_g_ref[...]
    proj = _dot(_bf16(h), w_all_ref[...])

    uv = _gelu_exact(proj[:, _C_UV:_C_ZA])
    u, v = uv[:, :GMLP_WIDTH], uv[:, GMLP_WIDTH:]
    mu = jnp.mean(v, axis=-1, keepdims=True)
    vc = v - mu
    var = jnp.mean(vc * vc, axis=-1, keepdims=True)
    vn = _bf16(vc * lax.rsqrt(var + EPS) * ln_g_ref[...] + ln_b_ref[...])
    gate_a = _silu(proj[:, _C_ZA:_C_CQ])
    t_chunk = _chunk_of(lax.broadcasted_iota(jnp.int32, (GMLP_BLOCK, GMLP_BLOCK), 0))
    s_chunk = _chunk_of(lax.broadcasted_iota(jnp.int32, (GMLP_BLOCK, GMLP_BLOCK), 1))
    for g in range(GMLP_GROUPS):
        w_g = _bf16(jnp.where(s_chunk <= t_chunk, w_s_ref[g], 0.0))
        cols = slice(g * GMLP_GROUP_DIM, (g + 1) * GMLP_GROUP_DIM)
        for r in range(TM // GMLP_BLOCK):
            rows = slice(r * GMLP_BLOCK, (r + 1) * GMLP_BLOCK)
            y = _dot(w_g, vn[rows, cols]) + b_s_ref[g]
            out_a_ref[0, rows, cols] = _bf16(u[rows, cols] * y * gate_a[rows, cols])

    gate_b_ref[0] = _silu(proj[:, _C_ZB:_C_END])

    c_q = proj[:, _C_CQ:_C_CKV]
    c_q = c_q * lax.rsqrt(jnp.mean(c_q * c_q, axis=-1, keepdims=True) + EPS) * qn_g_ref[...]
    c_kv = proj[:, _C_CKV:_C_KR]
    c_kv = _bf16(c_kv * lax.rsqrt(jnp.mean(c_kv * c_kv, axis=-1, keepdims=True) + EPS) * kvn_g_ref[...])
    cos = cos_ref[...]
    sin = sin_ref[...]
    k_rope = proj[:, _C_KR:_C_KRS] * cos[:, :LANES] + proj[:, _C_KRS:_C_SLAB] * sin[:, :LANES]
    slab = proj[:, _C_SLAB:_C_ZB]
    kidx_ref[0] = _bf16(slab)
    widx_ref[0] = slab
    kcat_ref[0, :, :DSA_KV_RANK] = c_kv
    kcat_ref[0, :, DSA_KV_RANK:] = _bf16(k_rope)
    lane = lax.broadcasted_iota(jnp.int32, (TM, LANES), 1)
    vcat_ref[0, :, :DSA_KV_RANK] = c_kv
    vcat_ref[0, :, DSA_KV_RANK:] = jnp.where(lane == 0, 1.0, 0.0).astype(jnp.bfloat16)

    q_all = _dot(_bf16(c_q), w_q_ref[...])
    q_rope = q_all[:, _Q_ROPE:_Q_ROPES] * cos + q_all[:, _Q_ROPES:_Q_IDX] * sin
    scale = np.float32(1.0 / math.sqrt(DSA_NOPE_DIM + DSA_ROPE_DIM))
    for hd in range(DSA_HEADS):
        cols = slice(_Q_NOPE + hd * LANES, _Q_NOPE + (hd + 1) * LANES)
        q_lat = _dot(_bf16(q_all[:, cols]), w_uk_ref[hd])
        grp, sub = divmod(hd, ROPE_LANE_GROUPS)
        own = (lane >= sub * DSA_ROPE_DIM) & (lane < (sub + 1) * DSA_ROPE_DIM)
        q_r = jnp.where(own, q_rope[:, grp * LANES:(grp + 1) * LANES] * scale, 0.0)
        for t in range(TM // TQ):
            src = slice(t * TQ, (t + 1) * TQ)
            dst = slice(hd * TQ, (hd + 1) * TQ)
            qcat_ref[0, t, dst, :DSA_KV_RANK] = _bf16(q_lat[src] * scale)
            qcat_ref[0, t, dst, DSA_KV_RANK:] = _bf16(q_r[src])
        icol = slice(_Q_IDX + hd * LANES, _Q_IDX + (hd + 1) * LANES)
        qidx_ref[0, hd] = _bf16(q_all[:, icol])


def _dsa_kernel(qidx_ref, widx_ref, qcat_ref, kidx_ref, kcat_ref, vcat_ref, o_ref,
                sc_ref, wrep_ref, bias_ref, m_ref, acc_ref, kmax_ref, *, top_k):
    i = pl.program_id(1)
    n_blocks = (i * TQ) // KB + 1
    chunks = [slice(c * LANES, (c + 1) * LANES) for c in range(KB // LANES)]
    idx_scale = np.float32(IDX_HEADS ** -0.5 * IDX_DIM ** -0.5)

    row = lax.broadcasted_iota(jnp.int32, (TQ, LANES), 0)
    row_t = lax.broadcasted_iota(jnp.int32, (ROW_TILE, LANES), 0)
    lane_t = lax.broadcasted_iota(jnp.int32, (ROW_TILE, LANES), 1)
    n_adm = (_chunk_of(i * TQ + row) + 1) * CHUNK

    def lane_all(op, a):
        return jnp.broadcast_to(op(a, axis=1, keepdims=True), a.shape)

    w = widx_ref[0] * idx_scale
    for hd in range(IDX_HEADS):
        wrep_ref[hd] = jnp.broadcast_to(w[:, W_IDX_LANE + hd:W_IDX_LANE + hd + 1], (TQ, LANES))

    def score_block(kb, carry):
        grp_max = [[], []]
        for r in range(TQ // ROW_TILE):
            rows = slice(r * ROW_TILE, (r + 1) * ROW_TILE)
            acc = []
            for half in range(KB // KEY_TILE):
                k = kidx_ref[0, pl.ds(pl.multiple_of(kb * KB + half * KEY_TILE, KEY_TILE), KEY_TILE), :]
                acc_h = [jnp.zeros((ROW_TILE, LANES), jnp.float32) for _ in range(KEY_TILE // LANES)]
                for hd in range(IDX_HEADS):
                    logit = _dot_nt(qidx_ref[0, hd, rows, :], k)
                    w_hd = wrep_ref[hd, rows, :]
                    for c in range(KEY_TILE // LANES):
                        acc_h[c] = acc_h[c] + w_hd * jnp.maximum(logit[:, c * LANES:(c + 1) * LANES], 0.0)
                acc += acc_h
            part = [carry[0][rows], carry[1][rows]]
            q_chunk_r = _chunk_of(i * TQ + r * ROW_TILE + row_t)
            for c, cols in enumerate(chunks):
                adm = _chunk_of(kb * KB + c * LANES + lane_t) <= q_chunk_r
                sc_ref[kb, rows, cols] = jnp.where(adm, _sort_key(acc[c]), KEY_NONE)
                part[c % 2] = jnp.maximum(part[c % 2], jnp.where(adm, acc[c], -jnp.inf))
            grp_max[0].append(part[0])
            grp_max[1].append(part[1])
        return jnp.concatenate(grp_max[0], axis=0), jnp.concatenate(grp_max[1], axis=0)

    neg_inf = jnp.full((TQ, LANES), -jnp.inf, jnp.float32)
    grp_a, grp_b = lax.fori_loop(0, n_blocks, score_block, (neg_inf, neg_inf))

    kf = np.float32(top_k)

    def count_ge(mid):
        parts = []
        for r in range(TQ // ROW_TILE):
            rows = slice(r * ROW_TILE, (r + 1) * ROW_TILE)
            mid_r = mid[rows]

            def body(kb, cnt, rows=rows, mid_r=mid_r):
                for cols in chunks:
                    cnt = cnt + jnp.where(sc_ref[kb, rows, cols] >= mid_r, 1.0, 0.0)
                return cnt
            parts.append(lax.fori_loop(0, n_blocks, body, jnp.zeros((ROW_TILE, LANES), jnp.float32)))
        return lane_all(jnp.sum, jnp.concatenate(parts, axis=0))

    def pivot(lo, hi):
        return (lo & hi) + lax.shift_right_arithmetic(lo ^ hi, jnp.int32(1))

    def pivot_zero_first(lo, hi):
        return jnp.where((lo <= 0) & (hi > 1), 1, jnp.where((lo < 0) & (hi == 1), 0, pivot(lo, hi)))

    select_all = n_adm <= top_k
    grp_min = lane_all(jnp.min, jnp.minimum(grp_a, grp_b))
    lo0 = jnp.where(select_all, KEY_NONE + 1, _sort_key(grp_min))
    hi0 = jnp.where(select_all, KEY_NONE + 1, _sort_key(lane_all(jnp.max, jnp.maximum(grp_a, grp_b))) + 1)

    def step(lo, hi, mid):
        cnt = count_ge(mid)
        ge = cnt >= kf
        return jnp.where(ge, mid, lo), jnp.where(cnt == kf, mid, jnp.where(ge, hi, mid))

    n_grp = np.float32(2 * LANES)
    grp_mean = lane_all(jnp.sum, grp_a + grp_b) / n_grp
    grp_var = lane_all(jnp.sum, grp_a * grp_a + grp_b * grp_b) / n_grp - grp_mean * grp_mean
    grp_std = jnp.sqrt(jnp.maximum(grp_var, 0.0))
    lo, hi = lo0, hi0
    for below in FIRST_PIVOTS:
        guess = _sort_key(grp_mean - np.float32(below) * grp_std)
        usable = (grp_min > -jnp.inf) & (guess > lo) & (guess < hi)
        lo, hi = step(lo, hi, jnp.where(usable, guess, pivot(lo, hi)))
    for _ in range(2):
        lo, hi = step(lo, hi, pivot_zero_first(lo, hi))

    def bisect_cond(state):
        it, lo, _, mid = state
        return jnp.logical_and(it < MAX_BISECT, jnp.max(jnp.where(mid != lo, 1.0, 0.0)) > 0.0)

    def bisect_body(state):
        it, lo, hi, mid = state
        lo, hi = step(lo, hi, mid)
        return it + 1, lo, hi, pivot(lo, hi)

    _, thr, hi, _ = lax.while_loop(bisect_cond, bisect_body, (jnp.int32(0), lo, hi, pivot(lo, hi)))
    tie = hi != thr

    @pl.when(jnp.max(jnp.where(tie, 1.0, 0.0)) > 0.0)
    def _():
        need = kf - count_ge(thr + 1)
        upto = (lax.broadcasted_iota(jnp.int32, (KB, KB), 0)
                <= lax.broadcasted_iota(jnp.int32, (KB, KB), 1)).astype(jnp.bfloat16)

        def drop_body(kb, seen):
            keys = [sc_ref[kb, :, cols] for cols in chunks]
            band = [tie & (key == thr) for key in keys]
            band_f = [jnp.where(b, 1.0, 0.0) for b in band]
            rank = _dot(jnp.concatenate([_bf16(b) for b in band_f], axis=1), upto)
            for c, cols in enumerate(chunks):
                late = band[c] & (rank[:, cols] + seen > need)
                sc_ref[kb, :, cols] = jnp.where(late, KEY_NONE, keys[c])
            return seen + lane_all(jnp.sum, functools.reduce(jnp.add, band_f))
        lax.fori_loop(0, n_blocks, drop_body, jnp.zeros((TQ, LANES), jnp.float32))

    @pl.when(i == 0)
    def _():
        def norm_block(kb, run):
            kc = kcat_ref[0, pl.ds(pl.multiple_of(kb * KB, KB), KB), :].astype(jnp.float32)
            return jnp.maximum(run, jnp.sum(kc * kc, axis=1, keepdims=True))
        run = lax.fori_loop(0, kcat_ref.shape[1] // KB, norm_block, jnp.zeros((KB, 1), jnp.float32))
        kmax_ref[...] = jnp.broadcast_to(jnp.max(run, axis=0, keepdims=True), kmax_ref.shape)

    k_norm2 = kmax_ref[0:1, :]
    q_all = qcat_ref[0, 0]
    q_norm2 = _dot(q_all * q_all, jnp.ones((Q_CAT, LANES), jnp.bfloat16))
    qk2 = jnp.maximum(q_norm2 * k_norm2, 1e-30)
    m_ref[...] = qk2 * lax.rsqrt(qk2) * BOUND_SLACK

    def set_bias(kb, slot=0):
        for cols in chunks:
            bias_ref[slot, :, cols] = jnp.where(sc_ref[kb, :, cols] >= thr, 0.0, NEG_BIG)

    def bias_rows(cols, slot=0):
        return jnp.concatenate([bias_ref[slot, :, cols]] * HEAD_GROUP, axis=0)

    group_rows = [slice(g * HEAD_GROUP * TQ, (g + 1) * HEAD_GROUP * TQ) for g in range(DSA_HEADS // HEAD_GROUP)]

    def attend(blocks_per_step):
        acc_ref[...] = jnp.zeros(acc_ref.shape, jnp.float32)

        def attn_step(kbs):
            rows = [pl.ds(pl.multiple_of(kb * KB, KB), KB) for kb in kbs]
            for slot, kb in enumerate(kbs):
                set_bias(kb, slot)
            for grp in group_rows:
                m = m_ref[grp, :]
                pv = []
                for slot, r in enumerate(rows):
                    s = _dot_nt(qcat_ref[0, 0, grp, :], kcat_ref[0, r, :])
                    p = [_bf16(jnp.exp(s[:, cols] + bias_rows(cols, slot) - m)) for cols in chunks]
                    pv.append(_dot(jnp.concatenate(p, axis=1), vcat_ref[0, r, :]))
                acc_ref[grp, :] += functools.reduce(jnp.add, pv)

        def full_step(j, carry):
            attn_step([j * blocks_per_step + b for b in range(blocks_per_step)])
            return carry
        n_full = n_blocks // blocks_per_step
        lax.fori_loop(0, n_full, full_step, 0)
        for b in range(blocks_per_step - 1):
            @pl.when(n_full * blocks_per_step + b < n_blocks)
            def _():
                attn_step([n_full * blocks_per_step + b])
        return jnp.min(acc_ref[:, DSA_KV_RANK:DSA_KV_RANK + 1])

    @pl.when(attend(ATTN_BLOCKS) < ROW_SUM_MIN)
    def _():
        m_ref[...] = jnp.full(m_ref.shape, NEG_BIG, jnp.float32)

        def max_block(kb, carry):
            kc = kcat_ref[0, pl.ds(pl.multiple_of(kb * KB, KB), KB), :]
            set_bias(kb)
            for grp in group_rows:
                s = _dot_nt(qcat_ref[0, 0, grp, :], kc)
                sm = [s[:, cols] + bias_rows(cols) for cols in chunks]
                m_ref[grp, :] = jnp.maximum(m_ref[grp, :], functools.reduce(jnp.maximum, sm))
            return carry
        lax.fori_loop(0, n_blocks, max_block, 0)
        m_ref[...] = lane_all(jnp.max, m_ref[...])
        attend(1)

    for hd in range(DSA_HEADS):
        a = acc_ref[hd * TQ:(hd + 1) * TQ, :]
        denom = jnp.broadcast_to(a[:, DSA_KV_RANK:DSA_KV_RANK + 1], (TQ, DSA_KV_RANK))
        o_ref[0, :, hd * DSA_KV_RANK:(hd + 1) * DSA_KV_RANK] = _bf16(a[:, :DSA_KV_RANK] / denom)


def _out_kernel(x_ref, out_a_ref, o_lat_ref, gate_b_ref, w_uv_ref, w_out_ref, post_g_ref, y_ref):
    o = _dot(o_lat_ref[0], w_uv_ref[...])
    out_b = _bf16(o * gate_b_ref[0])
    y = _dot(out_a_ref[0], w_out_ref[:GMLP_WIDTH, :]) + _dot(out_b, w_out_ref[GMLP_WIDTH:, :])
    y = y * lax.rsqrt(jnp.mean(y * y, axis=-1, keepdims=True) + EPS) * post_g_ref[...]
    y_ref[0] = x_ref[0] + y


def _full(shape):
    return pl.BlockSpec(shape, lambda b, t: (0,) * len(shape))


def _layer(x, w_in, pre_g, post_g, ln_g, ln_b, w_s, b_s, qn_g, kvn_g, w_uq, w_uk, w_uv, w_q_idx, w_out):
    B, S, D = x.shape
    assert D == D_MODEL and S % KB == 0 and S % TM == 0 and TM % TQ == 0 and KB % TQ == 0
    top_k = min(TOPK_MAX, S // 4)
    assert top_k <= 2 * LANES
    f32, bf16 = jnp.float32, jnp.bfloat16

    kr = w_in[:, 1920:1952]
    kr_sw = jnp.concatenate([kr[:, ROPE_HALF:], kr[:, :ROPE_HALF]], axis=1)
    w_all = jnp.concatenate([
        w_in[:, :1920], jnp.tile(kr, (1, ROPE_LANE_GROUPS)), jnp.tile(kr_sw, (1, ROPE_LANE_GROUPS)),
        w_in[:, 1952:2024], jnp.zeros((D, LANES - IDX_DIM - IDX_HEADS), f32), w_in[:, 2024:]], axis=1).astype(bf16)
    wq3 = w_uq.reshape(DSA_Q_RANK, DSA_HEADS, DSA_NOPE_DIM + DSA_ROPE_DIM)
    nope = jnp.pad(wq3[:, :, :DSA_NOPE_DIM], ((0, 0), (0, 0), (0, LANES - DSA_NOPE_DIM)))
    rope = wq3[:, :, DSA_NOPE_DIM:]
    rope_sw = jnp.concatenate([rope[:, :, ROPE_HALF:], rope[:, :, :ROPE_HALF]], axis=2)
    wqi = jnp.pad(w_q_idx.reshape(DSA_Q_RANK, IDX_HEADS, IDX_DIM), ((0, 0), (0, 0), (0, LANES - IDX_DIM)))
    w_q = jnp.concatenate([nope.reshape(DSA_Q_RANK, -1), rope.reshape(DSA_Q_RANK, -1),
                           rope_sw.reshape(DSA_Q_RANK, -1), wqi.reshape(DSA_Q_RANK, -1)], axis=1).astype(bf16)
    w_uk_t = jnp.pad(jnp.transpose(w_uk, (1, 2, 0)), ((0, 0), (0, LANES - DSA_NOPE_DIM), (0, 0))).astype(bf16)
    eye = jnp.eye(DSA_HEADS, dtype=f32)
    w_uv_bd = (jnp.transpose(w_uv, (1, 0, 2))[:, :, None, :] * eye[:, None, :, None]).reshape(
        DSA_HEADS * DSA_KV_RANK, DSA_WIDTH).astype(bf16)
    b_s_b = jnp.broadcast_to(b_s[:, :, None], (GMLP_GROUPS, GMLP_BLOCK, GMLP_GROUP_DIM))

    pos = jnp.arange(S, dtype=f32)
    inv_freq = ROPE_THETA ** (-jnp.arange(0, DSA_ROPE_DIM, 2, dtype=f32) / DSA_ROPE_DIM)
    ang = pos[:, None] * inv_freq[None, :]
    cos_t = jnp.tile(jnp.concatenate([jnp.cos(ang), jnp.cos(ang)], axis=1), (1, DSA_HEADS))
    sin_t = jnp.tile(jnp.concatenate([-jnp.sin(ang), jnp.sin(ang)], axis=1), (1, DSA_HEADS))

    row2 = lambda a: a.reshape(1, -1)
    tok = lambda width: pl.BlockSpec((1, TM, width), lambda b, t: (b, t, 0))
    hm = lambda width: pl.BlockSpec((1, DSA_HEADS, TM, width), lambda b, t: (b, 0, t, 0))

    out_a, gate_b, qidx, widx, qcat, kidx, kcat, vcat = pl.pallas_call(
        _proj_kernel,
        grid=(B, S // TM),
        in_specs=[tok(D), _full((1, D)), _full((D, _C_END)), _full((1, GMLP_WIDTH)), _full((1, GMLP_WIDTH)),
                  _full((GMLP_GROUPS, GMLP_BLOCK, GMLP_BLOCK)), _full((GMLP_GROUPS, GMLP_BLOCK, GMLP_GROUP_DIM)),
                  _full((1, DSA_Q_RANK)), _full((1, DSA_KV_RANK)), _full((DSA_Q_RANK, _Q_END)),
                  _full((DSA_HEADS, LANES, DSA_KV_RANK)),
                  pl.BlockSpec((TM, 2 * LANES), lambda b, t: (t, 0)),
                  pl.BlockSpec((TM, 2 * LANES), lambda b, t: (t, 0))],
        out_specs=[tok(GMLP_WIDTH), tok(DSA_WIDTH), hm(LANES), tok(LANES),
                   pl.BlockSpec((1, TM // TQ, DSA_HEADS * TQ, Q_CAT), lambda b, t: (b, t, 0, 0)),
                   tok(LANES), tok(Q_CAT), tok(Q_CAT)],
        out_shape=[jax.ShapeDtypeStruct((B, S, GMLP_WIDTH), bf16),
                   jax.ShapeDtypeStruct((B, S, DSA_WIDTH), f32),
                   jax.ShapeDtypeStruct((B, DSA_HEADS, S, LANES), bf16),
                   jax.ShapeDtypeStruct((B, S, LANES), f32),
                   jax.ShapeDtypeStruct((B, S // TQ, DSA_HEADS * TQ, Q_CAT), bf16),
                   jax.ShapeDtypeStruct((B, S, LANES), bf16),
                   jax.ShapeDtypeStruct((B, S, Q_CAT), bf16),
                   jax.ShapeDtypeStruct((B, S, Q_CAT), bf16)],
        compiler_params=pltpu.CompilerParams(
            dimension_semantics=("arbitrary", "arbitrary"), vmem_limit_bytes=48 * 1024 * 1024),
        name="proj_gmlp_dsa_prep",
    )(x, row2(pre_g), w_all, row2(ln_g), row2(ln_b), w_s, b_s_b, row2(qn_g), row2(kvn_g), w_q, w_uk_t,
      cos_t, sin_t)

    qt = lambda width: pl.BlockSpec((1, DSA_HEADS, TQ, width), lambda b, t: (b, 0, t, 0))
    seq = lambda width: pl.BlockSpec((1, S, width), lambda b, t: (b, 0, 0), pipeline_mode=pl.Buffered(1))
    o_lat = pl.pallas_call(
        functools.partial(_dsa_kernel, top_k=top_k),
        grid=(B, S // TQ),
        in_specs=[qt(LANES), pl.BlockSpec((1, TQ, LANES), lambda b, t: (b, t, 0)),
                  pl.BlockSpec((1, 1, DSA_HEADS * TQ, Q_CAT), lambda b, t: (b, t, 0, 0)),
                  seq(LANES), seq(Q_CAT), seq(Q_CAT)],
        out_specs=pl.BlockSpec((1, TQ, DSA_HEADS * DSA_KV_RANK), lambda b, t: (b, t, 0)),
        out_shape=jax.ShapeDtypeStruct((B, S, DSA_HEADS * DSA_KV_RANK), bf16),
        scratch_shapes=[pltpu.VMEM((S // KB, TQ, KB), jnp.int32),
                        pltpu.VMEM((IDX_HEADS, TQ, LANES), f32),
                        pltpu.VMEM((ATTN_BLOCKS, TQ, KB), f32),
                        pltpu.VMEM((DSA_HEADS * TQ, LANES), f32),
                        pltpu.VMEM((DSA_HEADS * TQ, Q_CAT), f32),
                        pltpu.VMEM((8, LANES), f32)],
        compiler_params=pltpu.CompilerParams(
            dimension_semantics=("arbitrary", "arbitrary"), vmem_limit_bytes=56 * 1024 * 1024),
        name="dsa_index_select_attend",
    )(qidx, widx, qcat, kidx, kcat, vcat)

    return pl.pallas_call(
        _out_kernel,
        grid=(B, S // TM),
        in_specs=[tok(D), tok(GMLP_WIDTH), tok(DSA_HEADS * DSA_KV_RANK), tok(DSA_WIDTH),
                  _full((DSA_HEADS * DSA_KV_RANK, DSA_WIDTH)), _full((GMLP_WIDTH + DSA_WIDTH, D)), _full((1, D))],
        out_specs=tok(D),
        out_shape=jax.ShapeDtypeStruct((B, S, D), f32),
        compiler_params=pltpu.CompilerParams(
            dimension_semantics=("arbitrary", "arbitrary"), vmem_limit_bytes=32 * 1024 * 1024),
        name="out_proj_norm_residual",
    )(x, out_a, o_lat, gate_b, w_uv_bd, w_out.astype(bf16), row2(post_g))


def kernel(x, w_in, pre_norm_g, post_norm_g, gmlp_ln_g, gmlp_ln_b, gmlp_w_s, gmlp_b_s, dsa_q_norm_g, dsa_kv_norm_g, dsa_w_uq, dsa_w_uk, dsa_w_uv, dsa_w_q_idx, w_out):
    for l in range(w_in.shape[0]):
        x = _layer(x, w_in[l], pre_norm_g[l], post_norm_g[l], gmlp_ln_g[l], gmlp_ln_b[l], gmlp_w_s[l],
                   gmlp_b_s[l], dsa_q_norm_g[l], dsa_kv_norm_g[l], dsa_w_uq[l], dsa_w_uk[l], dsa_w_uv[l],
                   dsa_w_q_idx[l], w_out[l])
    return x
```

```python
import functools
import math

import jax
import jax.numpy as jnp
import numpy as np
from jax import lax
from jax.experimental import pallas as pl
from jax.experimental.pallas import tpu as pltpu

D_MODEL = 1024
CHUNK = 64
EPS = 1e-6
GMLP_GROUPS = 4
GMLP_GROUP_DIM = 128
GMLP_WIDTH = GMLP_GROUPS * GMLP_GROUP_DIM
GMLP_BLOCK = 128
DSA_HEADS = 8
DSA_V_DIM = 64
DSA_WIDTH = DSA_HEADS * DSA_V_DIM
DSA_NOPE_DIM = 64
DSA_ROPE_DIM = 32
DSA_Q_RANK = 256
DSA_KV_RANK = 128
IDX_HEADS = 8
IDX_DIM = 64
TOPK_MAX = 256
ROPE_THETA = 10000.0

LANES = 128
ROPE_HALF = DSA_ROPE_DIM // 2
ROPE_LANE_GROUPS = LANES // DSA_ROPE_DIM
Q_CAT = DSA_KV_RANK + LANES
W_IDX_LANE = IDX_DIM

TM = 512
TQ = 512
KB = 512
KEY_TILE = 128
Q_TILE = 256
COUNT_ROWS = 32
N_GROUPS = 256
V_ROWS = DSA_KV_RANK + 16
NEG_BIG = -1e30
ROW_SUM_MIN = 1e-26
BOUND_SLACK = 1.02
KEY_NONE = -2 ** 31
FIRST_PIVOTS = (1.0, 0.0)
MAX_BISECT = 40

_C_UV, _C_ZA, _C_CQ, _C_CKV = 0, 1024, 1536, 1792
_C_KR, _C_KRS, _C_SLAB, _C_ZB, _C_END = 1920, 2048, 2176, 2304, 2816
_Q_NOPE, _Q_ROPE, _Q_ROPES, _Q_IDX, _Q_END = 0, 1024, 1280, 1536, 2560

_NT = (((1,), (1,)), ((), ()))


def _bf16(a):
    return a.astype(jnp.bfloat16)


def _dot(a, b):
    return jnp.dot(a, b, preferred_element_type=jnp.float32)


def _dot_nt(a, b):
    return lax.dot_general(a, b, _NT, preferred_element_type=jnp.float32)


def _chunk_of(pos):
    return lax.shift_right_logical(pos, jnp.int32(CHUNK.bit_length() - 1))


def _sort_key(s):
    bits = pltpu.bitcast(s, jnp.int32)
    return jnp.where(bits < 0, bits ^ jnp.int32(0x7FFFFFFF), bits)


def _silu(z):
    return z / (1.0 + jnp.exp(-z))


def _gelu_exact(a):
    return 0.5 * a * (1.0 + lax.erf(a * np.float32(math.sqrt(0.5))))


def _proj_kernel(x_ref, pre_g_ref, w_all_ref, ln_g_ref, ln_b_ref, w_s_ref, b_s_ref,
                 qn_g_ref, kvn_g_ref, w_q_ref, w_uk_ref, cos_ref, sin_ref,
                 out_a_ref, gate_b_ref, qidx_ref, widx_ref, qcat_ref, kidx_ref, kcat_ref, vt_ref):
    x = x_ref[0]
    h = x * lax.rsqrt(jnp.mean(x * x, axis=-1, keepdims=True) + EPS) * pre_g_ref[...]
    proj = _dot(_bf16(h), w_all_ref[...])

    uv = _gelu_exact(proj[:, _C_UV:_C_ZA])
    u, v = uv[:, :GMLP_WIDTH], uv[:, GMLP_WIDTH:]
    mu = jnp.mean(v, axis=-1, keepdims=True)
    vc = v - mu
    var = jnp.mean(vc * vc, axis=-1, keepdims=True)
    vn = _bf16(vc * lax.rsqrt(var + EPS) * ln_g_ref[...] + ln_b_ref[...])
    gate_a = _silu(proj[:, _C_ZA:_C_CQ])
    t_chunk = _chunk_of(lax.broadcasted_iota(jnp.int32, (GMLP_BLOCK, GMLP_BLOCK), 0))
    s_chunk = _chunk_of(lax.broadcasted_iota(jnp.int32, (GMLP_BLOCK, GMLP_BLOCK), 1))
    for g in range(GMLP_GROUPS):
        w_g = _bf16(jnp.where(s_chunk <= t_chunk, w_s_ref[g], 0.0))
        cols = slice(g * GMLP_GROUP_DIM, (g + 1) * GMLP_GROUP_DIM)
        for r in range(TM // GMLP_BLOCK):
            rows = slice(r * GMLP_BLOCK, (r + 1) * GMLP_BLOCK)
            y = _dot(w_g, vn[rows, cols]) + b_s_ref[g]
            out_a_ref[0, rows, cols] = _bf16(u[rows, cols] * y * gate_a[rows, cols])

    gate_b_ref[0] = _silu(proj[:, _C_ZB:_C_END])

    c_q = proj[:, _C_CQ:_C_CKV]
    c_q = c_q * lax.rsqrt(jnp.mean(c_q * c_q, axis=-1, keepdims=True) + EPS) * qn_g_ref[...]
    c_kv = proj[:, _C_CKV:_C_KR]
    c_kv32 = c_kv * lax.rsqrt(jnp.mean(c_kv * c_kv, axis=-1, keepdims=True) + EPS) * kvn_g_ref[...]
    c_kv = _bf16(c_kv32)
    cos = cos_ref[...]
    sin = sin_ref[...]
    k_rope = proj[:, _C_KR:_C_KRS] * cos[:, :LANES] + proj[:, _C_KRS:_C_SLAB] * sin[:, :LANES]
    slab = proj[:, _C_SLAB:_C_ZB]
    kidx_ref[0] = _bf16(slab)
    widx_ref[0] = slab.T
    kcat_ref[0, :, :DSA_KV_RANK] = c_kv
    kcat_ref[0, :, DSA_KV_RANK:] = _bf16(k_rope)
    lane = lax.broadcasted_iota(jnp.int32, (TM, LANES), 1)
    extra = lax.broadcasted_iota(jnp.int32, (V_ROWS - DSA_KV_RANK, KB), 0)
    for t in range(TM // KB):
        rows = slice(t * KB, (t + 1) * KB)
        vt_ref[0, t, :DSA_KV_RANK, :] = _bf16(c_kv32[rows].T)
        vt_ref[0, t, DSA_KV_RANK:, :] = jnp.where(extra == 0, 1.0, 0.0).astype(jnp.bfloat16)

    q_all = _dot(_bf16(c_q), w_q_ref[...])
    q_rope = q_all[:, _Q_ROPE:_Q_ROPES] * cos + q_all[:, _Q_ROPES:_Q_IDX] * sin
    scale = np.float32(1.0 / math.sqrt(DSA_NOPE_DIM + DSA_ROPE_DIM))
    for hd in range(DSA_HEADS):
        cols = slice(_Q_NOPE + hd * LANES, _Q_NOPE + (hd + 1) * LANES)
        q_lat = _dot(_bf16(q_all[:, cols]), w_uk_ref[hd])
        grp, sub = divmod(hd, ROPE_LANE_GROUPS)
        own = (lane >= sub * DSA_ROPE_DIM) & (lane < (sub + 1) * DSA_ROPE_DIM)
        q_r = jnp.where(own, q_rope[:, grp * LANES:(grp + 1) * LANES] * scale, 0.0)
        for t in range(TM // TQ):
            src = slice(t * TQ, (t + 1) * TQ)
            dst = slice(hd * TQ, (hd + 1) * TQ)
            qcat_ref[0, t, dst, :DSA_KV_RANK] = _bf16(q_lat[src] * scale)
            qcat_ref[0, t, dst, DSA_KV_RANK:] = _bf16(q_r[src])
        icol = slice(_Q_IDX + hd * LANES, _Q_IDX + (hd + 1) * LANES)
        qidx_ref[0, hd] = _bf16(q_all[:, icol])


def _dsa_kernel(qidx_ref, widx_ref, qcat_ref, kidx_ref, kcat_ref, vt_ref, o_ref,
                sc_ref, gmax_ref, bias_ref, m_ref, acc_ref, kmax_ref, *, top_k):
    i = pl.program_id(1)
    n_blocks = (i * TQ) // KB + 1
    idx_scale = np.float32(IDX_HEADS ** -0.5 * IDX_DIM ** -0.5)
    f32 = jnp.float32

    q_chunk = _chunk_of(i * TQ + lax.broadcasted_iota(jnp.int32, (1, TQ), 1))
    key_t = lax.broadcasted_iota(jnp.int32, (KEY_TILE, Q_TILE), 0)

    def over_keys(op, a):
        return op(a, axis=0, keepdims=True)

    w = widx_ref[0] * idx_scale
    gmax_ref[...] = jnp.full(gmax_ref.shape, -jnp.inf, f32)

    def score_block(kb, carry):
        for ks in range(KB // KEY_TILE):
            k = kidx_ref[0, pl.ds(pl.multiple_of(kb * KB + ks * KEY_TILE, KEY_TILE), KEY_TILE), :]
            krows = slice(ks * KEY_TILE, (ks + 1) * KEY_TILE)
            grows = slice((ks * KEY_TILE) % N_GROUPS, (ks * KEY_TILE) % N_GROUPS + KEY_TILE)
            for qt in range(TQ // Q_TILE):
                qs = slice(qt * Q_TILE, (qt + 1) * Q_TILE)
                acc = jnp.zeros((KEY_TILE, Q_TILE), f32)
                for hd in range(IDX_HEADS):
                    logit = _dot_nt(k, qidx_ref[0, hd, qs, :])
                    acc = acc + w[W_IDX_LANE + hd:W_IDX_LANE + hd + 1, qs] * jnp.maximum(logit, 0.0)
                adm = _chunk_of(kb * KB + ks * KEY_TILE + key_t) <= q_chunk[:, qs]
                sc_ref[kb, krows, qs] = jnp.where(adm, _sort_key(acc), KEY_NONE)
                gmax_ref[grows, qs] = jnp.maximum(gmax_ref[grows, qs], jnp.where(adm, acc, -jnp.inf))
        return carry
    lax.fori_loop(0, n_blocks, score_block, 0)

    kf = np.float32(top_k)

    def count_ge(mid):
        def body(kb, cnt):
            for r in range(KB // COUNT_ROWS):
                cnt = cnt + jnp.where(sc_ref[kb, r * COUNT_ROWS:(r + 1) * COUNT_ROWS, :] >= mid, 1.0, 0.0)
            return cnt
        return over_keys(jnp.sum, lax.fori_loop(0, n_blocks, body, jnp.zeros((COUNT_ROWS, TQ), f32)))

    def pivot(lo, hi):
        return (lo & hi) + lax.shift_right_arithmetic(lo ^ hi, jnp.int32(1))

    def pivot_zero_first(lo, hi):
        return jnp.where((lo <= 0) & (hi > 1), 1, jnp.where((lo < 0) & (hi == 1), 0, pivot(lo, hi)))

    grp = gmax_ref[...]
    grp_min = over_keys(jnp.min, grp)
    select_all = (q_chunk + 1) * CHUNK <= top_k
    lo = jnp.where(select_all, KEY_NONE + 1, _sort_key(grp_min))
    hi = jnp.where(select_all, KEY_NONE + 1, _sort_key(over_keys(jnp.max, grp)) + 1)

    def step(lo, hi, mid):
        cnt = count_ge(mid)
        ge = cnt >= kf
        return jnp.where(ge, mid, lo), jnp.where(cnt == kf, mid, jnp.where(ge, hi, mid))

    grp_mean = over_keys(jnp.sum, grp) / np.float32(N_GROUPS)
    grp_var = over_keys(jnp.sum, grp * grp) / np.float32(N_GROUPS) - grp_mean * grp_mean
    grp_std = jnp.sqrt(jnp.maximum(grp_var, 0.0))
    for below in FIRST_PIVOTS:
        guess = _sort_key(grp_mean - np.float32(below) * grp_std)
        usable = (grp_min > -jnp.inf) & (guess > lo) & (guess < hi)
        lo, hi = step(lo, hi, jnp.where(usable, guess, pivot(lo, hi)))
    for _ in range(2):
        lo, hi = step(lo, hi, pivot_zero_first(lo, hi))

    def bisect_cond(state):
        it, lo, _, mid = state
        return jnp.logical_and(it < MAX_BISECT, jnp.max(jnp.where(mid != lo, 1.0, 0.0)) > 0.0)

    def bisect_body(state):
        it, lo, hi, mid = state
        lo, hi = step(lo, hi, mid)
        return it + 1, lo, hi, pivot(lo, hi)

    _, thr, hi, _ = lax.while_loop(bisect_cond, bisect_body, (jnp.int32(0), lo, hi, pivot(lo, hi)))
    tie = hi != thr

    @pl.when(jnp.max(jnp.where(tie, 1.0, 0.0)) > 0.0)
    def _():
        need = kf - count_ge(thr + 1)
        upto = (lax.broadcasted_iota(jnp.int32, (KB, KB), 1)
                <= lax.broadcasted_iota(jnp.int32, (KB, KB), 0)).astype(jnp.bfloat16)

        def drop_body(kb, seen):
            added = []
            for qt in range(TQ // Q_TILE):
                qs = slice(qt * Q_TILE, (qt + 1) * Q_TILE)
                keys = sc_ref[kb, :, qs]
                band = tie[:, qs] & (keys == thr[:, qs])
                band_f = jnp.where(band, 1.0, 0.0)
                rank = _dot(upto, _bf16(band_f))
                late = band & (rank + seen[:, qs] > need[:, qs])
                sc_ref[kb, :, qs] = jnp.where(late, KEY_NONE, keys)
                added.append(over_keys(jnp.sum, band_f))
            return seen + jnp.concatenate(added, axis=1)
        lax.fori_loop(0, n_blocks, drop_body, jnp.zeros((1, TQ), f32))

    @pl.when(i == 0)
    def _():
        def norm_block(kb, run):
            kc = kcat_ref[0, pl.ds(pl.multiple_of(kb * KB, KB), KB), :].astype(f32)
            return jnp.maximum(run, jnp.sum(kc * kc, axis=1, keepdims=True))
        run = lax.fori_loop(0, kcat_ref.shape[1] // KB, norm_block, jnp.zeros((KB, 1), f32))
        kmax_ref[...] = jnp.broadcast_to(jnp.max(run, axis=0, keepdims=True), kmax_ref.shape)

    k_norm2 = jnp.concatenate([kmax_ref[0:1, :]] * (TQ // LANES), axis=1)
    ones_row = jnp.ones((8, Q_CAT), jnp.bfloat16)
    head_rows = [slice(hd * TQ, (hd + 1) * TQ) for hd in range(DSA_HEADS)]
    for hd, rows in enumerate(head_rows):
        q = qcat_ref[0, 0, rows, :]
        qk2 = jnp.maximum(_dot_nt(ones_row, q * q)[0:1] * k_norm2, 1e-30)
        m_ref[hd:hd + 1, :] = qk2 * lax.rsqrt(qk2) * BOUND_SLACK

    def set_bias(kb):
        bias_ref[...] = jnp.where(sc_ref[kb] >= thr, 0.0, NEG_BIG)

    def attend():
        acc_ref[...] = jnp.zeros(acc_ref.shape, f32)

        def attn_block(kb, carry):
            kc = kcat_ref[0, pl.ds(pl.multiple_of(kb * KB, KB), KB), :]
            vt = vt_ref[0, kb]
            set_bias(kb)
            for hd, rows in enumerate(head_rows):
                s = _dot_nt(kc, qcat_ref[0, 0, rows, :])
                p = _bf16(jnp.exp(s + bias_ref[...] - m_ref[hd:hd + 1, :]))
                acc_ref[hd] += _dot(vt, p)
            return carry
        lax.fori_loop(0, n_blocks, attn_block, 0)
        return jnp.min(acc_ref[:, DSA_KV_RANK:DSA_KV_RANK + 1, :])

    @pl.when(attend() < ROW_SUM_MIN)
    def _():
        m_ref[...] = jnp.full(m_ref.shape, NEG_BIG, f32)

        def max_block(kb, carry):
            kc = kcat_ref[0, pl.ds(pl.multiple_of(kb * KB, KB), KB), :]
            set_bias(kb)
            for hd, rows in enumerate(head_rows):
                s = _dot_nt(kc, qcat_ref[0, 0, rows, :]) + bias_ref[...]
                m_ref[hd:hd + 1, :] = jnp.maximum(m_ref[hd:hd + 1, :], over_keys(jnp.max, s))
            return carry
        lax.fori_loop(0, n_blocks, max_block, 0)
        attend()

    for hd in range(DSA_HEADS):
        a = acc_ref[hd]
        o_t = a[:DSA_KV_RANK] / a[DSA_KV_RANK:DSA_KV_RANK + 1]
        o_ref[0, :, hd * DSA_KV_RANK:(hd + 1) * DSA_KV_RANK] = _bf16(o_t.T)


def _out_kernel(x_ref, out_a_ref, o_lat_ref, gate_b_ref, w_uv_ref, w_out_ref, post_g_ref, y_ref):
    o = _dot(o_lat_ref[0], w_uv_ref[...])
    out_b = _bf16(o * gate_b_ref[0])
    y = _dot(out_a_ref[0], w_out_ref[:GMLP_WIDTH, :]) + _dot(out_b, w_out_ref[GMLP_WIDTH:, :])
    y = y * lax.rsqrt(jnp.mean(y * y, axis=-1, keepdims=True) + EPS) * post_g_ref[...]
    y_ref[0] = x_ref[0] + y


def _full(shape):
    return pl.BlockSpec(shape, lambda b, t: (0,) * len(shape))


def _layer(x, w_in, pre_g, post_g, ln_g, ln_b, w_s, b_s, qn_g, kvn_g, w_uq, w_uk, w_uv, w_q_idx, w_out):
    B, S, D = x.shape
    assert D == D_MODEL and S % KB == 0 and S % TM == 0 and TM % TQ == 0 and KB % TQ == 0 and TM % KB == 0
    top_k = min(TOPK_MAX, S // 4)
    assert top_k <= N_GROUPS <= KB and KB % N_GROUPS == 0 and N_GROUPS % KEY_TILE == 0
    f32, bf16 = jnp.float32, jnp.bfloat16

    kr = w_in[:, 1920:1952]
    kr_sw = jnp.concatenate([kr[:, ROPE_HALF:], kr[:, :ROPE_HALF]], axis=1)
    w_all = jnp.concatenate([
        w_in[:, :1920], jnp.tile(kr, (1, ROPE_LANE_GROUPS)), jnp.tile(kr_sw, (1, ROPE_LANE_GROUPS)),
        w_in[:, 1952:2024], jnp.zeros((D, LANES - IDX_DIM - IDX_HEADS), f32), w_in[:, 2024:]], axis=1).astype(bf16)
    wq3 = w_uq.reshape(DSA_Q_RANK, DSA_HEADS, DSA_NOPE_DIM + DSA_ROPE_DIM)
    nope = jnp.pad(wq3[:, :, :DSA_NOPE_DIM], ((0, 0), (0, 0), (0, LANES - DSA_NOPE_DIM)))
    rope = wq3[:, :, DSA_NOPE_DIM:]
    rope_sw = jnp.concatenate([rope[:, :, ROPE_HALF:], rope[:, :, :ROPE_HALF]], axis=2)
    wqi = jnp.pad(w_q_idx.reshape(DSA_Q_RANK, IDX_HEADS, IDX_DIM), ((0, 0), (0, 0), (0, LANES - IDX_DIM)))
    w_q = jnp.concatenate([nope.reshape(DSA_Q_RANK, -1), rope.reshape(DSA_Q_RANK, -1),
                           rope_sw.reshape(DSA_Q_RANK, -1), wqi.reshape(DSA_Q_RANK, -1)], axis=1).astype(bf16)
    w_uk_t = jnp.pad(jnp.transpose(w_uk, (1, 2, 0)), ((0, 0), (0, LANES - DSA_NOPE_DIM), (0, 0))).astype(bf16)
    eye = jnp.eye(DSA_HEADS, dtype=f32)
    w_uv_bd = (jnp.transpose(w_uv, (1, 0, 2))[:, :, None, :] * eye[:, None, :, None]).reshape(
        DSA_HEADS * DSA_KV_RANK, DSA_WIDTH).astype(bf16)
    b_s_b = jnp.broadcast_to(b_s[:, :, None], (GMLP_GROUPS, GMLP_BLOCK, GMLP_GROUP_DIM))

    pos = jnp.arange(S, dtype=f32)
    inv_freq = ROPE_THETA ** (-jnp.arange(0, DSA_ROPE_DIM, 2, dtype=f32) / DSA_ROPE_DIM)
    ang = pos[:, None] * inv_freq[None, :]
    cos_t = jnp.tile(jnp.concatenate([jnp.cos(ang), jnp.cos(ang)], axis=1), (1, DSA_HEADS))
    sin_t = jnp.tile(jnp.concatenate([-jnp.sin(ang), jnp.sin(ang)], axis=1), (1, DSA_HEADS))

    row2 = lambda a: a.reshape(1, -1)
    tok = lambda width: pl.BlockSpec((1, TM, width), lambda b, t: (b, t, 0))
    hm = lambda width: pl.BlockSpec((1, DSA_HEADS, TM, width), lambda b, t: (b, 0, t, 0))

    out_a, gate_b, qidx, widx, qcat, kidx, kcat, vt = pl.pallas_call(
        _proj_kernel,
        grid=(B, S // TM),
        in_specs=[tok(D), _full((1, D)), _full((D, _C_END)), _full((1, GMLP_WIDTH)), _full((1, GMLP_WIDTH)),
                  _full((GMLP_GROUPS, GMLP_BLOCK, GMLP_BLOCK)), _full((GMLP_GROUPS, GMLP_BLOCK, GMLP_GROUP_DIM)),
                  _full((1, DSA_Q_RANK)), _full((1, DSA_KV_RANK)), _full((DSA_Q_RANK, _Q_END)),
                  _full((DSA_HEADS, LANES, DSA_KV_RANK)),
                  pl.BlockSpec((TM, 2 * LANES), lambda b, t: (t, 0)),
                  pl.BlockSpec((TM, 2 * LANES), lambda b, t: (t, 0))],
        out_specs=[tok(GMLP_WIDTH), tok(DSA_WIDTH), hm(LANES),
                   pl.BlockSpec((1, LANES, TM), lambda b, t: (b, 0, t)),
                   pl.BlockSpec((1, TM // TQ, DSA_HEADS * TQ, Q_CAT), lambda b, t: (b, t, 0, 0)),
                   tok(LANES), tok(Q_CAT),
                   pl.BlockSpec((1, TM // KB, V_ROWS, KB), lambda b, t: (b, t, 0, 0))],
        out_shape=[jax.ShapeDtypeStruct((B, S, GMLP_WIDTH), bf16),
                   jax.ShapeDtypeStruct((B, S, DSA_WIDTH), f32),
                   jax.ShapeDtypeStruct((B, DSA_HEADS, S, LANES), bf16),
                   jax.ShapeDtypeStruct((B, LANES, S), f32),
                   jax.ShapeDtypeStruct((B, S // TQ, DSA_HEADS * TQ, Q_CAT), bf16),
                   jax.ShapeDtypeStruct((B, S, LANES), bf16),
                   jax.ShapeDtypeStruct((B, S, Q_CAT), bf16),
                   jax.ShapeDtypeStruct((B, S // KB, V_ROWS, KB), bf16)],
        compiler_params=pltpu.CompilerParams(
            dimension_semantics=("arbitrary", "arbitrary"), vmem_limit_bytes=48 * 1024 * 1024),
        name="proj_gmlp_dsa_prep",
    )(x, row2(pre_g), w_all, row2(ln_g), row2(ln_b), w_s, b_s_b, row2(qn_g), row2(kvn_g), w_q, w_uk_t,
      cos_t, sin_t)

    qt = lambda width: pl.BlockSpec((1, DSA_HEADS, TQ, width), lambda b, t: (b, 0, t, 0))
    seq = lambda width: pl.BlockSpec((1, S, width), lambda b, t: (b, 0, 0), pipeline_mode=pl.Buffered(1))
    o_lat = pl.pallas_call(
        functools.partial(_dsa_kernel, top_k=top_k),
        grid=(B, S // TQ),
        in_specs=[qt(LANES), pl.BlockSpec((1, LANES, TQ), lambda b, t: (b, 0, t)),
                  pl.BlockSpec((1, 1, DSA_HEADS * TQ, Q_CAT), lambda b, t: (b, t, 0, 0)),
                  seq(LANES), seq(Q_CAT),
                  pl.BlockSpec((1, S // KB, V_ROWS, KB), lambda b, t: (b, 0, 0, 0), pipeline_mode=pl.Buffered(1))],
        out_specs=pl.BlockSpec((1, TQ, DSA_HEADS * DSA_KV_RANK), lambda b, t: (b, t, 0)),
        out_shape=jax.ShapeDtypeStruct((B, S, DSA_HEADS * DSA_KV_RANK), bf16),
        scratch_shapes=[pltpu.VMEM((S // KB, KB, TQ), jnp.int32),
                        pltpu.VMEM((N_GROUPS, TQ), f32),
                        pltpu.VMEM((KB, TQ), f32),
                        pltpu.VMEM((DSA_HEADS, TQ), f32),
                        pltpu.VMEM((DSA_HEADS, V_ROWS, TQ), f32),
                        pltpu.VMEM((8, LANES), f32)],
        compiler_params=pltpu.CompilerParams(
            dimension_semantics=("arbitrary", "arbitrary"), vmem_limit_bytes=56 * 1024 * 1024),
        name="dsa_index_select_attend",
    )(qidx, widx, qcat, kidx, kcat, vt)

    return pl.pallas_call(
        _out_kernel,
        grid=(B, S // TM),
        in_specs=[tok(D), tok(GMLP_WIDTH), tok(DSA_HEADS * DSA_KV_RANK), tok(DSA_WIDTH),
                  _full((DSA_HEADS * DSA_KV_RANK, DSA_WIDTH)), _full((GMLP_WIDTH + DSA_WIDTH, D)), _full((1, D))],
        out_specs=tok(D),
        out_shape=jax.ShapeDtypeStruct((B, S, D), f32),
        compiler_params=pltpu.CompilerParams(
            dimension_semantics=("arbitrary", "arbitrary"), vmem_limit_bytes=32 * 1024 * 1024),
        name="out_proj_norm_residual",
    )(x, out_a, o_lat, gate_b, w_uv_bd, w_out.astype(bf16), row2(post_g))


def kernel(x, w_in, pre_norm_g, post_norm_g, gmlp_ln_g, gmlp_ln_b, gmlp_w_s, gmlp_b_s, dsa_q_norm_g, dsa_kv_norm_g, dsa_w_uq, dsa_w_uk, dsa_w_uv, dsa_w_q_idx, w_out):
    for l in range(w_in.shape[0]):
        x = _layer(x, w_in[l], pre_norm_g[l], post_norm_g[l], gmlp_ln_g[l], gmlp_ln_b[l], gmlp_w_s[l],
                   gmlp_b_s[l], dsa_q_norm_g[l], dsa_kv_norm_g[l], dsa_w_uq[l], dsa_w_uk[l], dsa_w_uv[l],
                   dsa_w_q_idx[l], w_out[l])
    return x
```

```python
import functools
import math

import jax
import jax.numpy as jnp
import numpy as np
from jax import lax
from jax.experimental import pallas as pl
from jax.experimental.pallas import tpu as pltpu

D_MODEL = 1024
CHUNK = 64
EPS = 1e-6
GMLP_GROUPS = 4
GMLP_GROUP_DIM = 128
GMLP_WIDTH = GMLP_GROUPS * GMLP_GROUP_DIM
GMLP_BLOCK = 128
DSA_HEADS = 8
DSA_V_DIM = 64
DSA_WIDTH = DSA_HEADS * DSA_V_DIM
DSA_NOPE_DIM = 64
DSA_ROPE_DIM = 32
DSA_Q_RANK = 256
DSA_KV_RANK = 128
IDX_HEADS = 8
IDX_DIM = 64
TOPK_MAX = 256
ROPE_THETA = 10000.0

LANES = 128
ROPE_HALF = DSA_ROPE_DIM // 2
ROPE_LANE_GROUPS = LANES // DSA_ROPE_DIM
Q_CAT = DSA_KV_RANK + LANES
W_IDX_LANE = IDX_DIM

TM = 512
TQ = 512
KB = 512
KEY_TILE = 128
Q_TILE = 256
COUNT_ROWS = 32
N_GROUPS = 256
NEG_BIG = -1e30
ROW_SUM_MIN = 1e-26
BOUND_SLACK = 1.02
KEY_NONE = -2 ** 31
FIRST_PIVOTS = (1.0, 0.0)
MAX_BISECT = 40

_C_UV, _C_ZA, _C_CQ, _C_CKV = 0, 1024, 1536, 1792
_C_KR, _C_KRS, _C_SLAB, _C_ZB, _C_END = 1920, 2048, 2176, 2304, 2816
_Q_NOPE, _Q_ROPE, _Q_ROPES, _Q_IDX, _Q_END = 0, 1024, 1280, 1536, 2560

_NT = (((1,), (1,)), ((), ()))


def _bf16(a):
    return a.astype(jnp.bfloat16)


def _dot(a, b):
    return jnp.dot(a, b, preferred_element_type=jnp.float32)


def _dot_nt(a, b):
    return lax.dot_general(a, b, _NT, preferred_element_type=jnp.float32)


def _chunk_of(pos):
    return lax.shift_right_logical(pos, jnp.int32(CHUNK.bit_length() - 1))


def _sort_key(s):
    bits = pltpu.bitcast(s, jnp.int32)
    return jnp.where(bits < 0, bits ^ jnp.int32(0x7FFFFFFF), bits)


def _silu(z):
    return z / (1.0 + jnp.exp(-z))


def _gelu_exact(a):
    return 0.5 * a * (1.0 + lax.erf(a * np.float32(math.sqrt(0.5))))


def _proj_kernel(x_ref, pre_g_ref, w_all_ref, ln_g_ref, ln_b_ref, w_s_ref, b_s_ref,
                 qn_g_ref, kvn_g_ref, w_q_ref, w_uk_ref, cos_ref, sin_ref,
                 out_a_ref, gate_b_ref, qidx_ref, widx_ref, qcat_ref, kidx_ref, kcat_ref, vcat_ref):
    x = x_ref[0]
    h = x * lax.rsqrt(jnp.mean(x * x, axis=-1, keepdims=True) + EPS) * pre_g_ref[...]
    proj = _dot(_bf16(h), w_all_ref[...])

    uv = _gelu_exact(proj[:, _C_UV:_C_ZA])
    u, v = uv[:, :GMLP_WIDTH], uv[:, GMLP_WIDTH:]
    mu = jnp.mean(v, axis=-1, keepdims=True)
    vc = v - mu
    var = jnp.mean(vc * vc, axis=-1, keepdims=True)
    vn = _bf16(vc * lax.rsqrt(var + EPS) * ln_g_ref[...] + ln_b_ref[...])
    gate_a = _silu(proj[:, _C_ZA:_C_CQ])
    t_chunk = _chunk_of(lax.broadcasted_iota(jnp.int32, (GMLP_BLOCK, GMLP_BLOCK), 0))
    s_chunk = _chunk_of(lax.broadcasted_iota(jnp.int32, (GMLP_BLOCK, GMLP_BLOCK), 1))
    for g in range(GMLP_GROUPS):
        w_g = _bf16(jnp.where(s_chunk <= t_chunk, w_s_ref[g], 0.0))
        cols = slice(g * GMLP_GROUP_DIM, (g + 1) * GMLP_GROUP_DIM)
        for r in range(TM // GMLP_BLOCK):
            rows = slice(r * GMLP_BLOCK, (r + 1) * GMLP_BLOCK)
            y = _dot(w_g, vn[rows, cols]) + b_s_ref[g]
            out_a_ref[0, rows, cols] = _bf16(u[rows, cols] * y * gate_a[rows, cols])

    gate_b_ref[0] = _silu(proj[:, _C_ZB:_C_END])

    c_q = proj[:, _C_CQ:_C_CKV]
    c_q = c_q * lax.rsqrt(jnp.mean(c_q * c_q, axis=-1, keepdims=True) + EPS) * qn_g_ref[...]
    c_kv = proj[:, _C_CKV:_C_KR]
    c_kv = _bf16(c_kv * lax.rsqrt(jnp.mean(c_kv * c_kv, axis=-1, keepdims=True) + EPS) * kvn_g_ref[...])
    cos = cos_ref[...]
    sin = sin_ref[...]
    k_rope = proj[:, _C_KR:_C_KRS] * cos[:, :LANES] + proj[:, _C_KRS:_C_SLAB] * sin[:, :LANES]
    slab = proj[:, _C_SLAB:_C_ZB]
    kidx_ref[0] = _bf16(slab)
    widx_ref[0] = slab.T
    kcat_ref[0, :, :DSA_KV_RANK] = c_kv
    kcat_ref[0, :, DSA_KV_RANK:] = _bf16(k_rope)
    lane = lax.broadcasted_iota(jnp.int32, (TM, LANES), 1)
    vcat_ref[0, :, :DSA_KV_RANK] = c_kv
    vcat_ref[0, :, DSA_KV_RANK:] = jnp.where(lane == 0, 1.0, 0.0).astype(jnp.bfloat16)

    q_all = _dot(_bf16(c_q), w_q_ref[...])
    q_rope = q_all[:, _Q_ROPE:_Q_ROPES] * cos + q_all[:, _Q_ROPES:_Q_IDX] * sin
    scale = np.float32(1.0 / math.sqrt(DSA_NOPE_DIM + DSA_ROPE_DIM))
    for hd in range(DSA_HEADS):
        cols = slice(_Q_NOPE + hd * LANES, _Q_NOPE + (hd + 1) * LANES)
        q_lat = _dot(_bf16(q_all[:, cols]), w_uk_ref[hd])
        grp, sub = divmod(hd, ROPE_LANE_GROUPS)
        own = (lane >= sub * DSA_ROPE_DIM) & (lane < (sub + 1) * DSA_ROPE_DIM)
        q_r = jnp.where(own, q_rope[:, grp * LANES:(grp + 1) * LANES] * scale, 0.0)
        for t in range(TM // TQ):
            src = slice(t * TQ, (t + 1) * TQ)
            dst = slice(hd * TQ, (hd + 1) * TQ)
            qcat_ref[0, t, dst, :DSA_KV_RANK] = _bf16(q_lat[src] * scale)
            qcat_ref[0, t, dst, DSA_KV_RANK:] = _bf16(q_r[src])
        icol = slice(_Q_IDX + hd * LANES, _Q_IDX + (hd + 1) * LANES)
        qidx_ref[0, hd] = _bf16(q_all[:, icol])


def _dsa_kernel(qidx_ref, widx_ref, qcat_ref, kidx_ref, kcat_ref, vcat_ref, o_ref,
                sc_ref, gmax_ref, bias_ref, m_ref, acc_ref, kmax_ref, *, top_k):
    i = pl.program_id(1)
    n_blocks = (i * TQ) // KB + 1
    idx_scale = np.float32(IDX_HEADS ** -0.5 * IDX_DIM ** -0.5)
    f32 = jnp.float32

    q_chunk = _chunk_of(i * TQ + lax.broadcasted_iota(jnp.int32, (1, TQ), 1))
    key_t = lax.broadcasted_iota(jnp.int32, (KEY_TILE, Q_TILE), 0)

    def over_keys(op, a):
        return op(a, axis=0, keepdims=True)

    w = widx_ref[0] * idx_scale
    gmax_ref[...] = jnp.full(gmax_ref.shape, -jnp.inf, f32)

    def score_block(kb, carry):
        for ks in range(KB // KEY_TILE):
            k = kidx_ref[0, pl.ds(pl.multiple_of(kb * KB + ks * KEY_TILE, KEY_TILE), KEY_TILE), :]
            krows = slice(ks * KEY_TILE, (ks + 1) * KEY_TILE)
            grows = slice((ks * KEY_TILE) % N_GROUPS, (ks * KEY_TILE) % N_GROUPS + KEY_TILE)
            for qt in range(TQ // Q_TILE):
                qs = slice(qt * Q_TILE, (qt + 1) * Q_TILE)
                acc = jnp.zeros((KEY_TILE, Q_TILE), f32)
                for hd in range(IDX_HEADS):
                    logit = _dot_nt(k, qidx_ref[0, hd, qs, :])
                    acc = acc + w[W_IDX_LANE + hd:W_IDX_LANE + hd + 1, qs] * jnp.maximum(logit, 0.0)
                adm = _chunk_of(kb * KB + ks * KEY_TILE + key_t) <= q_chunk[:, qs]
                sc_ref[kb, krows, qs] = jnp.where(adm, _sort_key(acc), KEY_NONE)
                gmax_ref[grows, qs] = jnp.maximum(gmax_ref[grows, qs], jnp.where(adm, acc, -jnp.inf))
        return carry
    lax.fori_loop(0, n_blocks, score_block, 0)

    kf = np.float32(top_k)

    def count_ge(mid):
        def body(kb, cnt):
            for r in range(KB // COUNT_ROWS):
                cnt = cnt + jnp.where(sc_ref[kb, r * COUNT_ROWS:(r + 1) * COUNT_ROWS, :] >= mid, 1.0, 0.0)
            return cnt
        return over_keys(jnp.sum, lax.fori_loop(0, n_blocks, body, jnp.zeros((COUNT_ROWS, TQ), f32)))

    def pivot(lo, hi):
        return (lo & hi) + lax.shift_right_arithmetic(lo ^ hi, jnp.int32(1))

    def pivot_zero_first(lo, hi):
        return jnp.where((lo <= 0) & (hi > 1), 1, jnp.where((lo < 0) & (hi == 1), 0, pivot(lo, hi)))

    grp = gmax_ref[...]
    grp_min = over_keys(jnp.min, grp)
    select_all = (q_chunk + 1) * CHUNK <= top_k
    lo = jnp.where(select_all, KEY_NONE + 1, _sort_key(grp_min))
    hi = jnp.where(select_all, KEY_NONE + 1, _sort_key(over_keys(jnp.max, grp)) + 1)

    def step(lo, hi, mid):
        cnt = count_ge(mid)
        ge = cnt >= kf
        return jnp.where(ge, mid, lo), jnp.where(cnt == kf, mid, jnp.where(ge, hi, mid))

    grp_mean = over_keys(jnp.sum, grp) / np.float32(N_GROUPS)
    grp_var = over_keys(jnp.sum, grp * grp) / np.float32(N_GROUPS) - grp_mean * grp_mean
    grp_std = jnp.sqrt(jnp.maximum(grp_var, 0.0))
    for below in FIRST_PIVOTS:
        guess = _sort_key(grp_mean - np.float32(below) * grp_std)
        usable = (grp_min > -jnp.inf) & (guess > lo) & (guess < hi)
        lo, hi = step(lo, hi, jnp.where(usable, guess, pivot(lo, hi)))
    for _ in range(2):
        lo, hi = step(lo, hi, pivot_zero_first(lo, hi))

    def bisect_cond(state):
        it, lo, _, mid = state
        return jnp.logical_and(it < MAX_BISECT, jnp.max(jnp.where(mid != lo, 1.0, 0.0)) > 0.0)

    def bisect_body(state):
        it, lo, hi, mid = state
        lo, hi = step(lo, hi, mid)
        return it + 1, lo, hi, pivot(lo, hi)

    _, thr, hi, _ = lax.while_loop(bisect_cond, bisect_body, (jnp.int32(0), lo, hi, pivot(lo, hi)))
    tie = hi != thr

    @pl.when(jnp.max(jnp.where(tie, 1.0, 0.0)) > 0.0)
    def _():
        need = kf - count_ge(thr + 1)
        upto = (lax.broadcasted_iota(jnp.int32, (KB, KB), 1)
                <= lax.broadcasted_iota(jnp.int32, (KB, KB), 0)).astype(jnp.bfloat16)

        def drop_body(kb, seen):
            added = []
            for qt in range(TQ // Q_TILE):
                qs = slice(qt * Q_TILE, (qt + 1) * Q_TILE)
                keys = sc_ref[kb, :, qs]
                band = tie[:, qs] & (keys == thr[:, qs])
                band_f = jnp.where(band, 1.0, 0.0)
                rank = _dot(upto, _bf16(band_f))
                late = band & (rank + seen[:, qs] > need[:, qs])
                sc_ref[kb, :, qs] = jnp.where(late, KEY_NONE, keys)
                added.append(over_keys(jnp.sum, band_f))
            return seen + jnp.concatenate(added, axis=1)
        lax.fori_loop(0, n_blocks, drop_body, jnp.zeros((1, TQ), f32))

    @pl.when(i == 0)
    def _():
        def norm_block(kb, run):
            kc = kcat_ref[0, pl.ds(pl.multiple_of(kb * KB, KB), KB), :].astype(f32)
            return jnp.maximum(run, jnp.sum(kc * kc, axis=1, keepdims=True))
        run = lax.fori_loop(0, kcat_ref.shape[1] // KB, norm_block, jnp.zeros((KB, 1), f32))
        kmax_ref[...] = jnp.broadcast_to(jnp.max(run, axis=0, keepdims=True), kmax_ref.shape)

    k_norm2 = kmax_ref[0:1, :]
    q_all = qcat_ref[0, 0]
    q_norm2 = _dot(q_all * q_all, jnp.ones((Q_CAT, LANES), jnp.bfloat16))
    qk2 = jnp.maximum(q_norm2 * k_norm2, 1e-30)
    m_ref[...] = qk2 * lax.rsqrt(qk2) * BOUND_SLACK

    chunks = [slice(c * LANES, (c + 1) * LANES) for c in range(KB // LANES)]
    head_rows = [slice(hd * TQ, (hd + 1) * TQ) for hd in range(DSA_HEADS)]

    def set_bias(kb):
        for qt in range(TQ // LANES):
            qs = slice(qt * LANES, (qt + 1) * LANES)
            bias_ref[qs, :] = jnp.where(sc_ref[kb, :, qs] >= thr[:, qs], 0.0, NEG_BIG).T

    def attend():
        acc_ref[...] = jnp.zeros(acc_ref.shape, f32)

        def attn_block(kb, carry):
            rows = pl.ds(pl.multiple_of(kb * KB, KB), KB)
            kc = kcat_ref[0, rows, :]
            vc = vcat_ref[0, rows, :]
            set_bias(kb)
            for hrows in head_rows:
                s = _dot_nt(qcat_ref[0, 0, hrows, :], kc)
                m = m_ref[hrows, :]
                p = [_bf16(jnp.exp(s[:, cols] + bias_ref[:, cols] - m)) for cols in chunks]
                acc_ref[hrows, :] += _dot(jnp.concatenate(p, axis=1), vc)
            return carry
        lax.fori_loop(0, n_blocks, attn_block, 0)
        return jnp.min(acc_ref[:, DSA_KV_RANK:DSA_KV_RANK + 1])

    @pl.when(attend() < ROW_SUM_MIN)
    def _():
        m_ref[...] = jnp.full(m_ref.shape, NEG_BIG, f32)

        def max_block(kb, carry):
            kc = kcat_ref[0, pl.ds(pl.multiple_of(kb * KB, KB), KB), :]
            set_bias(kb)
            for hrows in head_rows:
                s = _dot_nt(qcat_ref[0, 0, hrows, :], kc)
                sm = [s[:, cols] + bias_ref[:, cols] for cols in chunks]
                m_ref[hrows, :] = jnp.maximum(m_ref[hrows, :], functools.reduce(jnp.maximum, sm))
            return carry
        lax.fori_loop(0, n_blocks, max_block, 0)
        m_all = m_ref[...]
        m_ref[...] = jnp.broadcast_to(jnp.max(m_all, axis=1, keepdims=True), m_all.shape)
        attend()

    for hd, hrows in enumerate(head_rows):
        a = acc_ref[hrows, :]
        denom = jnp.broadcast_to(a[:, DSA_KV_RANK:DSA_KV_RANK + 1], (TQ, DSA_KV_RANK))
        o_ref[0, :, hd * DSA_KV_RANK:(hd + 1) * DSA_KV_RANK] = _bf16(a[:, :DSA_KV_RANK] / denom)


def _out_kernel(x_ref, out_a_ref, o_lat_ref, gate_b_ref, w_uv_ref, w_out_ref, post_g_ref, y_ref):
    o = _dot(o_lat_ref[0], w_uv_ref[...])
    out_b = _bf16(o * gate_b_ref[0])
    y = _dot(out_a_ref[0], w_out_ref[:GMLP_WIDTH, :]) + _dot(out_b, w_out_ref[GMLP_WIDTH:, :])
    y = y * lax.rsqrt(jnp.mean(y * y, axis=-1, keepdims=True) + EPS) * post_g_ref[...]
    y_ref[0] = x_ref[0] + y


def _full(shape):
    return pl.BlockSpec(shape, lambda b, t: (0,) * len(shape))


def _layer(x, w_in, pre_g, post_g, ln_g, ln_b, w_s, b_s, qn_g, kvn_g, w_uq, w_uk, w_uv, w_q_idx, w_out):
    B, S, D = x.shape
    assert D == D_MODEL and S % KB == 0 and S % TM == 0 and TM % TQ == 0 and KB % TQ == 0
    top_k = min(TOPK_MAX, S // 4)
    assert top_k <= N_GROUPS <= KB and KB % N_GROUPS == 0 and N_GROUPS % KEY_TILE == 0
    f32, bf16 = jnp.float32, jnp.bfloat16

    kr = w_in[:, 1920:1952]
    kr_sw = jnp.concatenate([kr[:, ROPE_HALF:], kr[:, :ROPE_HALF]], axis=1)
    w_all = jnp.concatenate([
        w_in[:, :1920], jnp.tile(kr, (1, ROPE_LANE_GROUPS)), jnp.tile(kr_sw, (1, ROPE_LANE_GROUPS)),
        w_in[:, 1952:2024], jnp.zeros((D, LANES - IDX_DIM - IDX_HEADS), f32), w_in[:, 2024:]], axis=1).astype(bf16)
    wq3 = w_uq.reshape(DSA_Q_RANK, DSA_HEADS, DSA_NOPE_DIM + DSA_ROPE_DIM)
    nope = jnp.pad(wq3[:, :, :DSA_NOPE_DIM], ((0, 0), (0, 0), (0, LANES - DSA_NOPE_DIM)))
    rope = wq3[:, :, DSA_NOPE_DIM:]
    rope_sw = jnp.concatenate([rope[:, :, ROPE_HALF:], rope[:, :, :ROPE_HALF]], axis=2)
    wqi = jnp.pad(w_q_idx.reshape(DSA_Q_RANK, IDX_HEADS, IDX_DIM), ((0, 0), (0, 0), (0, LANES - IDX_DIM)))
    w_q = jnp.concatenate([nope.reshape(DSA_Q_RANK, -1), rope.reshape(DSA_Q_RANK, -1),
                           rope_sw.reshape(DSA_Q_RANK, -1), wqi.reshape(DSA_Q_RANK, -1)], axis=1).astype(bf16)
    w_uk_t = jnp.pad(jnp.transpose(w_uk, (1, 2, 0)), ((0, 0), (0, LANES - DSA_NOPE_DIM), (0, 0))).astype(bf16)
    eye = jnp.eye(DSA_HEADS, dtype=f32)
    w_uv_bd = (jnp.transpose(w_uv, (1, 0, 2))[:, :, None, :] * eye[:, None, :, None]).reshape(
        DSA_HEADS * DSA_KV_RANK, DSA_WIDTH).astype(bf16)
    b_s_b = jnp.broadcast_to(b_s[:, :, None], (GMLP_GROUPS, GMLP_BLOCK, GMLP_GROUP_DIM))

    pos = jnp.arange(S, dtype=f32)
    inv_freq = ROPE_THETA ** (-jnp.arange(0, DSA_ROPE_DIM, 2, dtype=f32) / DSA_ROPE_DIM)
    ang = pos[:, None] * inv_freq[None, :]
    cos_t = jnp.tile(jnp.concatenate([jnp.cos(ang), jnp.cos(ang)], axis=1), (1, DSA_HEADS))
    sin_t = jnp.tile(jnp.concatenate([-jnp.sin(ang), jnp.sin(ang)], axis=1), (1, DSA_HEADS))

    row2 = lambda a: a.reshape(1, -1)
    tok = lambda width: pl.BlockSpec((1, TM, width), lambda b, t: (b, t, 0))
    hm = lambda width: pl.BlockSpec((1, DSA_HEADS, TM, width), lambda b, t: (b, 0, t, 0))

    out_a, gate_b, qidx, widx, qcat, kidx, kcat, vcat = pl.pallas_call(
        _proj_kernel,
        grid=(B, S // TM),
        in_specs=[tok(D), _full((1, D)), _full((D, _C_END)), _full((1, GMLP_WIDTH)), _full((1, GMLP_WIDTH)),
                  _full((GMLP_GROUPS, GMLP_BLOCK, GMLP_BLOCK)), _full((GMLP_GROUPS, GMLP_BLOCK, GMLP_GROUP_DIM)),
                  _full((1, DSA_Q_RANK)), _full((1, DSA_KV_RANK)), _full((DSA_Q_RANK, _Q_END)),
                  _full((DSA_HEADS, LANES, DSA_KV_RANK)),
                  pl.BlockSpec((TM, 2 * LANES), lambda b, t: (t, 0)),
                  pl.BlockSpec((TM, 2 * LANES), lambda b, t: (t, 0))],
        out_specs=[tok(GMLP_WIDTH), tok(DSA_WIDTH), hm(LANES),
                   pl.BlockSpec((1, LANES, TM), lambda b, t: (b, 0, t)),
                   pl.BlockSpec((1, TM // TQ, DSA_HEADS * TQ, Q_CAT), lambda b, t: (b, t, 0, 0)),
                   tok(LANES), tok(Q_CAT), tok(Q_CAT)],
        out_shape=[jax.ShapeDtypeStruct((B, S, GMLP_WIDTH), bf16),
                   jax.ShapeDtypeStruct((B, S, DSA_WIDTH), f32),
                   jax.ShapeDtypeStruct((B, DSA_HEADS, S, LANES), bf16),
                   jax.ShapeDtypeStruct((B, LANES, S), f32),
                   jax.ShapeDtypeStruct((B, S // TQ, DSA_HEADS * TQ, Q_CAT), bf16),
                   jax.ShapeDtypeStruct((B, S, LANES), bf16),
                   jax.ShapeDtypeStruct((B, S, Q_CAT), bf16),
                   jax.ShapeDtypeStruct((B, S, Q_CAT), bf16)],
        compiler_params=pltpu.CompilerParams(
            dimension_semantics=("arbitrary", "arbitrary"), vmem_limit_bytes=48 * 1024 * 1024),
        name="proj_gmlp_dsa_prep",
    )(x, row2(pre_g), w_all, row2(ln_g), row2(ln_b), w_s, b_s_b, row2(qn_g), row2(kvn_g), w_q, w_uk_t,
      cos_t, sin_t)

    qt = lambda width: pl.BlockSpec((1, DSA_HEADS, TQ, width), lambda b, t: (b, 0, t, 0))
    seq = lambda width: pl.BlockSpec((1, S, width), lambda b, t: (b, 0, 0), pipeline_mode=pl.Buffered(1))
    o_lat = pl.pallas_call(
        functools.partial(_dsa_kernel, top_k=top_k),
        grid=(B, S // TQ),
        in_specs=[qt(LANES), pl.BlockSpec((1, LANES, TQ), lambda b, t: (b, 0, t)),
                  pl.BlockSpec((1, 1, DSA_HEADS * TQ, Q_CAT), lambda b, t: (b, t, 0, 0)),
                  seq(LANES), seq(Q_CAT), seq(Q_CAT)],
        out_specs=pl.BlockSpec((1, TQ, DSA_HEADS * DSA_KV_RANK), lambda b, t: (b, t, 0)),
        out_shape=jax.ShapeDtypeStruct((B, S, DSA_HEADS * DSA_KV_RANK), bf16),
        scratch_shapes=[pltpu.VMEM((S // KB, KB, TQ), jnp.int32),
                        pltpu.VMEM((N_GROUPS, TQ), f32),
                        pltpu.VMEM((TQ, KB), f32),
                        pltpu.VMEM((DSA_HEADS * TQ, LANES), f32),
                        pltpu.VMEM((DSA_HEADS * TQ, Q_CAT), f32),
                        pltpu.VMEM((8, LANES), f32)],
        compiler_params=pltpu.CompilerParams(
            dimension_semantics=("arbitrary", "arbitrary"), vmem_limit_bytes=56 * 1024 * 1024),
        name="dsa_index_select_attend",
    )(qidx, widx, qcat, kidx, kcat, vcat)

    return pl.pallas_call(
        _out_kernel,
        grid=(B, S // TM),
        in_specs=[tok(D), tok(GMLP_WIDTH), tok(DSA_HEADS * DSA_KV_RANK), tok(DSA_WIDTH),
                  _full((DSA_HEADS * DSA_KV_RANK, DSA_WIDTH)), _full((GMLP_WIDTH + DSA_WIDTH, D)), _full((1, D))],
        out_specs=tok(D),
        out_shape=jax.ShapeDtypeStruct((B, S, D), f32),
        compiler_params=pltpu.CompilerParams(
            dimension_semantics=("arbitrary", "arbitrary"), vmem_limit_bytes=32 * 1024 * 1024),
        name="out_proj_norm_residual",
    )(x, out_a, o_lat, gate_b, w_uv_bd, w_out.astype(bf16), row2(post_g))


def kernel(x, w_in, pre_norm_g, post_norm_g, gmlp_ln_g, gmlp_ln_b, gmlp_w_s, gmlp_b_s, dsa_q_norm_g, dsa_kv_norm_g, dsa_w_uq, dsa_w_uk, dsa_w_uv, dsa_w_q_idx, w_out):
    for l in range(w_in.shape[0]):
        x = _layer(x, w_in[l], pre_norm_g[l], post_norm_g[l], gmlp_ln_g[l], gmlp_ln_b[l], gmlp_w_s[l],
                   gmlp_b_s[l], dsa_q_norm_g[l], dsa_kv_norm_g[l], dsa_w_uq[l], dsa_w_uk[l], dsa_w_uv[l],
                   dsa_w_q_idx[l], w_out[l])
    return x
```

```python
import functools
import math

import jax
import jax.numpy as jnp
import numpy as np
from jax import lax
from jax.experimental import pallas as pl
from jax.experimental.pallas import tpu as pltpu

D_MODEL = 1024
CHUNK = 64
EPS = 1e-6
GMLP_GROUPS = 4
GMLP_GROUP_DIM = 128
GMLP_WIDTH = GMLP_GROUPS * GMLP_GROUP_DIM
GMLP_BLOCK = 128
DSA_HEADS = 8
DSA_V_DIM = 64
DSA_WIDTH = DSA_HEADS * DSA_V_DIM
DSA_NOPE_DIM = 64
DSA_ROPE_DIM = 32
DSA_Q_RANK = 256
DSA_KV_RANK = 128
IDX_HEADS = 8
IDX_DIM = 64
TOPK_MAX = 256
ROPE_THETA = 10000.0

LANES = 128
ROPE_HALF = DSA_ROPE_DIM // 2
ROPE_LANE_GROUPS = LANES // DSA_ROPE_DIM
Q_CAT = DSA_KV_RANK + LANES
W_IDX_LANE = IDX_DIM

TM = 512
TQ = 512
KB = 512
KEY_TILE = 128
Q_TILE = 256
COUNT_ROWS = 32
N_GROUPS = 256
NEG_BIG = -1e30
ROW_SUM_MIN = 1e-26
BOUND_SLACK = 1.02
KEY_TINY = 0x00800000
KEY_LOWEST = -0x7F800000
FIRST_PIVOTS = (1.0, 0.0)
MAX_BISECT = 40

_C_UV, _C_ZA, _C_CQ, _C_CKV = 0, 1024, 1536, 1792
_C_KR, _C_KRS, _C_SLAB, _C_ZB, _C_END = 1920, 2048, 2176, 2304, 2816
_Q_NOPE, _Q_ROPE, _Q_ROPES, _Q_IDX, _Q_END = 0, 1024, 1280, 1536, 2560

_NT = (((1,), (1,)), ((), ()))


def _bf16(a):
    return a.astype(jnp.bfloat16)


def _dot(a, b):
    return jnp.dot(a, b, preferred_element_type=jnp.float32)


def _dot_nt(a, b):
    return lax.dot_general(a, b, _NT, preferred_element_type=jnp.float32)


def _chunk_of(pos):
    return lax.shift_right_logical(pos, jnp.int32(CHUNK.bit_length() - 1))


def _sort_key(s):
    bits = pltpu.bitcast(s, jnp.int32)
    return jnp.where(bits < 0, bits ^ jnp.int32(0x7FFFFFFF), bits)


def _key_to_float(key):
    return pltpu.bitcast(jnp.where(key < 0, key ^ jnp.int32(0x7FFFFFFF), key), jnp.float32)


def _silu(z):
    return z / (1.0 + jnp.exp(-z))


def _gelu_exact(a):
    return 0.5 * a * (1.0 + lax.erf(a * np.float32(math.sqrt(0.5))))


def _proj_kernel(x_ref, pre_g_ref, w_all_ref, ln_g_ref, ln_b_ref, w_s_ref, b_s_ref,
                 qn_g_ref, kvn_g_ref, w_q_ref, w_uk_ref, cos_ref, sin_ref,
                 out_a_ref, gate_b_ref, qidx_ref, widx_ref, qcat_ref, kidx_ref, kcat_ref, vcat_ref):
    x = x_ref[0]
    h = x * lax.rsqrt(jnp.mean(x * x, axis=-1, keepdims=True) + EPS) * pre_g_ref[...]
    proj = _dot(_bf16(h), w_all_ref[...])

    uv = _gelu_exact(proj[:, _C_UV:_C_ZA])
    u, v = uv[:, :GMLP_WIDTH], uv[:, GMLP_WIDTH:]
    mu = jnp.mean(v, axis=-1, keepdims=True)
    vc = v - mu
    var = jnp.mean(vc * vc, axis=-1, keepdims=True)
    vn = _bf16(vc * lax.rsqrt(var + EPS) * ln_g_ref[...] + ln_b_ref[...])
    gate_a = _silu(proj[:, _C_ZA:_C_CQ])
    t_chunk = _chunk_of(lax.broadcasted_iota(jnp.int32, (GMLP_BLOCK, GMLP_BLOCK), 0))
    s_chunk = _chunk_of(lax.broadcasted_iota(jnp.int32, (GMLP_BLOCK, GMLP_BLOCK), 1))
    for g in range(GMLP_GROUPS):
        w_g = _bf16(jnp.where(s_chunk <= t_chunk, w_s_ref[g], 0.0))
        cols = slice(g * GMLP_GROUP_DIM, (g + 1) * GMLP_GROUP_DIM)
        for r in range(TM // GMLP_BLOCK):
            rows = slice(r * GMLP_BLOCK, (r + 1) * GMLP_BLOCK)
            y = _dot(w_g, vn[rows, cols]) + b_s_ref[g]
            out_a_ref[0, rows, cols] = _bf16(u[rows, cols] * y * gate_a[rows, cols])

    gate_b_ref[0] = _silu(proj[:, _C_ZB:_C_END])

    c_q = proj[:, _C_CQ:_C_CKV]
    c_q = c_q * lax.rsqrt(jnp.mean(c_q * c_q, axis=-1, keepdims=True) + EPS) * qn_g_ref[...]
    c_kv = proj[:, _C_CKV:_C_KR]
    c_kv = _bf16(c_kv * lax.rsqrt(jnp.mean(c_kv * c_kv, axis=-1, keepdims=True) + EPS) * kvn_g_ref[...])
    cos = cos_ref[...]
    sin = sin_ref[...]
    k_rope = proj[:, _C_KR:_C_KRS] * cos[:, :LANES] + proj[:, _C_KRS:_C_SLAB] * sin[:, :LANES]
    slab = proj[:, _C_SLAB:_C_ZB]
    kidx_ref[0] = _bf16(slab)
    widx_ref[0] = slab.T
    kcat_ref[0, :, :DSA_KV_RANK] = c_kv
    kcat_ref[0, :, DSA_KV_RANK:] = _bf16(k_rope)
    lane = lax.broadcasted_iota(jnp.int32, (TM, LANES), 1)
    vcat_ref[0, :, :DSA_KV_RANK] = c_kv
    vcat_ref[0, :, DSA_KV_RANK:] = jnp.where(lane == 0, 1.0, 0.0).astype(jnp.bfloat16)

    q_all = _dot(_bf16(c_q), w_q_ref[...])
    q_rope = q_all[:, _Q_ROPE:_Q_ROPES] * cos + q_all[:, _Q_ROPES:_Q_IDX] * sin
    scale = np.float32(1.0 / math.sqrt(DSA_NOPE_DIM + DSA_ROPE_DIM))
    for hd in range(DSA_HEADS):
        cols = slice(_Q_NOPE + hd * LANES, _Q_NOPE + (hd + 1) * LANES)
        q_lat = _dot(_bf16(q_all[:, cols]), w_uk_ref[hd])
        grp, sub = divmod(hd, ROPE_LANE_GROUPS)
        own = (lane >= sub * DSA_ROPE_DIM) & (lane < (sub + 1) * DSA_ROPE_DIM)
        q_r = jnp.where(own, q_rope[:, grp * LANES:(grp + 1) * LANES] * scale, 0.0)
        for t in range(TM // TQ):
            src = slice(t * TQ, (t + 1) * TQ)
            dst = slice(hd * TQ, (hd + 1) * TQ)
            qcat_ref[0, t, dst, :DSA_KV_RANK] = _bf16(q_lat[src] * scale)
            qcat_ref[0, t, dst, DSA_KV_RANK:] = _bf16(q_r[src])
        icol = slice(_Q_IDX + hd * LANES, _Q_IDX + (hd + 1) * LANES)
        qidx_ref[0, hd] = _bf16(q_all[:, icol])


def _dsa_kernel(qidx_ref, widx_ref, qcat_ref, kidx_ref, kcat_ref, vcat_ref, o_ref,
                sc_ref, gmax_ref, bias_ref, m_ref, acc_ref, kmax_ref, *, top_k):
    i = pl.program_id(1)
    n_blocks = (i * TQ) // KB + 1
    idx_scale = np.float32(IDX_HEADS ** -0.5 * IDX_DIM ** -0.5)
    f32 = jnp.float32

    q_chunk = _chunk_of(i * TQ + lax.broadcasted_iota(jnp.int32, (1, TQ), 1))
    key_t = lax.broadcasted_iota(jnp.int32, (KEY_TILE, Q_TILE), 0)

    def over_keys(op, a):
        return op(a, axis=0, keepdims=True)

    w = widx_ref[0] * idx_scale
    gmax_ref[...] = jnp.full(gmax_ref.shape, -jnp.inf, f32)

    def score_block(kb, carry):
        for ks in range(KB // KEY_TILE):
            k = kidx_ref[0, pl.ds(pl.multiple_of(kb * KB + ks * KEY_TILE, KEY_TILE), KEY_TILE), :]
            krows = slice(ks * KEY_TILE, (ks + 1) * KEY_TILE)
            grows = slice((ks * KEY_TILE) % N_GROUPS, (ks * KEY_TILE) % N_GROUPS + KEY_TILE)
            for qt in range(TQ // Q_TILE):
                qs = slice(qt * Q_TILE, (qt + 1) * Q_TILE)
                acc = jnp.zeros((KEY_TILE, Q_TILE), f32)
                for hd in range(IDX_HEADS):
                    logit = _dot_nt(k, qidx_ref[0, hd, qs, :])
                    acc = acc + w[W_IDX_LANE + hd:W_IDX_LANE + hd + 1, qs] * jnp.maximum(logit, 0.0)
                adm = _chunk_of(kb * KB + ks * KEY_TILE + key_t) <= q_chunk[:, qs]
                sc_ref[kb, krows, qs] = jnp.where(adm, acc, -jnp.inf)
                gmax_ref[grows, qs] = jnp.maximum(gmax_ref[grows, qs], jnp.where(adm, acc, -jnp.inf))
        return carry
    lax.fori_loop(0, n_blocks, score_block, 0)

    kf = np.float32(top_k)

    def count_ge(mid):
        mid_f = _key_to_float(mid)

        def body(kb, cnt):
            for r in range(KB // COUNT_ROWS):
                cnt = cnt + jnp.where(sc_ref[kb, r * COUNT_ROWS:(r + 1) * COUNT_ROWS, :] >= mid_f, 1.0, 0.0)
            return cnt
        return over_keys(jnp.sum, lax.fori_loop(0, n_blocks, body, jnp.zeros((COUNT_ROWS, TQ), f32)))

    def canonical(key):
        return jnp.where((key > 0) & (key < KEY_TINY), KEY_TINY, jnp.where((key < 0) & (key >= -KEY_TINY), 0, key))

    def pivot(lo, hi):
        mid = canonical((lo & hi) + lax.shift_right_arithmetic(lo ^ hi, jnp.int32(1)))
        return jnp.where((mid <= lo) | (mid >= hi), lo, mid)

    def pivot_zero_first(lo, hi):
        return jnp.where((lo <= 0) & (hi > KEY_TINY), KEY_TINY,
                         jnp.where((lo < 0) & (hi == KEY_TINY), 0, pivot(lo, hi)))

    grp = gmax_ref[...]
    grp_min = over_keys(jnp.min, grp)
    select_all = (q_chunk + 1) * CHUNK <= top_k
    lo = jnp.where(select_all, KEY_LOWEST, canonical(_sort_key(grp_min)))
    hi = jnp.where(select_all, KEY_LOWEST, canonical(_sort_key(over_keys(jnp.max, grp)) + 1))

    def step(lo, hi, mid):
        cnt = count_ge(mid)
        ge = cnt >= kf
        return jnp.where(ge, mid, lo), jnp.where(cnt == kf, mid, jnp.where(ge, hi, mid))

    grp_mean = over_keys(jnp.sum, grp) / np.float32(N_GROUPS)
    grp_var = over_keys(jnp.sum, grp * grp) / np.float32(N_GROUPS) - grp_mean * grp_mean
    grp_std = jnp.sqrt(jnp.maximum(grp_var, 0.0))
    for below in FIRST_PIVOTS:
        guess = canonical(_sort_key(grp_mean - np.float32(below) * grp_std))
        usable = (grp_min > -jnp.inf) & (guess > lo) & (guess < hi)
        lo, hi = step(lo, hi, jnp.where(usable, guess, pivot(lo, hi)))
    for _ in range(2):
        lo, hi = step(lo, hi, pivot_zero_first(lo, hi))

    def bisect_cond(state):
        it, lo, _, mid = state
        return jnp.logical_and(it < MAX_BISECT, jnp.max(jnp.where(mid != lo, 1.0, 0.0)) > 0.0)

    def bisect_body(state):
        it, lo, hi, mid = state
        lo, hi = step(lo, hi, mid)
        return it + 1, lo, hi, pivot(lo, hi)

    _, lo, hi, _ = lax.while_loop(bisect_cond, bisect_body, (jnp.int32(0), lo, hi, pivot(lo, hi)))
    thr = _key_to_float(lo)
    tie = hi != lo

    @pl.when(jnp.max(jnp.where(tie, 1.0, 0.0)) > 0.0)
    def _():
        need = kf - count_ge(hi)
        upto = (lax.broadcasted_iota(jnp.int32, (KB, KB), 1)
                <= lax.broadcasted_iota(jnp.int32, (KB, KB), 0)).astype(jnp.bfloat16)

        def drop_body(kb, seen):
            added = []
            for qt in range(TQ // Q_TILE):
                qs = slice(qt * Q_TILE, (qt + 1) * Q_TILE)
                keys = sc_ref[kb, :, qs]
                band = tie[:, qs] & (keys == thr[:, qs])
                band_f = jnp.where(band, 1.0, 0.0)
                rank = _dot(upto, _bf16(band_f))
                late = band & (rank + seen[:, qs] > need[:, qs])
                sc_ref[kb, :, qs] = jnp.where(late, -jnp.inf, keys)
                added.append(over_keys(jnp.sum, band_f))
            return seen + jnp.concatenate(added, axis=1)
        lax.fori_loop(0, n_blocks, drop_body, jnp.zeros((1, TQ), f32))

    @pl.when(i == 0)
    def _():
        def norm_block(kb, run):
            kc = kcat_ref[0, pl.ds(pl.multiple_of(kb * KB, KB), KB), :].astype(f32)
            return jnp.maximum(run, jnp.sum(kc * kc, axis=1, keepdims=True))
        run = lax.fori_loop(0, kcat_ref.shape[1] // KB, norm_block, jnp.zeros((KB, 1), f32))
        kmax_ref[...] = jnp.broadcast_to(jnp.max(run, axis=0, keepdims=True), kmax_ref.shape)

    k_norm2 = kmax_ref[0:1, :]
    q_all = qcat_ref[0, 0]
    q_norm2 = _dot(q_all * q_all, jnp.ones((Q_CAT, LANES), jnp.bfloat16))
    qk2 = jnp.maximum(q_norm2 * k_norm2, 1e-30)
    m_ref[...] = qk2 * lax.rsqrt(qk2) * BOUND_SLACK

    chunks = [slice(c * LANES, (c + 1) * LANES) for c in range(KB // LANES)]
    head_rows = [slice(hd * TQ, (hd + 1) * TQ) for hd in range(DSA_HEADS)]

    def set_bias(kb):
        for qt in range(TQ // LANES):
            qs = slice(qt * LANES, (qt + 1) * LANES)
            bias_ref[qs, :] = jnp.where(sc_ref[kb, :, qs] >= thr[:, qs], 0.0, NEG_BIG).T

    def attend():
        acc_ref[...] = jnp.zeros(acc_ref.shape, f32)

        def attn_block(kb, carry):
            rows = pl.ds(pl.multiple_of(kb * KB, KB), KB)
            kc = kcat_ref[0, rows, :]
            vc = vcat_ref[0, rows, :]
            set_bias(kb)
            for hrows in head_rows:
                s = _dot_nt(qcat_ref[0, 0, hrows, :], kc)
                m = m_ref[hrows, :]
                p = [_bf16(jnp.exp(s[:, cols] + bias_ref[:, cols] - m)) for cols in chunks]
                acc_ref[hrows, :] += _dot(jnp.concatenate(p, axis=1), vc)
            return carry
        lax.fori_loop(0, n_blocks, attn_block, 0)
        return jnp.min(acc_ref[:, DSA_KV_RANK:DSA_KV_RANK + 1])

    @pl.when(attend() < ROW_SUM_MIN)
    def _():
        m_ref[...] = jnp.full(m_ref.shape, NEG_BIG, f32)

        def max_block(kb, carry):
            kc = kcat_ref[0, pl.ds(pl.multiple_of(kb * KB, KB), KB), :]
            set_bias(kb)
            for hrows in head_rows:
                s = _dot_nt(qcat_ref[0, 0, hrows, :], kc)
                sm = [s[:, cols] + bias_ref[:, cols] for cols in chunks]
                m_ref[hrows, :] = jnp.maximum(m_ref[hrows, :], functools.reduce(jnp.maximum, sm))
            return carry
        lax.fori_loop(0, n_blocks, max_block, 0)
        m_all = m_ref[...]
        m_ref[...] = jnp.broadcast_to(jnp.max(m_all, axis=1, keepdims=True), m_all.shape)
        attend()

    for hd, hrows in enumerate(head_rows):
        a = acc_ref[hrows, :]
        denom = jnp.broadcast_to(a[:, DSA_KV_RANK:DSA_KV_RANK + 1], (TQ, DSA_KV_RANK))
        o_ref[0, :, hd * DSA_KV_RANK:(hd + 1) * DSA_KV_RANK] = _bf16(a[:, :DSA_KV_RANK] / denom)


def _out_kernel(x_ref, out_a_ref, o_lat_ref, gate_b_ref, w_uv_ref, w_out_ref, post_g_ref, y_ref):
    o = _dot(o_lat_ref[0], w_uv_ref[...])
    out_b = _bf16(o * gate_b_ref[0])
    y = _dot(out_a_ref[0], w_out_ref[:GMLP_WIDTH, :]) + _dot(out_b, w_out_ref[GMLP_WIDTH:, :])
    y = y * lax.rsqrt(jnp.mean(y * y, axis=-1, keepdims=True) + EPS) * post_g_ref[...]
    y_ref[0] = x_ref[0] + y


def _full(shape):
    return pl.BlockSpec(shape, lambda b, t: (0,) * len(shape))


def _layer(x, w_in, pre_g, post_g, ln_g, ln_b, w_s, b_s, qn_g, kvn_g, w_uq, w_uk, w_uv, w_q_idx, w_out):
    B, S, D = x.shape
    assert D == D_MODEL and S % KB == 0 and S % TM == 0 and TM % TQ == 0 and KB % TQ == 0
    top_k = min(TOPK_MAX, S // 4)
    assert top_k <= N_GROUPS <= KB and KB % N_GROUPS == 0 and N_GROUPS % KEY_TILE == 0
    f32, bf16 = jnp.float32, jnp.bfloat16

    c_kr = 3 * GMLP_WIDTH + DSA_Q_RANK + DSA_KV_RANK
    c_idx = c_kr + DSA_ROPE_DIM
    c_zb = c_idx + IDX_DIM + IDX_HEADS
    assert w_in.shape[1] == c_zb + DSA_WIDTH
    kr = w_in[:, c_kr:c_idx]
    kr_sw = jnp.concatenate([kr[:, ROPE_HALF:], kr[:, :ROPE_HALF]], axis=1)
    w_all = jnp.concatenate([
        w_in[:, :c_kr], jnp.tile(kr, (1, ROPE_LANE_GROUPS)), jnp.tile(kr_sw, (1, ROPE_LANE_GROUPS)),
        w_in[:, c_idx:c_zb], jnp.zeros((D, LANES - IDX_DIM - IDX_HEADS), f32), w_in[:, c_zb:]], axis=1).astype(bf16)
    assert w_all.shape[1] == _C_END
    wq3 = w_uq.reshape(DSA_Q_RANK, DSA_HEADS, DSA_NOPE_DIM + DSA_ROPE_DIM)
    nope = jnp.pad(wq3[:, :, :DSA_NOPE_DIM], ((0, 0), (0, 0), (0, LANES - DSA_NOPE_DIM)))
    rope = wq3[:, :, DSA_NOPE_DIM:]
    rope_sw = jnp.concatenate([rope[:, :, ROPE_HALF:], rope[:, :, :ROPE_HALF]], axis=2)
    wqi = jnp.pad(w_q_idx.reshape(DSA_Q_RANK, IDX_HEADS, IDX_DIM), ((0, 0), (0, 0), (0, LANES - IDX_DIM)))
    w_q = jnp.concatenate([nope.reshape(DSA_Q_RANK, -1), rope.reshape(DSA_Q_RANK, -1),
                           rope_sw.reshape(DSA_Q_RANK, -1), wqi.reshape(DSA_Q_RANK, -1)], axis=1).astype(bf16)
    w_uk_t = jnp.pad(jnp.transpose(w_uk, (1, 2, 0)), ((0, 0), (0, LANES - DSA_NOPE_DIM), (0, 0))).astype(bf16)
    eye = jnp.eye(DSA_HEADS, dtype=f32)
    w_uv_bd = (jnp.transpose(w_uv, (1, 0, 2))[:, :, None, :] * eye[:, None, :, None]).reshape(
        DSA_HEADS * DSA_KV_RANK, DSA_WIDTH).astype(bf16)
    b_s_b = jnp.broadcast_to(b_s[:, :, None], (GMLP_GROUPS, GMLP_BLOCK, GMLP_GROUP_DIM))

    pos = jnp.arange(S, dtype=f32)
    inv_freq = ROPE_THETA ** (-jnp.arange(0, DSA_ROPE_DIM, 2, dtype=f32) / DSA_ROPE_DIM)
    ang = pos[:, None] * inv_freq[None, :]
    cos_t = jnp.tile(jnp.concatenate([jnp.cos(ang), jnp.cos(ang)], axis=1), (1, DSA_HEADS))
    sin_t = jnp.tile(jnp.concatenate([-jnp.sin(ang), jnp.sin(ang)], axis=1), (1, DSA_HEADS))

    row2 = lambda a: a.reshape(1, -1)
    tok = lambda width: pl.BlockSpec((1, TM, width), lambda b, t: (b, t, 0))
    hm = lambda width: pl.BlockSpec((1, DSA_HEADS, TM, width), lambda b, t: (b, 0, t, 0))

    out_a, gate_b, qidx, widx, qcat, kidx, kcat, vcat = pl.pallas_call(
        _proj_kernel,
        grid=(B, S // TM),
        in_specs=[tok(D), _full((1, D)), _full((D, _C_END)), _full((1, GMLP_WIDTH)), _full((1, GMLP_WIDTH)),
                  _full((GMLP_GROUPS, GMLP_BLOCK, GMLP_BLOCK)), _full((GMLP_GROUPS, GMLP_BLOCK, GMLP_GROUP_DIM)),
                  _full((1, DSA_Q_RANK)), _full((1, DSA_KV_RANK)), _full((DSA_Q_RANK, _Q_END)),
                  _full((DSA_HEADS, LANES, DSA_KV_RANK)),
                  pl.BlockSpec((TM, 2 * LANES), lambda b, t: (t, 0)),
                  pl.BlockSpec((TM, 2 * LANES), lambda b, t: (t, 0))],
        out_specs=[tok(GMLP_WIDTH), tok(DSA_WIDTH), hm(LANES),
                   pl.BlockSpec((1, LANES, TM), lambda b, t: (b, 0, t)),
                   pl.BlockSpec((1, TM // TQ, DSA_HEADS * TQ, Q_CAT), lambda b, t: (b, t, 0, 0)),
                   tok(LANES), tok(Q_CAT), tok(Q_CAT)],
        out_shape=[jax.ShapeDtypeStruct((B, S, GMLP_WIDTH), bf16),
                   jax.ShapeDtypeStruct((B, S, DSA_WIDTH), f32),
                   jax.ShapeDtypeStruct((B, DSA_HEADS, S, LANES), bf16),
                   jax.ShapeDtypeStruct((B, LANES, S), f32),
                   jax.ShapeDtypeStruct((B, S // TQ, DSA_HEADS * TQ, Q_CAT), bf16),
                   jax.ShapeDtypeStruct((B, S, LANES), bf16),
                   jax.ShapeDtypeStruct((B, S, Q_CAT), bf16),
                   jax.ShapeDtypeStruct((B, S, Q_CAT), bf16)],
        compiler_params=pltpu.CompilerParams(
            dimension_semantics=("arbitrary", "arbitrary"), vmem_limit_bytes=48 * 1024 * 1024),
        name="proj_gmlp_dsa_prep",
    )(x, row2(pre_g), w_all, row2(ln_g), row2(ln_b), w_s, b_s_b, row2(qn_g), row2(kvn_g), w_q, w_uk_t,
      cos_t, sin_t)

    qt = lambda width: pl.BlockSpec((1, DSA_HEADS, TQ, width), lambda b, t: (b, 0, t, 0))
    seq = lambda width: pl.BlockSpec((1, S, width), lambda b, t: (b, 0, 0), pipeline_mode=pl.Buffered(1))
    o_lat = pl.pallas_call(
        functools.partial(_dsa_kernel, top_k=top_k),
        grid=(B, S // TQ),
        in_specs=[qt(LANES), pl.BlockSpec((1, LANES, TQ), lambda b, t: (b, 0, t)),
                  pl.BlockSpec((1, 1, DSA_HEADS * TQ, Q_CAT), lambda b, t: (b, t, 0, 0)),
                  seq(LANES), seq(Q_CAT), seq(Q_CAT)],
        out_specs=pl.BlockSpec((1, TQ, DSA_HEADS * DSA_KV_RANK), lambda b, t: (b, t, 0)),
        out_shape=jax.ShapeDtypeStruct((B, S, DSA_HEADS * DSA_KV_RANK), bf16),
        scratch_shapes=[pltpu.VMEM((S // KB, KB, TQ), f32),
                        pltpu.VMEM((N_GROUPS, TQ), f32),
                        pltpu.VMEM((TQ, KB), f32),
                        pltpu.VMEM((DSA_HEADS * TQ, LANES), f32),
                        pltpu.VMEM((DSA_HEADS * TQ, Q_CAT), f32),
                        pltpu.VMEM((8, LANES), f32)],
        compiler_params=pltpu.CompilerParams(
            dimension_semantics=("arbitrary", "arbitrary"), vmem_limit_bytes=56 * 1024 * 1024),
        name="dsa_index_select_attend",
    )(qidx, widx, qcat, kidx, kcat, vcat)

    return pl.pallas_call(
        _out_kernel,
        grid=(B, S // TM),
        in_specs=[tok(D), tok(GMLP_WIDTH), tok(DSA_HEADS * DSA_KV_RANK), tok(DSA_WIDTH),
                  _full((DSA_HEADS * DSA_KV_RANK, DSA_WIDTH)), _full((GMLP_WIDTH + DSA_WIDTH, D)), _full((1, D))],
        out_specs=tok(D),
        out_shape=jax.ShapeDtypeStruct((B, S, D), f32),
        compiler_params=pltpu.CompilerParams(
            dimension_semantics=("arbitrary", "arbitrary"), vmem_limit_bytes=32 * 1024 * 1024),
        name="out_proj_norm_residual",
    )(x, out_a, o_lat, gate_b, w_uv_bd, w_out.astype(bf16), row2(post_g))


def kernel(x, w_in, pre_norm_g, post_norm_g, gmlp_ln_g, gmlp_ln_b, gmlp_w_s, gmlp_b_s, dsa_q_norm_g, dsa_kv_norm_g, dsa_w_uq, dsa_w_uk, dsa_w_uv, dsa_w_q_idx, w_out):
    for l in range(w_in.shape[0]):
        x = _layer(x, w_in[l], pre_norm_g[l], post_norm_g[l], gmlp_ln_g[l], gmlp_ln_b[l], gmlp_w_s[l],
                   gmlp_b_s[l], dsa_q_norm_g[l], dsa_kv_norm_g[l], dsa_w_uq[l], dsa_w_uk[l], dsa_w_uv[l],
                   dsa_w_q_idx[l], w_out[l])
    return x
```

```python
import functools
import math

import jax
import jax.numpy as jnp
import numpy as np
from jax import lax
from jax.experimental import pallas as pl
from jax.experimental.pallas import tpu as pltpu

D_MODEL = 1024
CHUNK = 64
EPS = 1e-6
GMLP_GROUPS = 4
GMLP_GROUP_DIM = 128
GMLP_WIDTH = GMLP_GROUPS * GMLP_GROUP_DIM
GMLP_BLOCK = 128
DSA_HEADS = 8
DSA_V_DIM = 64
DSA_WIDTH = DSA_HEADS * DSA_V_DIM
DSA_NOPE_DIM = 64
DSA_ROPE_DIM = 32
DSA_Q_RANK = 256
DSA_KV_RANK = 128
IDX_HEADS = 8
IDX_DIM = 64
TOPK_MAX = 256
ROPE_THETA = 10000.0

LANES = 128
ROPE_HALF = DSA_ROPE_DIM // 2
ROPE_LANE_GROUPS = LANES // DSA_ROPE_DIM
Q_CAT = DSA_KV_RANK + LANES
W_IDX_LANE = IDX_DIM
UV_GROUP = LANES // DSA_V_DIM

TM = 512
TQ = 512
KB = 512
KEY_TILE = 128
Q_TILE = 256
COUNT_ROWS = 32
N_GROUPS = 256
NEG_BIG = -1e30
ROW_SUM_MIN = 1e-26
BOUND_SLACK = 1.02
KEY_TINY = 0x00800000
KEY_LOWEST = -0x7F800000
FIRST_PIVOTS = (0.8, 0.0, 0.4)
MAX_BISECT = 40

_C_UV, _C_ZA, _C_CQ, _C_CKV = 0, 1024, 1536, 1792
_C_KR, _C_KRS, _C_SLAB, _C_ZB, _C_END = 1920, 2048, 2176, 2304, 2816
_Q_NOPE, _Q_ROPE, _Q_ROPES, _Q_IDX, _Q_END = 0, 1024, 1280, 1536, 2560

_NT = (((1,), (1,)), ((), ()))


def _bf16(a):
    return a.astype(jnp.bfloat16)


def _dot(a, b):
    return jnp.dot(a, b, preferred_element_type=jnp.float32)


def _dot_nt(a, b):
    return lax.dot_general(a, b, _NT, preferred_element_type=jnp.float32)


def _chunk_of(pos):
    return lax.shift_right_logical(pos, jnp.int32(CHUNK.bit_length() - 1))


def _sort_key(s):
    bits = pltpu.bitcast(s, jnp.int32)
    return jnp.where(bits < 0, bits ^ jnp.int32(0x7FFFFFFF), bits)


def _key_to_float(key):
    return pltpu.bitcast(jnp.where(key < 0, key ^ jnp.int32(0x7FFFFFFF), key), jnp.float32)


def _silu(z):
    return z / (1.0 + jnp.exp(-z))


def _gelu_exact(a):
    return 0.5 * a * (1.0 + lax.erf(a * np.float32(math.sqrt(0.5))))


def _proj_kernel(x_ref, pre_g_ref, w_all_ref, ln_g_ref, ln_b_ref, w_s_ref, b_s_ref,
                 qn_g_ref, kvn_g_ref, w_q_ref, w_uk_ref, cos_ref, sin_ref,
                 out_a_ref, gate_b_ref, qidx_ref, widx_ref, qcat_ref, kidx_ref, kcat_ref, vcat_ref):
    x = x_ref[0]
    h = x * lax.rsqrt(jnp.mean(x * x, axis=-1, keepdims=True) + EPS) * pre_g_ref[...]
    proj = _dot(_bf16(h), w_all_ref[...])

    uv = _gelu_exact(proj[:, _C_UV:_C_ZA])
    u, v = uv[:, :GMLP_WIDTH], uv[:, GMLP_WIDTH:]
    mu = jnp.mean(v, axis=-1, keepdims=True)
    vc = v - mu
    var = jnp.mean(vc * vc, axis=-1, keepdims=True)
    vn = _bf16(vc * lax.rsqrt(var + EPS) * ln_g_ref[...] + ln_b_ref[...])
    gate_a = _silu(proj[:, _C_ZA:_C_CQ])
    t_chunk = _chunk_of(lax.broadcasted_iota(jnp.int32, (GMLP_BLOCK, GMLP_BLOCK), 0))
    s_chunk = _chunk_of(lax.broadcasted_iota(jnp.int32, (GMLP_BLOCK, GMLP_BLOCK), 1))
    for g in range(GMLP_GROUPS):
        w_g = _bf16(jnp.where(s_chunk <= t_chunk, w_s_ref[g], 0.0))
        cols = slice(g * GMLP_GROUP_DIM, (g + 1) * GMLP_GROUP_DIM)
        for r in range(TM // GMLP_BLOCK):
            rows = slice(r * GMLP_BLOCK, (r + 1) * GMLP_BLOCK)
            y = _dot(w_g, vn[rows, cols]) + b_s_ref[g]
            out_a_ref[0, rows, cols] = _bf16(u[rows, cols] * y * gate_a[rows, cols])

    gate_b_ref[0] = _silu(proj[:, _C_ZB:_C_END])

    c_q = proj[:, _C_CQ:_C_CKV]
    c_q = c_q * lax.rsqrt(jnp.mean(c_q * c_q, axis=-1, keepdims=True) + EPS) * qn_g_ref[...]
    c_kv = proj[:, _C_CKV:_C_KR]
    c_kv = _bf16(c_kv * lax.rsqrt(jnp.mean(c_kv * c_kv, axis=-1, keepdims=True) + EPS) * kvn_g_ref[...])
    cos = cos_ref[...]
    sin = sin_ref[...]
    k_rope = proj[:, _C_KR:_C_KRS] * cos[:, :LANES] + proj[:, _C_KRS:_C_SLAB] * sin[:, :LANES]
    slab = proj[:, _C_SLAB:_C_ZB]
    kidx_ref[0] = _bf16(slab)
    widx_ref[0] = slab.T
    kcat_ref[0, :, :DSA_KV_RANK] = c_kv
    kcat_ref[0, :, DSA_KV_RANK:] = _bf16(k_rope)
    lane = lax.broadcasted_iota(jnp.int32, (TM, LANES), 1)
    vcat_ref[0, :, :DSA_KV_RANK] = c_kv
    vcat_ref[0, :, DSA_KV_RANK:] = jnp.where(lane == 0, 1.0, 0.0).astype(jnp.bfloat16)

    q_all = _dot(_bf16(c_q), w_q_ref[...])
    q_rope = q_all[:, _Q_ROPE:_Q_ROPES] * cos + q_all[:, _Q_ROPES:_Q_IDX] * sin
    scale = np.float32(1.0 / math.sqrt(DSA_NOPE_DIM + DSA_ROPE_DIM))
    for hd in range(DSA_HEADS):
        cols = slice(_Q_NOPE + hd * LANES, _Q_NOPE + (hd + 1) * LANES)
        q_lat = _dot(_bf16(q_all[:, cols]), w_uk_ref[hd])
        grp, sub = divmod(hd, ROPE_LANE_GROUPS)
        own = (lane >= sub * DSA_ROPE_DIM) & (lane < (sub + 1) * DSA_ROPE_DIM)
        q_r = jnp.where(own, q_rope[:, grp * LANES:(grp + 1) * LANES] * scale, 0.0)
        for t in range(TM // TQ):
            src = slice(t * TQ, (t + 1) * TQ)
            dst = slice(hd * TQ, (hd + 1) * TQ)
            qcat_ref[0, t, dst, :DSA_KV_RANK] = _bf16(q_lat[src] * scale)
            qcat_ref[0, t, dst, DSA_KV_RANK:] = _bf16(q_r[src])
        icol = slice(_Q_IDX + hd * LANES, _Q_IDX + (hd + 1) * LANES)
        qidx_ref[0, hd] = _bf16(q_all[:, icol])


def _dsa_kernel(qidx_ref, widx_ref, qcat_ref, kidx_ref, kcat_ref, vcat_ref, o_ref,
                sc_ref, gmax_ref, bias_ref, m_ref, acc_ref, kmax_ref, *, top_k):
    i = pl.program_id(1)
    n_blocks = (i * TQ) // KB + 1
    idx_scale = np.float32(IDX_HEADS ** -0.5 * IDX_DIM ** -0.5)
    f32 = jnp.float32

    q_chunk = _chunk_of(i * TQ + lax.broadcasted_iota(jnp.int32, (1, TQ), 1))
    key_t = lax.broadcasted_iota(jnp.int32, (KEY_TILE, Q_TILE), 0)

    def over_keys(op, a):
        return op(a, axis=0, keepdims=True)

    w = widx_ref[0] * idx_scale
    gmax_ref[...] = jnp.full(gmax_ref.shape, -jnp.inf, f32)

    def score_block(kb, carry):
        for ks in range(KB // KEY_TILE):
            k = kidx_ref[0, pl.ds(pl.multiple_of(kb * KB + ks * KEY_TILE, KEY_TILE), KEY_TILE), :]
            krows = slice(ks * KEY_TILE, (ks + 1) * KEY_TILE)
            grows = slice((ks * KEY_TILE) % N_GROUPS, (ks * KEY_TILE) % N_GROUPS + KEY_TILE)
            for qt in range(TQ // Q_TILE):
                qs = slice(qt * Q_TILE, (qt + 1) * Q_TILE)
                acc = jnp.zeros((KEY_TILE, Q_TILE), f32)
                for hd in range(IDX_HEADS):
                    logit = _dot_nt(k, qidx_ref[0, hd, qs, :])
                    acc = acc + w[W_IDX_LANE + hd:W_IDX_LANE + hd + 1, qs] * jnp.maximum(logit, 0.0)
                adm = _chunk_of(kb * KB + ks * KEY_TILE + key_t) <= q_chunk[:, qs]
                sc_ref[kb, krows, qs] = jnp.where(adm, acc, -jnp.inf)
                gmax_ref[grows, qs] = jnp.maximum(gmax_ref[grows, qs], jnp.where(adm, acc, -jnp.inf))
        return carry
    lax.fori_loop(0, n_blocks, score_block, 0)

    kf = np.float32(top_k)

    def count_ge(mid):
        mid_f = _key_to_float(mid)

        def body(kb, cnt):
            for r in range(KB // COUNT_ROWS):
                cnt = cnt + jnp.where(sc_ref[kb, r * COUNT_ROWS:(r + 1) * COUNT_ROWS, :] >= mid_f, 1.0, 0.0)
            return cnt
        return over_keys(jnp.sum, lax.fori_loop(0, n_blocks, body, jnp.zeros((COUNT_ROWS, TQ), f32)))

    def canonical(key):
        return jnp.where((key > 0) & (key < KEY_TINY), KEY_TINY, jnp.where((key < 0) & (key >= -KEY_TINY), 0, key))

    def pivot(lo, hi):
        mid = canonical((lo & hi) + lax.shift_right_arithmetic(lo ^ hi, jnp.int32(1)))
        return jnp.where((mid <= lo) | (mid >= hi), lo, mid)

    def pivot_zero_first(lo, hi):
        return jnp.where((lo <= 0) & (hi > KEY_TINY), KEY_TINY,
                         jnp.where((lo < 0) & (hi == KEY_TINY), 0, pivot(lo, hi)))

    grp = gmax_ref[...]
    grp_min = over_keys(jnp.min, grp)
    select_all = (q_chunk + 1) * CHUNK <= top_k
    lo = jnp.where(select_all, KEY_LOWEST, canonical(_sort_key(grp_min)))
    hi = jnp.where(select_all, KEY_LOWEST, canonical(_sort_key(over_keys(jnp.max, grp)) + 1))

    def step(lo, hi, mid):
        cnt = count_ge(mid)
        ge = cnt >= kf
        return jnp.where(ge, mid, lo), jnp.where(cnt == kf, mid, jnp.where(ge, hi, mid))

    grp_mean = over_keys(jnp.sum, grp) / np.float32(N_GROUPS)
    grp_var = over_keys(jnp.sum, grp * grp) / np.float32(N_GROUPS) - grp_mean * grp_mean
    grp_std = jnp.sqrt(jnp.maximum(grp_var, 0.0))
    for below in FIRST_PIVOTS:
        guess = canonical(_sort_key(grp_mean - np.float32(below) * grp_std))
        usable = (grp_min > -jnp.inf) & (guess > lo) & (guess < hi)
        lo, hi = step(lo, hi, jnp.where(usable, guess, pivot(lo, hi)))
    for _ in range(2):
        lo, hi = step(lo, hi, pivot_zero_first(lo, hi))

    def bisect_cond(state):
        it, lo, _, mid = state
        return jnp.logical_and(it < MAX_BISECT, jnp.max(jnp.where(mid != lo, 1.0, 0.0)) > 0.0)

    def bisect_body(state):
        it, lo, hi, mid = state
        lo, hi = step(lo, hi, mid)
        return it + 1, lo, hi, pivot(lo, hi)

    _, lo, hi, _ = lax.while_loop(bisect_cond, bisect_body, (jnp.int32(0), lo, hi, pivot(lo, hi)))
    thr = _key_to_float(lo)
    tie = hi != lo

    @pl.when(jnp.max(jnp.where(tie, 1.0, 0.0)) > 0.0)
    def _():
        need = kf - count_ge(hi)
        upto = (lax.broadcasted_iota(jnp.int32, (KB, KB), 1)
                <= lax.broadcasted_iota(jnp.int32, (KB, KB), 0)).astype(jnp.bfloat16)

        for qt in range(TQ // Q_TILE):
            qs = slice(qt * Q_TILE, (qt + 1) * Q_TILE)

            @pl.when(jnp.max(jnp.where(tie[:, qs], 1.0, 0.0)) > 0.0)
            def _(qs=qs):
                def drop_body(kb, seen):
                    keys = sc_ref[kb, :, qs]
                    band = tie[:, qs] & (keys == thr[:, qs])
                    band_f = jnp.where(band, 1.0, 0.0)
                    rank = _dot(upto, _bf16(band_f))
                    late = band & (rank + seen > need[:, qs])
                    sc_ref[kb, :, qs] = jnp.where(late, -jnp.inf, keys)
                    return seen + over_keys(jnp.sum, band_f)
                lax.fori_loop(0, n_blocks, drop_body, jnp.zeros((1, Q_TILE), f32))

    @pl.when(i == 0)
    def _():
        def norm_block(kb, run):
            kc = kcat_ref[0, pl.ds(pl.multiple_of(kb * KB, KB), KB), :].astype(f32)
            return jnp.maximum(run, jnp.sum(kc * kc, axis=1, keepdims=True))
        run = lax.fori_loop(0, kcat_ref.shape[1] // KB, norm_block, jnp.zeros((KB, 1), f32))
        kmax_ref[...] = jnp.broadcast_to(jnp.max(run, axis=0, keepdims=True), kmax_ref.shape)

    k_norm2 = kmax_ref[0:1, :]
    q_all = qcat_ref[0, 0]
    q_norm2 = _dot(q_all * q_all, jnp.ones((Q_CAT, LANES), jnp.bfloat16))
    qk2 = jnp.maximum(q_norm2 * k_norm2, 1e-30)
    m_ref[...] = qk2 * lax.rsqrt(qk2) * BOUND_SLACK

    chunks = [slice(c * LANES, (c + 1) * LANES) for c in range(KB // LANES)]
    head_rows = [slice(hd * TQ, (hd + 1) * TQ) for hd in range(DSA_HEADS)]

    def set_bias(kb):
        for qt in range(TQ // LANES):
            qs = slice(qt * LANES, (qt + 1) * LANES)
            bias_ref[qs, :] = jnp.where(sc_ref[kb, :, qs] >= thr[:, qs], 0.0, NEG_BIG).T

    def attend():
        acc_ref[...] = jnp.zeros(acc_ref.shape, f32)

        def attn_block(kb, carry):
            rows = pl.ds(pl.multiple_of(kb * KB, KB), KB)
            kc = kcat_ref[0, rows, :]
            vc = vcat_ref[0, rows, :]
            set_bias(kb)
            for hrows in head_rows:
                s = _dot_nt(qcat_ref[0, 0, hrows, :], kc)
                m = m_ref[hrows, :]
                p = [_bf16(jnp.exp(s[:, cols] + bias_ref[:, cols] - m)) for cols in chunks]
                acc_ref[hrows, :] += _dot(jnp.concatenate(p, axis=1), vc)
            return carry
        lax.fori_loop(0, n_blocks, attn_block, 0)
        return jnp.min(acc_ref[:, DSA_KV_RANK:DSA_KV_RANK + 1])

    @pl.when(attend() < ROW_SUM_MIN)
    def _():
        m_ref[...] = jnp.full(m_ref.shape, NEG_BIG, f32)

        def max_block(kb, carry):
            kc = kcat_ref[0, pl.ds(pl.multiple_of(kb * KB, KB), KB), :]
            set_bias(kb)
            for hrows in head_rows:
                s = _dot_nt(qcat_ref[0, 0, hrows, :], kc)
                sm = [s[:, cols] + bias_ref[:, cols] for cols in chunks]
                m_ref[hrows, :] = jnp.maximum(m_ref[hrows, :], functools.reduce(jnp.maximum, sm))
            return carry
        lax.fori_loop(0, n_blocks, max_block, 0)
        m_all = m_ref[...]
        m_ref[...] = jnp.broadcast_to(jnp.max(m_all, axis=1, keepdims=True), m_all.shape)
        attend()

    for hd, hrows in enumerate(head_rows):
        a = acc_ref[hrows, :]
        denom = jnp.broadcast_to(a[:, DSA_KV_RANK:DSA_KV_RANK + 1], (TQ, DSA_KV_RANK))
        o_ref[0, :, hd * DSA_KV_RANK:(hd + 1) * DSA_KV_RANK] = _bf16(a[:, :DSA_KV_RANK] / denom)


def _out_kernel(x_ref, out_a_ref, o_lat_ref, gate_b_ref, w_uv_ref, w_out_ref, post_g_ref, y_ref):
    group = UV_GROUP * DSA_KV_RANK
    o = jnp.concatenate([_dot(o_lat_ref[0, :, g * group:(g + 1) * group], w_uv_ref[g])
                         for g in range(DSA_HEADS // UV_GROUP)], axis=1)
    out_b = _bf16(o * gate_b_ref[0])
    y = _dot(out_a_ref[0], w_out_ref[:GMLP_WIDTH, :]) + _dot(out_b, w_out_ref[GMLP_WIDTH:, :])
    y = y * lax.rsqrt(jnp.mean(y * y, axis=-1, keepdims=True) + EPS) * post_g_ref[...]
    y_ref[0] = x_ref[0] + y


def _full(shape):
    return pl.BlockSpec(shape, lambda b, t: (0,) * len(shape))


def _layer(x, w_in, pre_g, post_g, ln_g, ln_b, w_s, b_s, qn_g, kvn_g, w_uq, w_uk, w_uv, w_q_idx, w_out):
    B, S, D = x.shape
    assert D == D_MODEL and S % KB == 0 and S % TM == 0 and TM % TQ == 0 and KB % TQ == 0
    top_k = min(TOPK_MAX, S // 4)
    assert top_k <= N_GROUPS <= KB and KB % N_GROUPS == 0 and N_GROUPS % KEY_TILE == 0
    f32, bf16 = jnp.float32, jnp.bfloat16

    c_kr = 3 * GMLP_WIDTH + DSA_Q_RANK + DSA_KV_RANK
    c_idx = c_kr + DSA_ROPE_DIM
    c_zb = c_idx + IDX_DIM + IDX_HEADS
    assert w_in.shape[1] == c_zb + DSA_WIDTH
    kr = w_in[:, c_kr:c_idx]
    kr_sw = jnp.concatenate([kr[:, ROPE_HALF:], kr[:, :ROPE_HALF]], axis=1)
    w_all = jnp.concatenate([
        w_in[:, :c_kr], jnp.tile(kr, (1, ROPE_LANE_GROUPS)), jnp.tile(kr_sw, (1, ROPE_LANE_GROUPS)),
        w_in[:, c_idx:c_zb], jnp.zeros((D, LANES - IDX_DIM - IDX_HEADS), f32), w_in[:, c_zb:]], axis=1).astype(bf16)
    assert w_all.shape[1] == _C_END
    wq3 = w_uq.reshape(DSA_Q_RANK, DSA_HEADS, DSA_NOPE_DIM + DSA_ROPE_DIM)
    nope = jnp.pad(wq3[:, :, :DSA_NOPE_DIM], ((0, 0), (0, 0), (0, LANES - DSA_NOPE_DIM)))
    rope = wq3[:, :, DSA_NOPE_DIM:]
    rope_sw = jnp.concatenate([rope[:, :, ROPE_HALF:], rope[:, :, :ROPE_HALF]], axis=2)
    wqi = jnp.pad(w_q_idx.reshape(DSA_Q_RANK, IDX_HEADS, IDX_DIM), ((0, 0), (0, 0), (0, LANES - IDX_DIM)))
    w_q = jnp.concatenate([nope.reshape(DSA_Q_RANK, -1), rope.reshape(DSA_Q_RANK, -1),
                           rope_sw.reshape(DSA_Q_RANK, -1), wqi.reshape(DSA_Q_RANK, -1)], axis=1).astype(bf16)
    w_uk_t = jnp.pad(jnp.transpose(w_uk, (1, 2, 0)), ((0, 0), (0, LANES - DSA_NOPE_DIM), (0, 0))).astype(bf16)
    eye = jnp.eye(UV_GROUP, dtype=f32)
    w_uv_g = jnp.transpose(w_uv, (1, 0, 2)).reshape(DSA_HEADS // UV_GROUP, UV_GROUP, DSA_KV_RANK, DSA_V_DIM)
    w_uv_bd = (w_uv_g[:, :, :, None, :] * eye[None, :, None, :, None]).reshape(
        DSA_HEADS // UV_GROUP, UV_GROUP * DSA_KV_RANK, UV_GROUP * DSA_V_DIM).astype(bf16)
    b_s_b = jnp.broadcast_to(b_s[:, :, None], (GMLP_GROUPS, GMLP_BLOCK, GMLP_GROUP_DIM))

    pos = jnp.arange(S, dtype=f32)
    inv_freq = ROPE_THETA ** (-jnp.arange(0, DSA_ROPE_DIM, 2, dtype=f32) / DSA_ROPE_DIM)
    ang = pos[:, None] * inv_freq[None, :]
    cos_t = jnp.tile(jnp.concatenate([jnp.cos(ang), jnp.cos(ang)], axis=1), (1, DSA_HEADS))
    sin_t = jnp.tile(jnp.concatenate([-jnp.sin(ang), jnp.sin(ang)], axis=1), (1, DSA_HEADS))

    row2 = lambda a: a.reshape(1, -1)
    tok = lambda width: pl.BlockSpec((1, TM, width), lambda b, t: (b, t, 0))
    hm = lambda width: pl.BlockSpec((1, DSA_HEADS, TM, width), lambda b, t: (b, 0, t, 0))

    out_a, gate_b, qidx, widx, qcat, kidx, kcat, vcat = pl.pallas_call(
        _proj_kernel,
        grid=(B, S // TM),
        in_specs=[tok(D), _full((1, D)), _full((D, _C_END)), _full((1, GMLP_WIDTH)), _full((1, GMLP_WIDTH)),
                  _full((GMLP_GROUPS, GMLP_BLOCK, GMLP_BLOCK)), _full((GMLP_GROUPS, GMLP_BLOCK, GMLP_GROUP_DIM)),
                  _full((1, DSA_Q_RANK)), _full((1, DSA_KV_RANK)), _full((DSA_Q_RANK, _Q_END)),
                  _full((DSA_HEADS, LANES, DSA_KV_RANK)),
                  pl.BlockSpec((TM, 2 * LANES), lambda b, t: (t, 0)),
                  pl.BlockSpec((TM, 2 * LANES), lambda b, t: (t, 0))],
        out_specs=[tok(GMLP_WIDTH), tok(DSA_WIDTH), hm(LANES),
                   pl.BlockSpec((1, LANES, TM), lambda b, t: (b, 0, t)),
                   pl.BlockSpec((1, TM // TQ, DSA_HEADS * TQ, Q_CAT), lambda b, t: (b, t, 0, 0)),
                   tok(LANES), tok(Q_CAT), tok(Q_CAT)],
        out_shape=[jax.ShapeDtypeStruct((B, S, GMLP_WIDTH), bf16),
                   jax.ShapeDtypeStruct((B, S, DSA_WIDTH), f32),
                   jax.ShapeDtypeStruct((B, DSA_HEADS, S, LANES), bf16),
                   jax.ShapeDtypeStruct((B, LANES, S), f32),
                   jax.ShapeDtypeStruct((B, S // TQ, DSA_HEADS * TQ, Q_CAT), bf16),
                   jax.ShapeDtypeStruct((B, S, LANES), bf16),
                   jax.ShapeDtypeStruct((B, S, Q_CAT), bf16),
                   jax.ShapeDtypeStruct((B, S, Q_CAT), bf16)],
        compiler_params=pltpu.CompilerParams(
            dimension_semantics=("arbitrary", "arbitrary"), vmem_limit_bytes=48 * 1024 * 1024),
        name="proj_gmlp_dsa_prep",
    )(x, row2(pre_g), w_all, row2(ln_g), row2(ln_b), w_s, b_s_b, row2(qn_g), row2(kvn_g), w_q, w_uk_t,
      cos_t, sin_t)

    qt = lambda width: pl.BlockSpec((1, DSA_HEADS, TQ, width), lambda b, t: (b, 0, t, 0))
    seq = lambda width: pl.BlockSpec((1, S, width), lambda b, t: (b, 0, 0), pipeline_mode=pl.Buffered(1))
    o_lat = pl.pallas_call(
        functools.partial(_dsa_kernel, top_k=top_k),
        grid=(B, S // TQ),
        in_specs=[qt(LANES), pl.BlockSpec((1, LANES, TQ), lambda b, t: (b, 0, t)),
                  pl.BlockSpec((1, 1, DSA_HEADS * TQ, Q_CAT), lambda b, t: (b, t, 0, 0)),
                  seq(LANES), seq(Q_CAT), seq(Q_CAT)],
        out_specs=pl.BlockSpec((1, TQ, DSA_HEADS * DSA_KV_RANK), lambda b, t: (b, t, 0)),
        out_shape=jax.ShapeDtypeStruct((B, S, DSA_HEADS * DSA_KV_RANK), bf16),
        scratch_shapes=[pltpu.VMEM((S // KB, KB, TQ), f32),
                        pltpu.VMEM((N_GROUPS, TQ), f32),
                        pltpu.VMEM((TQ, KB), f32),
                        pltpu.VMEM((DSA_HEADS * TQ, LANES), f32),
                        pltpu.VMEM((DSA_HEADS * TQ, Q_CAT), f32),
                        pltpu.VMEM((8, LANES), f32)],
        compiler_params=pltpu.CompilerParams(
            dimension_semantics=("arbitrary", "arbitrary"), vmem_limit_bytes=56 * 1024 * 1024),
        name="dsa_index_select_attend",
    )(qidx, widx, qcat, kidx, kcat, vcat)

    return pl.pallas_call(
        _out_kernel,
        grid=(B, S // TM),
        in_specs=[tok(D), tok(GMLP_WIDTH), tok(DSA_HEADS * DSA_KV_RANK), tok(DSA_WIDTH),
                  _full((DSA_HEADS // UV_GROUP, UV_GROUP * DSA_KV_RANK, UV_GROUP * DSA_V_DIM)),
                  _full((GMLP_WIDTH + DSA_WIDTH, D)), _full((1, D))],
        out_specs=tok(D),
        out_shape=jax.ShapeDtypeStruct((B, S, D), f32),
        compiler_params=pltpu.CompilerParams(
            dimension_semantics=("arbitrary", "arbitrary"), vmem_limit_bytes=32 * 1024 * 1024),
        name="out_proj_norm_residual",
    )(x, out_a, o_lat, gate_b, w_uv_bd, w_out.astype(bf16), row2(post_g))


def kernel(x, w_in, pre_norm_g, post_norm_g, gmlp_ln_g, gmlp_ln_b, gmlp_w_s, gmlp_b_s, dsa_q_norm_g, dsa_kv_norm_g, dsa_w_uq, dsa_w_uk, dsa_w_uv, dsa_w_q_idx, w_out):
    for l in range(w_in.shape[0]):
        x = _layer(x, w_in[l], pre_norm_g[l], post_norm_g[l], gmlp_ln_g[l], gmlp_ln_b[l], gmlp_w_s[l],
                   gmlp_b_s[l], dsa_q_norm_g[l], dsa_kv_norm_g[l], dsa_w_uq[l], dsa_w_uk[l], dsa_w_uv[l],
                   dsa_w_q_idx[l], w_out[l])
    return x
```

```python
import functools
import math

import jax
import jax.numpy as jnp
import numpy as np
from jax import lax
from jax.experimental import pallas as pl
from jax.experimental.pallas import tpu as pltpu

D_MODEL = 1024
CHUNK = 64
EPS = 1e-6
GMLP_GROUPS = 4
GMLP_GROUP_DIM = 128
GMLP_WIDTH = GMLP_GROUPS * GMLP_GROUP_DIM
GMLP_BLOCK = 128
DSA_HEADS = 8
DSA_V_DIM = 64
DSA_WIDTH = DSA_HEADS * DSA_V_DIM
DSA_NOPE_DIM = 64
DSA_ROPE_DIM = 32
DSA_Q_RANK = 256
DSA_KV_RANK = 128
IDX_HEADS = 8
IDX_DIM = 64
TOPK_MAX = 256
ROPE_THETA = 10000.0

LANES = 128
ROPE_HALF = DSA_ROPE_DIM // 2
ROPE_LANE_GROUPS = LANES // DSA_ROPE_DIM
Q_CAT = DSA_KV_RANK + LANES
W_IDX_LANE = IDX_DIM
UV_GROUP = LANES // DSA_V_DIM

TM = 512
TQ = 512
KB = 512
KEY_TILE = 128
Q_TILE = 256
COUNT_ROWS = 32
N_GROUPS = 256
NEG_BIG = -1e30
ROW_SUM_MIN = 1e-26
BOUND_SLACK = 1.02
KEY_TINY = 0x00800000
KEY_LOWEST = -0x7F800000
FIRST_PIVOTS = (0.8, 0.0, 0.4)
MAX_BISECT = 40

_C_UV, _C_ZA, _C_CQ, _C_CKV = 0, 1024, 1536, 1792
_C_KR, _C_KRS, _C_SLAB, _C_ZB, _C_END = 1920, 2048, 2176, 2304, 2816
_Q_NOPE, _Q_ROPE, _Q_ROPES, _Q_IDX, _Q_END = 0, 1024, 1280, 1536, 2560

_NT = (((1,), (1,)), ((), ()))


def _bf16(a):
    return a.astype(jnp.bfloat16)


def _dot(a, b):
    return jnp.dot(a, b, preferred_element_type=jnp.float32)


def _dot_nt(a, b):
    return lax.dot_general(a, b, _NT, preferred_element_type=jnp.float32)


def _chunk_of(pos):
    return lax.shift_right_logical(pos, jnp.int32(CHUNK.bit_length() - 1))


def _sort_key(s):
    bits = pltpu.bitcast(s, jnp.int32)
    return jnp.where(bits < 0, bits ^ jnp.int32(0x7FFFFFFF), bits)


def _key_to_float(key):
    return pltpu.bitcast(jnp.where(key < 0, key ^ jnp.int32(0x7FFFFFFF), key), jnp.float32)


def _silu(z):
    return z / (1.0 + jnp.exp(-z))


def _gelu_exact(a):
    return 0.5 * a * (1.0 + lax.erf(a * np.float32(math.sqrt(0.5))))


def _proj_kernel(x_ref, pre_g_ref, w_all_ref, ln_g_ref, ln_b_ref, w_s_ref, b_s_ref,
                 qn_g_ref, kvn_g_ref, w_q_ref, w_uk_ref, cos_ref, sin_ref,
                 out_a_ref, gate_b_ref, qidx_ref, widx_ref, qcat_ref, kidx_ref, kcat_ref, vcat_ref):
    x = x_ref[0]
    h = x * lax.rsqrt(jnp.mean(x * x, axis=-1, keepdims=True) + EPS) * pre_g_ref[...]
    proj = _dot(_bf16(h), w_all_ref[...])

    uv = _gelu_exact(proj[:, _C_UV:_C_ZA])
    u, v = uv[:, :GMLP_WIDTH], uv[:, GMLP_WIDTH:]
    mu = jnp.mean(v, axis=-1, keepdims=True)
    vc = v - mu
    var = jnp.mean(vc * vc, axis=-1, keepdims=True)
    vn = _bf16(vc * lax.rsqrt(var + EPS) * ln_g_ref[...] + ln_b_ref[...])
    gate_a = _silu(proj[:, _C_ZA:_C_CQ])
    t_chunk = _chunk_of(lax.broadcasted_iota(jnp.int32, (GMLP_BLOCK, GMLP_BLOCK), 0))
    s_chunk = _chunk_of(lax.broadcasted_iota(jnp.int32, (GMLP_BLOCK, GMLP_BLOCK), 1))
    for g in range(GMLP_GROUPS):
        w_g = _bf16(jnp.where(s_chunk <= t_chunk, w_s_ref[g], 0.0))
        cols = slice(g * GMLP_GROUP_DIM, (g + 1) * GMLP_GROUP_DIM)
        for r in range(TM // GMLP_BLOCK):
            rows = slice(r * GMLP_BLOCK, (r + 1) * GMLP_BLOCK)
            y = _dot(w_g, vn[rows, cols]) + b_s_ref[g]
            out_a_ref[0, rows, cols] = _bf16(u[rows, cols] * y * gate_a[rows, cols])

    gate_b_ref[0] = _silu(proj[:, _C_ZB:_C_END])

    c_q = proj[:, _C_CQ:_C_CKV]
    c_q = c_q * lax.rsqrt(jnp.mean(c_q * c_q, axis=-1, keepdims=True) + EPS) * qn_g_ref[...]
    c_kv = proj[:, _C_CKV:_C_KR]
    c_kv = _bf16(c_kv * lax.rsqrt(jnp.mean(c_kv * c_kv, axis=-1, keepdims=True) + EPS) * kvn_g_ref[...])
    cos = cos_ref[...]
    sin = sin_ref[...]
    k_rope = proj[:, _C_KR:_C_KRS] * cos[:, :LANES] + proj[:, _C_KRS:_C_SLAB] * sin[:, :LANES]
    slab = proj[:, _C_SLAB:_C_ZB]
    kidx_ref[0] = _bf16(slab)
    widx_ref[0] = slab.T
    kcat_ref[0, :, :DSA_KV_RANK] = c_kv
    kcat_ref[0, :, DSA_KV_RANK:] = _bf16(k_rope)
    lane = lax.broadcasted_iota(jnp.int32, (TM, LANES), 1)
    vcat_ref[0, :, :DSA_KV_RANK] = c_kv
    vcat_ref[0, :, DSA_KV_RANK:] = jnp.where(lane == 0, 1.0, 0.0).astype(jnp.bfloat16)

    q_all = _dot(_bf16(c_q), w_q_ref[...])
    q_rope = q_all[:, _Q_ROPE:_Q_ROPES] * cos + q_all[:, _Q_ROPES:_Q_IDX] * sin
    scale = np.float32(1.0 / math.sqrt(DSA_NOPE_DIM + DSA_ROPE_DIM))
    for hd in range(DSA_HEADS):
        cols = slice(_Q_NOPE + hd * LANES, _Q_NOPE + (hd + 1) * LANES)
        q_lat = _dot(_bf16(q_all[:, cols]), w_uk_ref[hd])
        grp, sub = divmod(hd, ROPE_LANE_GROUPS)
        own = (lane >= sub * DSA_ROPE_DIM) & (lane < (sub + 1) * DSA_ROPE_DIM)
        q_r = jnp.where(own, q_rope[:, grp * LANES:(grp + 1) * LANES] * scale, 0.0)
        for t in range(TM // TQ):
            src = slice(t * TQ, (t + 1) * TQ)
            dst = slice(hd * TQ, (hd + 1) * TQ)
            qcat_ref[0, t, dst, :DSA_KV_RANK] = _bf16(q_lat[src] * scale)
            qcat_ref[0, t, dst, DSA_KV_RANK:] = _bf16(q_r[src])
        icol = slice(_Q_IDX + hd * LANES, _Q_IDX + (hd + 1) * LANES)
        qidx_ref[0, hd] = _bf16(q_all[:, icol])


def _dsa_kernel(qidx_ref, widx_ref, qcat_ref, kidx_ref, kcat_ref, vcat_ref, o_ref,
                sc_ref, gmax_ref, bias_ref, m_ref, acc_ref, kmax_ref, *, top_k):
    i = pl.program_id(1)
    n_blocks = (i * TQ) // KB + 1
    idx_scale = np.float32(IDX_HEADS ** -0.5 * IDX_DIM ** -0.5)
    f32 = jnp.float32

    q_chunk = _chunk_of(i * TQ + lax.broadcasted_iota(jnp.int32, (1, TQ), 1))
    key_t = lax.broadcasted_iota(jnp.int32, (KEY_TILE, Q_TILE), 0)

    def over_keys(op, a):
        return op(a, axis=0, keepdims=True)

    w = widx_ref[0] * idx_scale
    gmax_ref[...] = jnp.full(gmax_ref.shape, -jnp.inf, f32)

    def score_block(kb, carry):
        for ks in range(KB // KEY_TILE):
            k = kidx_ref[0, pl.ds(pl.multiple_of(kb * KB + ks * KEY_TILE, KEY_TILE), KEY_TILE), :]
            krows = slice(ks * KEY_TILE, (ks + 1) * KEY_TILE)
            grows = slice((ks * KEY_TILE) % N_GROUPS, (ks * KEY_TILE) % N_GROUPS + KEY_TILE)
            for qt in range(TQ // Q_TILE):
                qs = slice(qt * Q_TILE, (qt + 1) * Q_TILE)
                acc = jnp.zeros((KEY_TILE, Q_TILE), f32)
                for hd in range(IDX_HEADS):
                    logit = _dot_nt(k, qidx_ref[0, hd, qs, :])
                    acc = acc + w[W_IDX_LANE + hd:W_IDX_LANE + hd + 1, qs] * jnp.maximum(logit, 0.0)
                adm = _chunk_of(kb * KB + ks * KEY_TILE + key_t) <= q_chunk[:, qs]
                sc_ref[kb, krows, qs] = jnp.where(adm, acc, -jnp.inf)
                gmax_ref[grows, qs] = jnp.maximum(gmax_ref[grows, qs], jnp.where(adm, acc, -jnp.inf))
        return carry
    lax.fori_loop(0, n_blocks, score_block, 0)

    kf = np.float32(top_k)

    def count_ge(mid):
        mid_f = _key_to_float(mid)

        def body(kb, cnt):
            for r in range(KB // COUNT_ROWS):
                cnt = cnt + jnp.where(sc_ref[kb, r * COUNT_ROWS:(r + 1) * COUNT_ROWS, :] >= mid_f, 1.0, 0.0)
            return cnt
        return over_keys(jnp.sum, lax.fori_loop(0, n_blocks, body, jnp.zeros((COUNT_ROWS, TQ), f32)))

    def canonical(key):
        return jnp.where((key > 0) & (key < KEY_TINY), KEY_TINY, jnp.where((key < 0) & (key >= -KEY_TINY), 0, key))

    def pivot(lo, hi):
        mid = canonical((lo & hi) + lax.shift_right_arithmetic(lo ^ hi, jnp.int32(1)))
        return jnp.where((mid <= lo) | (mid >= hi), lo, mid)

    def pivot_zero_first(lo, hi):
        return jnp.where((lo <= 0) & (hi > KEY_TINY), KEY_TINY,
                         jnp.where((lo < 0) & (hi == KEY_TINY), 0, pivot(lo, hi)))

    grp = gmax_ref[...]
    grp_min = over_keys(jnp.min, grp)
    select_all = (q_chunk + 1) * CHUNK <= top_k
    lo = jnp.where(select_all, KEY_LOWEST, canonical(_sort_key(grp_min)))
    hi = jnp.where(select_all, KEY_LOWEST, canonical(_sort_key(over_keys(jnp.max, grp)) + 1))

    def step(lo, hi, mid):
        cnt = count_ge(mid)
        ge = cnt >= kf
        return jnp.where(ge, mid, lo), jnp.where(cnt == kf, mid, jnp.where(ge, hi, mid))

    grp_mean = over_keys(jnp.sum, grp) / np.float32(N_GROUPS)
    grp_var = over_keys(jnp.sum, grp * grp) / np.float32(N_GROUPS) - grp_mean * grp_mean
    grp_std = jnp.sqrt(jnp.maximum(grp_var, 0.0))
    for below in FIRST_PIVOTS:
        guess = canonical(_sort_key(grp_mean - np.float32(below) * grp_std))
        usable = (grp_min > -jnp.inf) & (guess > lo) & (guess < hi)
        lo, hi = step(lo, hi, jnp.where(usable, guess, pivot(lo, hi)))
    for _ in range(2):
        lo, hi = step(lo, hi, pivot_zero_first(lo, hi))

    def bisect_cond(state):
        it, lo, _, mid = state
        return jnp.logical_and(it < MAX_BISECT, jnp.max(jnp.where(mid != lo, 1.0, 0.0)) > 0.0)

    def bisect_body(state):
        it, lo, hi, mid = state
        lo, hi = step(lo, hi, mid)
        return it + 1, lo, hi, pivot(lo, hi)

    _, lo, hi, _ = lax.while_loop(bisect_cond, bisect_body, (jnp.int32(0), lo, hi, pivot(lo, hi)))
    thr = _key_to_float(lo)
    tie = hi != lo

    @pl.when(jnp.max(jnp.where(tie, 1.0, 0.0)) > 0.0)
    def _():
        need = kf - count_ge(hi)
        upto = (lax.broadcasted_iota(jnp.int32, (KB, KB), 1)
                <= lax.broadcasted_iota(jnp.int32, (KB, KB), 0)).astype(jnp.bfloat16)

        for qt in range(TQ // Q_TILE):
            qs = slice(qt * Q_TILE, (qt + 1) * Q_TILE)

            @pl.when(jnp.max(jnp.where(tie[:, qs], 1.0, 0.0)) > 0.0)
            def _(qs=qs):
                def drop_body(kb, seen):
                    keys = sc_ref[kb, :, qs]
                    band = tie[:, qs] & (keys == thr[:, qs])
                    band_f = jnp.where(band, 1.0, 0.0)
                    rank = _dot(upto, _bf16(band_f))
                    late = band & (rank + seen > need[:, qs])
                    sc_ref[kb, :, qs] = jnp.where(late, -jnp.inf, keys)
                    return seen + over_keys(jnp.sum, band_f)
                lax.fori_loop(0, n_blocks, drop_body, jnp.zeros((1, Q_TILE), f32))

    @pl.when(i == 0)
    def _():
        def norm_block(kb, run):
            kc = kcat_ref[0, pl.ds(pl.multiple_of(kb * KB, KB), KB), :].astype(f32)
            return jnp.maximum(run, jnp.sum(kc * kc, axis=1, keepdims=True))
        run = lax.fori_loop(0, kcat_ref.shape[1] // KB, norm_block, jnp.zeros((KB, 1), f32))
        kmax_ref[...] = jnp.broadcast_to(jnp.max(run, axis=0, keepdims=True), kmax_ref.shape)

    k_norm2 = kmax_ref[0:1, :]
    q_all = qcat_ref[0, 0]
    q_norm2 = _dot(q_all * q_all, jnp.ones((Q_CAT, LANES), jnp.bfloat16))
    qk2 = jnp.maximum(q_norm2 * k_norm2, 1e-30)
    m_ref[...] = qk2 * lax.rsqrt(qk2) * BOUND_SLACK

    chunks = [slice(c * LANES, (c + 1) * LANES) for c in range(KB // LANES)]
    head_rows = [slice(hd * TQ, (hd + 1) * TQ) for hd in range(DSA_HEADS)]

    def set_bias(kb):
        for qt in range(TQ // LANES):
            qs = slice(qt * LANES, (qt + 1) * LANES)
            bias_ref[qs, :] = jnp.where(sc_ref[kb, :, qs] >= thr[:, qs], 0.0, NEG_BIG).T

    def attend():
        def attn_block(kb, first):
            rows = pl.ds(pl.multiple_of(kb * KB, KB), KB)
            kc = kcat_ref[0, rows, :]
            vc = vcat_ref[0, rows, :]
            set_bias(kb)
            for hrows in head_rows:
                s = _dot_nt(qcat_ref[0, 0, hrows, :], kc)
                m = m_ref[hrows, :]
                p = [_bf16(jnp.exp(s[:, cols] + bias_ref[:, cols] - m)) for cols in chunks]
                pv = _dot(jnp.concatenate(p, axis=1), vc)
                acc_ref[hrows, :] = pv if first else acc_ref[hrows, :] + pv

        attn_block(0, True)

        def more(kb, carry):
            attn_block(kb, False)
            return carry
        lax.fori_loop(1, n_blocks, more, 0)
        return jnp.min(acc_ref[:, DSA_KV_RANK:DSA_KV_RANK + 1])

    @pl.when(attend() < ROW_SUM_MIN)
    def _():
        m_ref[...] = jnp.full(m_ref.shape, NEG_BIG, f32)

        def max_block(kb, carry):
            kc = kcat_ref[0, pl.ds(pl.multiple_of(kb * KB, KB), KB), :]
            set_bias(kb)
            for hrows in head_rows:
                s = _dot_nt(qcat_ref[0, 0, hrows, :], kc)
                sm = [s[:, cols] + bias_ref[:, cols] for cols in chunks]
                m_ref[hrows, :] = jnp.maximum(m_ref[hrows, :], functools.reduce(jnp.maximum, sm))
            return carry
        lax.fori_loop(0, n_blocks, max_block, 0)
        m_all = m_ref[...]
        m_ref[...] = jnp.broadcast_to(jnp.max(m_all, axis=1, keepdims=True), m_all.shape)
        attend()

    for hd, hrows in enumerate(head_rows):
        a = acc_ref[hrows, :]
        denom = jnp.broadcast_to(a[:, DSA_KV_RANK:DSA_KV_RANK + 1], (TQ, DSA_KV_RANK))
        o_ref[0, :, hd * DSA_KV_RANK:(hd + 1) * DSA_KV_RANK] = _bf16(a[:, :DSA_KV_RANK] / denom)


def _out_kernel(x_ref, out_a_ref, o_lat_ref, gate_b_ref, w_uv_ref, w_out_ref, post_g_ref, y_ref):
    group = UV_GROUP * DSA_KV_RANK
    o = jnp.concatenate([_dot(o_lat_ref[0, :, g * group:(g + 1) * group], w_uv_ref[g])
                         for g in range(DSA_HEADS // UV_GROUP)], axis=1)
    out_b = _bf16(o * gate_b_ref[0])
    y = _dot(out_a_ref[0], w_out_ref[:GMLP_WIDTH, :]) + _dot(out_b, w_out_ref[GMLP_WIDTH:, :])
    y = y * lax.rsqrt(jnp.mean(y * y, axis=-1, keepdims=True) + EPS) * post_g_ref[...]
    y_ref[0] = x_ref[0] + y


def _full(shape):
    return pl.BlockSpec(shape, lambda b, t: (0,) * len(shape))


def _layer(x, w_in, pre_g, post_g, ln_g, ln_b, w_s, b_s, qn_g, kvn_g, w_uq, w_uk, w_uv, w_q_idx, w_out):
    B, S, D = x.shape
    assert D == D_MODEL and S % KB == 0 and S % TM == 0 and TM % TQ == 0 and KB % TQ == 0
    top_k = min(TOPK_MAX, S // 4)
    assert top_k <= N_GROUPS <= KB and KB % N_GROUPS == 0 and N_GROUPS % KEY_TILE == 0
    f32, bf16 = jnp.float32, jnp.bfloat16

    c_kr = 3 * GMLP_WIDTH + DSA_Q_RANK + DSA_KV_RANK
    c_idx = c_kr + DSA_ROPE_DIM
    c_zb = c_idx + IDX_DIM + IDX_HEADS
    assert w_in.shape[1] == c_zb + DSA_WIDTH
    kr = w_in[:, c_kr:c_idx]
    kr_sw = jnp.concatenate([kr[:, ROPE_HALF:], kr[:, :ROPE_HALF]], axis=1)
    w_all = jnp.concatenate([
        w_in[:, :c_kr], jnp.tile(kr, (1, ROPE_LANE_GROUPS)), jnp.tile(kr_sw, (1, ROPE_LANE_GROUPS)),
        w_in[:, c_idx:c_zb], jnp.zeros((D, LANES - IDX_DIM - IDX_HEADS), f32), w_in[:, c_zb:]], axis=1).astype(bf16)
    assert w_all.shape[1] == _C_END
    wq3 = w_uq.reshape(DSA_Q_RANK, DSA_HEADS, DSA_NOPE_DIM + DSA_ROPE_DIM)
    nope = jnp.pad(wq3[:, :, :DSA_NOPE_DIM], ((0, 0), (0, 0), (0, LANES - DSA_NOPE_DIM)))
    rope = wq3[:, :, DSA_NOPE_DIM:]
    rope_sw = jnp.concatenate([rope[:, :, ROPE_HALF:], rope[:, :, :ROPE_HALF]], axis=2)
    wqi = jnp.pad(w_q_idx.reshape(DSA_Q_RANK, IDX_HEADS, IDX_DIM), ((0, 0), (0, 0), (0, LANES - IDX_DIM)))
    w_q = jnp.concatenate([nope.reshape(DSA_Q_RANK, -1), rope.reshape(DSA_Q_RANK, -1),
                           rope_sw.reshape(DSA_Q_RANK, -1), wqi.reshape(DSA_Q_RANK, -1)], axis=1).astype(bf16)
    w_uk_t = jnp.pad(jnp.transpose(w_uk, (1, 2, 0)), ((0, 0), (0, LANES - DSA_NOPE_DIM), (0, 0))).astype(bf16)
    eye = jnp.eye(UV_GROUP, dtype=f32)
    w_uv_g = jnp.transpose(w_uv, (1, 0, 2)).reshape(DSA_HEADS // UV_GROUP, UV_GROUP, DSA_KV_RANK, DSA_V_DIM)
    w_uv_bd = (w_uv_g[:, :, :, None, :] * eye[None, :, None, :, None]).reshape(
        DSA_HEADS // UV_GROUP, UV_GROUP * DSA_KV_RANK, UV_GROUP * DSA_V_DIM).astype(bf16)
    b_s_b = jnp.broadcast_to(b_s[:, :, None], (GMLP_GROUPS, GMLP_BLOCK, GMLP_GROUP_DIM))

    pos = jnp.arange(S, dtype=f32)
    inv_freq = ROPE_THETA ** (-jnp.arange(0, DSA_ROPE_DIM, 2, dtype=f32) / DSA_ROPE_DIM)
    ang = pos[:, None] * inv_freq[None, :]
    cos_t = jnp.tile(jnp.concatenate([jnp.cos(ang), jnp.cos(ang)], axis=1), (1, DSA_HEADS))
    sin_t = jnp.tile(jnp.concatenate([-jnp.sin(ang), jnp.sin(ang)], axis=1), (1, DSA_HEADS))

    row2 = lambda a: a.reshape(1, -1)
    tok = lambda width: pl.BlockSpec((1, TM, width), lambda b, t: (b, t, 0))
    hm = lambda width: pl.BlockSpec((1, DSA_HEADS, TM, width), lambda b, t: (b, 0, t, 0))

    out_a, gate_b, qidx, widx, qcat, kidx, kcat, vcat = pl.pallas_call(
        _proj_kernel,
        grid=(B, S // TM),
        in_specs=[tok(D), _full((1, D)), _full((D, _C_END)), _full((1, GMLP_WIDTH)), _full((1, GMLP_WIDTH)),
                  _full((GMLP_GROUPS, GMLP_BLOCK, GMLP_BLOCK)), _full((GMLP_GROUPS, GMLP_BLOCK, GMLP_GROUP_DIM)),
                  _full((1, DSA_Q_RANK)), _full((1, DSA_KV_RANK)), _full((DSA_Q_RANK, _Q_END)),
                  _full((DSA_HEADS, LANES, DSA_KV_RANK)),
                  pl.BlockSpec((TM, 2 * LANES), lambda b, t: (t, 0)),
                  pl.BlockSpec((TM, 2 * LANES), lambda b, t: (t, 0))],
        out_specs=[tok(GMLP_WIDTH), tok(DSA_WIDTH), hm(LANES),
                   pl.BlockSpec((1, LANES, TM), lambda b, t: (b, 0, t)),
                   pl.BlockSpec((1, TM // TQ, DSA_HEADS * TQ, Q_CAT), lambda b, t: (b, t, 0, 0)),
                   tok(LANES), tok(Q_CAT), tok(Q_CAT)],
        out_shape=[jax.ShapeDtypeStruct((B, S, GMLP_WIDTH), bf16),
                   jax.ShapeDtypeStruct((B, S, DSA_WIDTH), f32),
                   jax.ShapeDtypeStruct((B, DSA_HEADS, S, LANES), bf16),
                   jax.ShapeDtypeStruct((B, LANES, S), f32),
                   jax.ShapeDtypeStruct((B, S // TQ, DSA_HEADS * TQ, Q_CAT), bf16),
                   jax.ShapeDtypeStruct((B, S, LANES), bf16),
                   jax.ShapeDtypeStruct((B, S, Q_CAT), bf16),
                   jax.ShapeDtypeStruct((B, S, Q_CAT), bf16)],
        compiler_params=pltpu.CompilerParams(
            dimension_semantics=("arbitrary", "arbitrary"), vmem_limit_bytes=48 * 1024 * 1024),
        name="proj_gmlp_dsa_prep",
    )(x, row2(pre_g), w_all, row2(ln_g), row2(ln_b), w_s, b_s_b, row2(qn_g), row2(kvn_g), w_q, w_uk_t,
      cos_t, sin_t)

    qt = lambda width: pl.BlockSpec((1, DSA_HEADS, TQ, width), lambda b, t: (b, 0, t, 0))
    seq = lambda width: pl.BlockSpec((1, S, width), lambda b, t: (b, 0, 0), pipeline_mode=pl.Buffered(1))
    o_lat = pl.pallas_call(
        functools.partial(_dsa_kernel, top_k=top_k),
        grid=(B, S // TQ),
        in_specs=[qt(LANES), pl.BlockSpec((1, LANES, TQ), lambda b, t: (b, 0, t)),
                  pl.BlockSpec((1, 1, DSA_HEADS * TQ, Q_CAT), lambda b, t: (b, t, 0, 0)),
                  seq(LANES), seq(Q_CAT), seq(Q_CAT)],
        out_specs=pl.BlockSpec((1, TQ, DSA_HEADS * DSA_KV_RANK), lambda b, t: (b, t, 0)),
        out_shape=jax.ShapeDtypeStruct((B, S, DSA_HEADS * DSA_KV_RANK), bf16),
        scratch_shapes=[pltpu.VMEM((S // KB, KB, TQ), f32),
                        pltpu.VMEM((N_GROUPS, TQ), f32),
                        pltpu.VMEM((TQ, KB), f32),
                        pltpu.VMEM((DSA_HEADS * TQ, LANES), f32),
                        pltpu.VMEM((DSA_HEADS * TQ, Q_CAT), f32),
                        pltpu.VMEM((8, LANES), f32)],
        compiler_params=pltpu.CompilerParams(
            dimension_semantics=("arbitrary", "arbitrary"), vmem_limit_bytes=56 * 1024 * 1024),
        name="dsa_index_select_attend",
    )(qidx, widx, qcat, kidx, kcat, vcat)

    return pl.pallas_call(
        _out_kernel,
        grid=(B, S // TM),
        in_specs=[tok(D), tok(GMLP_WIDTH), tok(DSA_HEADS * DSA_KV_RANK), tok(DSA_WIDTH),
                  _full((DSA_HEADS // UV_GROUP, UV_GROUP * DSA_KV_RANK, UV_GROUP * DSA_V_DIM)),
                  _full((GMLP_WIDTH + DSA_WIDTH, D)), _full((1, D))],
        out_specs=tok(D),
        out_shape=jax.ShapeDtypeStruct((B, S, D), f32),
        compiler_params=pltpu.CompilerParams(
            dimension_semantics=("arbitrary", "arbitrary"), vmem_limit_bytes=32 * 1024 * 1024),
        name="out_proj_norm_residual",
    )(x, out_a, o_lat, gate_b, w_uv_bd, w_out.astype(bf16), row2(post_g))


def kernel(x, w_in, pre_norm_g, post_norm_g, gmlp_ln_g, gmlp_ln_b, gmlp_w_s, gmlp_b_s, dsa_q_norm_g, dsa_kv_norm_g, dsa_w_uq, dsa_w_uk, dsa_w_uv, dsa_w_q_idx, w_out):
    for l in range(w_in.shape[0]):
        x = _layer(x, w_in[l], pre_norm_g[l], post_norm_g[l], gmlp_ln_g[l], gmlp_ln_b[l], gmlp_w_s[l],
                   gmlp_b_s[l], dsa_q_norm_g[l], dsa_kv_norm_g[l], dsa_w_uq[l], dsa_w_uk[l], dsa_w_uv[l],
                   dsa_w_q_idx[l], w_out[l])
    return x
```

```python
import functools
import math

import jax
import jax.numpy as jnp
import numpy as np
from jax import lax
from jax.experimental import pallas as pl
from jax.experimental.pallas import tpu as pltpu

D_MODEL = 1024
CHUNK = 64
EPS = 1e-6
GMLP_GROUPS = 4
GMLP_GROUP_DIM = 128
GMLP_WIDTH = GMLP_GROUPS * GMLP_GROUP_DIM
GMLP_BLOCK = 128
DSA_HEADS = 8
DSA_V_DIM = 64
DSA_WIDTH = DSA_HEADS * DSA_V_DIM
DSA_NOPE_DIM = 64
DSA_ROPE_DIM = 32
DSA_Q_RANK = 256
DSA_KV_RANK = 128
IDX_HEADS = 8
IDX_DIM = 64
TOPK_MAX = 256
ROPE_THETA = 10000.0

LANES = 128
ROPE_HALF = DSA_ROPE_DIM // 2
ROPE_LANE_GROUPS = LANES // DSA_ROPE_DIM
Q_CAT = DSA_KV_RANK + LANES
W_IDX_LANE = IDX_DIM
UV_GROUP = LANES // DSA_V_DIM

TM = 512
TQ = 512
KB = 512
KEY_TILE = 128
Q_TILE = 256
COUNT_ROWS = 32
N_GROUPS = 256
NEG_BIG = -1e30
ROW_SUM_MIN = 1e-26
BOUND_SLACK = 1.02
KEY_TINY = 0x00800000
KEY_LOWEST = -0x7F800000
FIRST_PIVOTS = (0.8, 0.0, 0.4)
MAX_BISECT = 40

_C_UV, _C_ZA, _C_CQ, _C_CKV = 0, 1024, 1536, 1792
_C_KR, _C_KRS, _C_SLAB, _C_ZB, _C_END = 1920, 2048, 2176, 2304, 2816
_Q_NOPE, _Q_ROPE, _Q_ROPES, _Q_IDX, _Q_END = 0, 1024, 1280, 1536, 2560

_NT = (((1,), (1,)), ((), ()))


def _bf16(a):
    return a.astype(jnp.bfloat16)


def _dot(a, b):
    return jnp.dot(a, b, preferred_element_type=jnp.float32)


def _dot_nt(a, b):
    return lax.dot_general(a, b, _NT, preferred_element_type=jnp.float32)


def _chunk_of(pos):
    return lax.shift_right_logical(pos, jnp.int32(CHUNK.bit_length() - 1))


def _sort_key(s):
    bits = pltpu.bitcast(s, jnp.int32)
    return jnp.where(bits < 0, bits ^ jnp.int32(0x7FFFFFFF), bits)


def _key_to_float(key):
    return pltpu.bitcast(jnp.where(key < 0, key ^ jnp.int32(0x7FFFFFFF), key), jnp.float32)


def _silu(z):
    return z / (1.0 + jnp.exp(-z))


def _gelu_exact(a):
    return 0.5 * a * (1.0 + lax.erf(a * np.float32(math.sqrt(0.5))))


def _proj_kernel(x_ref, pre_g_ref, w_all_ref, ln_g_ref, ln_b_ref, w_s_ref, b_s_ref,
                 qn_g_ref, kvn_g_ref, w_q_ref, w_uk_ref, cos_ref, sin_ref,
                 out_a_ref, gate_b_ref, qidx_ref, widx_ref, qcat_ref, kidx_ref, kcat_ref, vcat_ref):
    x = x_ref[0]
    h = x * lax.rsqrt(jnp.mean(x * x, axis=-1, keepdims=True) + EPS) * pre_g_ref[...]
    proj = _dot(_bf16(h), w_all_ref[...])

    uv = _gelu_exact(proj[:, _C_UV:_C_ZA])
    u, v = uv[:, :GMLP_WIDTH], uv[:, GMLP_WIDTH:]
    mu = jnp.mean(v, axis=-1, keepdims=True)
    vc = v - mu
    var = jnp.mean(vc * vc, axis=-1, keepdims=True)
    vn = _bf16(vc * lax.rsqrt(var + EPS) * ln_g_ref[...] + ln_b_ref[...])
    gate_a = _silu(proj[:, _C_ZA:_C_CQ])
    t_chunk = _chunk_of(lax.broadcasted_iota(jnp.int32, (GMLP_BLOCK, GMLP_BLOCK), 0))
    s_chunk = _chunk_of(lax.broadcasted_iota(jnp.int32, (GMLP_BLOCK, GMLP_BLOCK), 1))
    for g in range(GMLP_GROUPS):
        w_g = _bf16(jnp.where(s_chunk <= t_chunk, w_s_ref[g], 0.0))
        cols = slice(g * GMLP_GROUP_DIM, (g + 1) * GMLP_GROUP_DIM)
        for r in range(TM // GMLP_BLOCK):
            rows = slice(r * GMLP_BLOCK, (r + 1) * GMLP_BLOCK)
            y = _dot(w_g, vn[rows, cols]) + b_s_ref[g]
            out_a_ref[0, rows, cols] = _bf16(u[rows, cols] * y * gate_a[rows, cols])

    gate_b_ref[0] = _silu(proj[:, _C_ZB:_C_END])

    c_q = proj[:, _C_CQ:_C_CKV]
    c_q = c_q * lax.rsqrt(jnp.mean(c_q * c_q, axis=-1, keepdims=True) + EPS) * qn_g_ref[...]
    c_kv = proj[:, _C_CKV:_C_KR]
    c_kv = _bf16(c_kv * lax.rsqrt(jnp.mean(c_kv * c_kv, axis=-1, keepdims=True) + EPS) * kvn_g_ref[...])
    cos = cos_ref[...]
    sin = sin_ref[...]
    k_rope = proj[:, _C_KR:_C_KRS] * cos[:, :LANES] + proj[:, _C_KRS:_C_SLAB] * sin[:, :LANES]
    slab = proj[:, _C_SLAB:_C_ZB]
    kidx_ref[0] = _bf16(slab)
    widx_ref[0] = slab.T
    kcat_ref[0, :, :DSA_KV_RANK] = c_kv
    kcat_ref[0, :, DSA_KV_RANK:] = _bf16(k_rope)
    lane = lax.broadcasted_iota(jnp.int32, (TM, LANES), 1)
    vcat_ref[0, :, :DSA_KV_RANK] = c_kv
    vcat_ref[0, :, DSA_KV_RANK:] = jnp.where(lane == 0, 1.0, 0.0).astype(jnp.bfloat16)

    q_all = _dot(_bf16(c_q), w_q_ref[...])
    q_rope = q_all[:, _Q_ROPE:_Q_ROPES] * cos + q_all[:, _Q_ROPES:_Q_IDX] * sin
    scale = np.float32(1.0 / math.sqrt(DSA_NOPE_DIM + DSA_ROPE_DIM))
    for hd in range(DSA_HEADS):
        cols = slice(_Q_NOPE + hd * LANES, _Q_NOPE + (hd + 1) * LANES)
        q_lat = _dot(_bf16(q_all[:, cols]), w_uk_ref[hd])
        grp, sub = divmod(hd, ROPE_LANE_GROUPS)
        own = (lane >= sub * DSA_ROPE_DIM) & (lane < (sub + 1) * DSA_ROPE_DIM)
        q_r = jnp.where(own, q_rope[:, grp * LANES:(grp + 1) * LANES] * scale, 0.0)
        for t in range(TM // TQ):
            src = slice(t * TQ, (t + 1) * TQ)
            dst = slice(hd * TQ, (hd + 1) * TQ)
            qcat_ref[0, t, dst, :DSA_KV_RANK] = _bf16(q_lat[src] * scale)
            qcat_ref[0, t, dst, DSA_KV_RANK:] = _bf16(q_r[src])
        icol = slice(_Q_IDX + hd * LANES, _Q_IDX + (hd + 1) * LANES)
        qidx_ref[0, hd] = _bf16(q_all[:, icol])


def _dsa_kernel(qidx_ref, widx_ref, qcat_ref, kidx_ref, kcat_ref, vcat_ref, o_ref,
                sc_ref, gmax_ref, bias_ref, m_ref, acc_ref, kmax_ref, *, top_k):
    i = pl.program_id(1)
    n_blocks = (i * TQ) // KB + 1
    idx_scale = np.float32(IDX_HEADS ** -0.5 * IDX_DIM ** -0.5)
    f32 = jnp.float32

    q_chunk = _chunk_of(i * TQ + lax.broadcasted_iota(jnp.int32, (1, TQ), 1))
    key_t = lax.broadcasted_iota(jnp.int32, (KEY_TILE, Q_TILE), 0)

    def over_keys(op, a):
        return op(a, axis=0, keepdims=True)

    w = widx_ref[0] * idx_scale
    gmax_ref[...] = jnp.full(gmax_ref.shape, -jnp.inf, f32)

    def score_block(kb, carry):
        for ks in range(KB // KEY_TILE):
            k = kidx_ref[0, pl.ds(pl.multiple_of(kb * KB + ks * KEY_TILE, KEY_TILE), KEY_TILE), :]
            krows = slice(ks * KEY_TILE, (ks + 1) * KEY_TILE)
            grows = slice((ks * KEY_TILE) % N_GROUPS, (ks * KEY_TILE) % N_GROUPS + KEY_TILE)
            for qt in range(TQ // Q_TILE):
                qs = slice(qt * Q_TILE, (qt + 1) * Q_TILE)
                acc = jnp.zeros((KEY_TILE, Q_TILE), f32)
                for hd in range(IDX_HEADS):
                    logit = _dot_nt(k, qidx_ref[0, hd, qs, :])
                    acc = acc + w[W_IDX_LANE + hd:W_IDX_LANE + hd + 1, qs] * jnp.maximum(logit, 0.0)
                adm = _chunk_of(kb * KB + ks * KEY_TILE + key_t) <= q_chunk[:, qs]
                sc_ref[kb, krows, qs] = jnp.where(adm, acc, -jnp.inf)
                gmax_ref[grows, qs] = jnp.maximum(gmax_ref[grows, qs], jnp.where(adm, acc, -jnp.inf))
        return carry
    lax.fori_loop(0, n_blocks, score_block, 0)

    kf = np.float32(top_k)

    def count_ge(mid):
        mid_f = _key_to_float(mid)

        def body(kb, cnt):
            for r in range(KB // COUNT_ROWS):
                cnt = cnt + jnp.where(sc_ref[kb, r * COUNT_ROWS:(r + 1) * COUNT_ROWS, :] >= mid_f, 1.0, 0.0)
            return cnt
        cnt = lax.fori_loop(0, n_blocks - 1, body, jnp.zeros((COUNT_ROWS, TQ), f32))
        for r in range(KB // COUNT_ROWS):
            lane0 = (r * COUNT_ROWS // CHUNK) * CHUNK // LANES * LANES
            rows = slice(r * COUNT_ROWS, (r + 1) * COUNT_ROWS)
            hit = jnp.where(sc_ref[n_blocks - 1, rows, lane0:] >= mid_f[:, lane0:], 1.0, 0.0)
            cnt = jnp.concatenate([cnt[:, :lane0], cnt[:, lane0:] + hit], axis=1) if lane0 else cnt + hit
        return over_keys(jnp.sum, cnt)

    def canonical(key):
        return jnp.where((key > 0) & (key < KEY_TINY), KEY_TINY, jnp.where((key < 0) & (key >= -KEY_TINY), 0, key))

    def pivot(lo, hi):
        mid = canonical((lo & hi) + lax.shift_right_arithmetic(lo ^ hi, jnp.int32(1)))
        return jnp.where((mid <= lo) | (mid >= hi), lo, mid)

    def pivot_zero_first(lo, hi):
        return jnp.where((lo <= 0) & (hi > KEY_TINY), KEY_TINY,
                         jnp.where((lo < 0) & (hi == KEY_TINY), 0, pivot(lo, hi)))

    grp = gmax_ref[...]
    grp_min = over_keys(jnp.min, grp)
    select_all = (q_chunk + 1) * CHUNK <= top_k
    lo = jnp.where(select_all, KEY_LOWEST, canonical(_sort_key(grp_min)))
    hi = jnp.where(select_all, KEY_LOWEST, canonical(_sort_key(over_keys(jnp.max, grp)) + 1))

    def step(lo, hi, mid):
        cnt = count_ge(mid)
        ge = cnt >= kf
        return jnp.where(ge, mid, lo), jnp.where(cnt == kf, mid, jnp.where(ge, hi, mid))

    grp_mean = over_keys(jnp.sum, grp) / np.float32(N_GROUPS)
    grp_var = over_keys(jnp.sum, grp * grp) / np.float32(N_GROUPS) - grp_mean * grp_mean
    grp_std = jnp.sqrt(jnp.maximum(grp_var, 0.0))
    for below in FIRST_PIVOTS:
        guess = canonical(_sort_key(grp_mean - np.float32(below) * grp_std))
        usable = (grp_min > -jnp.inf) & (guess > lo) & (guess < hi)
        lo, hi = step(lo, hi, jnp.where(usable, guess, pivot(lo, hi)))
    for _ in range(2):
        lo, hi = step(lo, hi, pivot_zero_first(lo, hi))

    def bisect_cond(state):
        it, lo, _, mid = state
        return jnp.logical_and(it < MAX_BISECT, jnp.max(jnp.where(mid != lo, 1.0, 0.0)) > 0.0)

    def bisect_body(state):
        it, lo, hi, mid = state
        lo, hi = step(lo, hi, mid)
        return it + 1, lo, hi, pivot(lo, hi)

    _, lo, hi, _ = lax.while_loop(bisect_cond, bisect_body, (jnp.int32(0), lo, hi, pivot(lo, hi)))
    thr = _key_to_float(lo)
    tie = hi != lo

    @pl.when(jnp.max(jnp.where(tie, 1.0, 0.0)) > 0.0)
    def _():
        need = kf - count_ge(hi)
        upto = (lax.broadcasted_iota(jnp.int32, (KB, KB), 1)
                <= lax.broadcasted_iota(jnp.int32, (KB, KB), 0)).astype(jnp.bfloat16)

        for qt in range(TQ // Q_TILE):
            qs = slice(qt * Q_TILE, (qt + 1) * Q_TILE)

            @pl.when(jnp.max(jnp.where(tie[:, qs], 1.0, 0.0)) > 0.0)
            def _(qs=qs):
                def drop_body(kb, seen):
                    keys = sc_ref[kb, :, qs]
                    band = tie[:, qs] & (keys == thr[:, qs])
                    band_f = jnp.where(band, 1.0, 0.0)
                    rank = _dot(upto, _bf16(band_f))
                    late = band & (rank + seen > need[:, qs])
                    sc_ref[kb, :, qs] = jnp.where(late, -jnp.inf, keys)
                    return seen + over_keys(jnp.sum, band_f)
                lax.fori_loop(0, n_blocks, drop_body, jnp.zeros((1, Q_TILE), f32))

    @pl.when(i == 0)
    def _():
        def norm_block(kb, run):
            kc = kcat_ref[0, pl.ds(pl.multiple_of(kb * KB, KB), KB), :].astype(f32)
            return jnp.maximum(run, jnp.sum(kc * kc, axis=1, keepdims=True))
        run = lax.fori_loop(0, kcat_ref.shape[1] // KB, norm_block, jnp.zeros((KB, 1), f32))
        kmax_ref[...] = jnp.broadcast_to(jnp.max(run, axis=0, keepdims=True), kmax_ref.shape)

    k_norm2 = kmax_ref[0:1, :]
    q_all = qcat_ref[0, 0]
    q_norm2 = _dot(q_all * q_all, jnp.ones((Q_CAT, LANES), jnp.bfloat16))
    qk2 = jnp.maximum(q_norm2 * k_norm2, 1e-30)
    m_ref[...] = qk2 * lax.rsqrt(qk2) * BOUND_SLACK

    chunks = [slice(c * LANES, (c + 1) * LANES) for c in range(KB // LANES)]
    head_rows = [slice(hd * TQ, (hd + 1) * TQ) for hd in range(DSA_HEADS)]

    def set_bias(kb):
        for qt in range(TQ // LANES):
            qs = slice(qt * LANES, (qt + 1) * LANES)
            bias_ref[qs, :] = jnp.where(sc_ref[kb, :, qs] >= thr[:, qs], 0.0, NEG_BIG).T

    def attend():
        def attn_block(kb, first):
            rows = pl.ds(pl.multiple_of(kb * KB, KB), KB)
            kc = kcat_ref[0, rows, :]
            vc = vcat_ref[0, rows, :]
            set_bias(kb)
            for hrows in head_rows:
                s = _dot_nt(qcat_ref[0, 0, hrows, :], kc)
                m = m_ref[hrows, :]
                p = [_bf16(jnp.exp(s[:, cols] + bias_ref[:, cols] - m)) for cols in chunks]
                pv = _dot(jnp.concatenate(p, axis=1), vc)
                acc_ref[hrows, :] = pv if first else acc_ref[hrows, :] + pv

        attn_block(0, True)

        def more(kb, carry):
            attn_block(kb, False)
            return carry
        lax.fori_loop(1, n_blocks, more, 0)
        return jnp.min(acc_ref[:, DSA_KV_RANK:DSA_KV_RANK + 1])

    @pl.when(attend() < ROW_SUM_MIN)
    def _():
        m_ref[...] = jnp.full(m_ref.shape, NEG_BIG, f32)

        def max_block(kb, carry):
            kc = kcat_ref[0, pl.ds(pl.multiple_of(kb * KB, KB), KB), :]
            set_bias(kb)
            for hrows in head_rows:
                s = _dot_nt(qcat_ref[0, 0, hrows, :], kc)
                sm = [s[:, cols] + bias_ref[:, cols] for cols in chunks]
                m_ref[hrows, :] = jnp.maximum(m_ref[hrows, :], functools.reduce(jnp.maximum, sm))
            return carry
        lax.fori_loop(0, n_blocks, max_block, 0)
        m_all = m_ref[...]
        m_ref[...] = jnp.broadcast_to(jnp.max(m_all, axis=1, keepdims=True), m_all.shape)
        attend()

    for hd, hrows in enumerate(head_rows):
        a = acc_ref[hrows, :]
        denom = jnp.broadcast_to(a[:, DSA_KV_RANK:DSA_KV_RANK + 1], (TQ, DSA_KV_RANK))
        o_ref[0, :, hd * DSA_KV_RANK:(hd + 1) * DSA_KV_RANK] = _bf16(a[:, :DSA_KV_RANK] / denom)


def _out_kernel(x_ref, out_a_ref, o_lat_ref, gate_b_ref, w_uv_ref, w_out_ref, post_g_ref, y_ref):
    group = UV_GROUP * DSA_KV_RANK
    o = jnp.concatenate([_dot(o_lat_ref[0, :, g * group:(g + 1) * group], w_uv_ref[g])
                         for g in range(DSA_HEADS // UV_GROUP)], axis=1)
    out_b = _bf16(o * gate_b_ref[0])
    y = _dot(out_a_ref[0], w_out_ref[:GMLP_WIDTH, :]) + _dot(out_b, w_out_ref[GMLP_WIDTH:, :])
    y = y * lax.rsqrt(jnp.mean(y * y, axis=-1, keepdims=True) + EPS) * post_g_ref[...]
    y_ref[0] = x_ref[0] + y


def _full(shape):
    return pl.BlockSpec(shape, lambda b, t: (0,) * len(shape))


def _layer(x, w_in, pre_g, post_g, ln_g, ln_b, w_s, b_s, qn_g, kvn_g, w_uq, w_uk, w_uv, w_q_idx, w_out):
    B, S, D = x.shape
    assert D == D_MODEL and S % KB == 0 and S % TM == 0 and TM % TQ == 0 and KB == TQ
    top_k = min(TOPK_MAX, S // 4)
    assert top_k <= N_GROUPS <= KB and KB % N_GROUPS == 0 and N_GROUPS % KEY_TILE == 0
    f32, bf16 = jnp.float32, jnp.bfloat16

    c_kr = 3 * GMLP_WIDTH + DSA_Q_RANK + DSA_KV_RANK
    c_idx = c_kr + DSA_ROPE_DIM
    c_zb = c_idx + IDX_DIM + IDX_HEADS
    assert w_in.shape[1] == c_zb + DSA_WIDTH
    kr = w_in[:, c_kr:c_idx]
    kr_sw = jnp.concatenate([kr[:, ROPE_HALF:], kr[:, :ROPE_HALF]], axis=1)
    w_all = jnp.concatenate([
        w_in[:, :c_kr], jnp.tile(kr, (1, ROPE_LANE_GROUPS)), jnp.tile(kr_sw, (1, ROPE_LANE_GROUPS)),
        w_in[:, c_idx:c_zb], jnp.zeros((D, LANES - IDX_DIM - IDX_HEADS), f32), w_in[:, c_zb:]], axis=1).astype(bf16)
    assert w_all.shape[1] == _C_END
    wq3 = w_uq.reshape(DSA_Q_RANK, DSA_HEADS, DSA_NOPE_DIM + DSA_ROPE_DIM)
    nope = jnp.pad(wq3[:, :, :DSA_NOPE_DIM], ((0, 0), (0, 0), (0, LANES - DSA_NOPE_DIM)))
    rope = wq3[:, :, DSA_NOPE_DIM:]
    rope_sw = jnp.concatenate([rope[:, :, ROPE_HALF:], rope[:, :, :ROPE_HALF]], axis=2)
    wqi = jnp.pad(w_q_idx.reshape(DSA_Q_RANK, IDX_HEADS, IDX_DIM), ((0, 0), (0, 0), (0, LANES - IDX_DIM)))
    w_q = jnp.concatenate([nope.reshape(DSA_Q_RANK, -1), rope.reshape(DSA_Q_RANK, -1),
                           rope_sw.reshape(DSA_Q_RANK, -1), wqi.reshape(DSA_Q_RANK, -1)], axis=1).astype(bf16)
    w_uk_t = jnp.pad(jnp.transpose(w_uk, (1, 2, 0)), ((0, 0), (0, LANES - DSA_NOPE_DIM), (0, 0))).astype(bf16)
    eye = jnp.eye(UV_GROUP, dtype=f32)
    w_uv_g = jnp.transpose(w_uv, (1, 0, 2)).reshape(DSA_HEADS // UV_GROUP, UV_GROUP, DSA_KV_RANK, DSA_V_DIM)
    w_uv_bd = (w_uv_g[:, :, :, None, :] * eye[None, :, None, :, None]).reshape(
        DSA_HEADS // UV_GROUP, UV_GROUP * DSA_KV_RANK, UV_GROUP * DSA_V_DIM).astype(bf16)
    b_s_b = jnp.broadcast_to(b_s[:, :, None], (GMLP_GROUPS, GMLP_BLOCK, GMLP_GROUP_DIM))

    pos = jnp.arange(S, dtype=f32)
    inv_freq = ROPE_THETA ** (-jnp.arange(0, DSA_ROPE_DIM, 2, dtype=f32) / DSA_ROPE_DIM)
    ang = pos[:, None] * inv_freq[None, :]
    cos_t = jnp.tile(jnp.concatenate([jnp.cos(ang), jnp.cos(ang)], axis=1), (1, DSA_HEADS))
    sin_t = jnp.tile(jnp.concatenate([-jnp.sin(ang), jnp.sin(ang)], axis=1), (1, DSA_HEADS))

    row2 = lambda a: a.reshape(1, -1)
    tok = lambda width: pl.BlockSpec((1, TM, width), lambda b, t: (b, t, 0))
    hm = lambda width: pl.BlockSpec((1, DSA_HEADS, TM, width), lambda b, t: (b, 0, t, 0))

    out_a, gate_b, qidx, widx, qcat, kidx, kcat, vcat = pl.pallas_call(
        _proj_kernel,
        grid=(B, S // TM),
        in_specs=[tok(D), _full((1, D)), _full((D, _C_END)), _full((1, GMLP_WIDTH)), _full((1, GMLP_WIDTH)),
                  _full((GMLP_GROUPS, GMLP_BLOCK, GMLP_BLOCK)), _full((GMLP_GROUPS, GMLP_BLOCK, GMLP_GROUP_DIM)),
                  _full((1, DSA_Q_RANK)), _full((1, DSA_KV_RANK)), _full((DSA_Q_RANK, _Q_END)),
                  _full((DSA_HEADS, LANES, DSA_KV_RANK)),
                  pl.BlockSpec((TM, 2 * LANES), lambda b, t: (t, 0)),
                  pl.BlockSpec((TM, 2 * LANES), lambda b, t: (t, 0))],
        out_specs=[tok(GMLP_WIDTH), tok(DSA_WIDTH), hm(LANES),
                   pl.BlockSpec((1, LANES, TM), lambda b, t: (b, 0, t)),
                   pl.BlockSpec((1, TM // TQ, DSA_HEADS * TQ, Q_CAT), lambda b, t: (b, t, 0, 0)),
                   tok(LANES), tok(Q_CAT), tok(Q_CAT)],
        out_shape=[jax.ShapeDtypeStruct((B, S, GMLP_WIDTH), bf16),
                   jax.ShapeDtypeStruct((B, S, DSA_WIDTH), f32),
                   jax.ShapeDtypeStruct((B, DSA_HEADS, S, LANES), bf16),
                   jax.ShapeDtypeStruct((B, LANES, S), f32),
                   jax.ShapeDtypeStruct((B, S // TQ, DSA_HEADS * TQ, Q_CAT), bf16),
                   jax.ShapeDtypeStruct((B, S, LANES), bf16),
                   jax.ShapeDtypeStruct((B, S, Q_CAT), bf16),
                   jax.ShapeDtypeStruct((B, S, Q_CAT), bf16)],
        compiler_params=pltpu.CompilerParams(
            dimension_semantics=("arbitrary", "arbitrary"), vmem_limit_bytes=48 * 1024 * 1024),
        name="proj_gmlp_dsa_prep",
    )(x, row2(pre_g), w_all, row2(ln_g), row2(ln_b), w_s, b_s_b, row2(qn_g), row2(kvn_g), w_q, w_uk_t,
      cos_t, sin_t)

    qt = lambda width: pl.BlockSpec((1, DSA_HEADS, TQ, width), lambda b, t: (b, 0, t, 0))
    seq = lambda width: pl.BlockSpec((1, S, width), lambda b, t: (b, 0, 0), pipeline_mode=pl.Buffered(1))
    o_lat = pl.pallas_call(
        functools.partial(_dsa_kernel, top_k=top_k),
        grid=(B, S // TQ),
        in_specs=[qt(LANES), pl.BlockSpec((1, LANES, TQ), lambda b, t: (b, 0, t)),
                  pl.BlockSpec((1, 1, DSA_HEADS * TQ, Q_CAT), lambda b, t: (b, t, 0, 0)),
                  seq(LANES), seq(Q_CAT), seq(Q_CAT)],
        out_specs=pl.BlockSpec((1, TQ, DSA_HEADS * DSA_KV_RANK), lambda b, t: (b, t, 0)),
        out_shape=jax.ShapeDtypeStruct((B, S, DSA_HEADS * DSA_KV_RANK), bf16),
        scratch_shapes=[pltpu.VMEM((S // KB, KB, TQ), f32),
                        pltpu.VMEM((N_GROUPS, TQ), f32),
                        pltpu.VMEM((TQ, KB), f32),
                        pltpu.VMEM((DSA_HEADS * TQ, LANES), f32),
                        pltpu.VMEM((DSA_HEADS * TQ, Q_CAT), f32),
                        pltpu.VMEM((8, LANES), f32)],
        compiler_params=pltpu.CompilerParams(
            dimension_semantics=("arbitrary", "arbitrary"), vmem_limit_bytes=56 * 1024 * 1024),
        name="dsa_index_select_attend",
    )(qidx, widx, qcat, kidx, kcat, vcat)

    return pl.pallas_call(
        _out_kernel,
        grid=(B, S // TM),
        in_specs=[tok(D), tok(GMLP_WIDTH), tok(DSA_HEADS * DSA_KV_RANK), tok(DSA_WIDTH),
                  _full((DSA_HEADS // UV_GROUP, UV_GROUP * DSA_KV_RANK, UV_GROUP * DSA_V_DIM)),
                  _full((GMLP_WIDTH + DSA_WIDTH, D)), _full((1, D))],
        out_specs=tok(D),
        out_shape=jax.ShapeDtypeStruct((B, S, D), f32),
        compiler_params=pltpu.CompilerParams(
            dimension_semantics=("arbitrary", "arbitrary"), vmem_limit_bytes=32 * 1024 * 1024),
        name="out_proj_norm_residual",
    )(x, out_a, o_lat, gate_b, w_uv_bd, w_out.astype(bf16), row2(post_g))


def kernel(x, w_in, pre_norm_g, post_norm_g, gmlp_ln_g, gmlp_ln_b, gmlp_w_s, gmlp_b_s, dsa_q_norm_g, dsa_kv_norm_g, dsa_w_uq, dsa_w_uk, dsa_w_uv, dsa_w_q_idx, w_out):
    for l in range(w_in.shape[0]):
        x = _layer(x, w_in[l], pre_norm_g[l], post_norm_g[l], gmlp_ln_g[l], gmlp_ln_b[l], gmlp_w_s[l],
                   gmlp_b_s[l], dsa_q_norm_g[l], dsa_kv_norm_g[l], dsa_w_uq[l], dsa_w_uk[l], dsa_w_uv[l],
                   dsa_w_q_idx[l], w_out[l])
    return x
```

```python
import functools
import math

import jax
import jax.numpy as jnp
import numpy as np
from jax import lax
from jax.experimental import pallas as pl
from jax.experimental.pallas import tpu as pltpu

D_MODEL = 1024
CHUNK = 64
EPS = 1e-6
GMLP_GROUPS = 4
GMLP_GROUP_DIM = 128
GMLP_WIDTH = GMLP_GROUPS * GMLP_GROUP_DIM
GMLP_BLOCK = 128
DSA_HEADS = 8
DSA_V_DIM = 64
DSA_WIDTH = DSA_HEADS * DSA_V_DIM
DSA_NOPE_DIM = 64
DSA_ROPE_DIM = 32
DSA_Q_RANK = 256
DSA_KV_RANK = 128
IDX_HEADS = 8
IDX_DIM = 64
TOPK_MAX = 256
ROPE_THETA = 10000.0

LANES = 128
ROPE_HALF = DSA_ROPE_DIM // 2
ROPE_LANE_GROUPS = LANES // DSA_ROPE_DIM
Q_CAT = DSA_KV_RANK + LANES
W_IDX_LANE = IDX_DIM
UV_GROUP = LANES // DSA_V_DIM

TM = 512
TQ = 512
KB = 512
KEY_TILE = 128
Q_TILE = 256
COUNT_ROWS = 32
N_GROUPS = 256
NEG_BIG = -1e30
ROW_SUM_MIN = 1e-26
BOUND_SLACK = 1.02
KEY_TINY = 0x00800000
KEY_LOWEST = -0x7F800000
FIRST_PIVOTS = (0.8, 0.0, 0.4)
MAX_BISECT = 40

_C_UV, _C_ZA, _C_CQ, _C_CKV = 0, 1024, 1536, 1792
_C_KR, _C_KRS, _C_SLAB, _C_ZB, _C_END = 1920, 2048, 2176, 2304, 2816
_Q_NOPE, _Q_ROPE, _Q_ROPES, _Q_IDX, _Q_END = 0, 1024, 1280, 1536, 2560

_NT = (((1,), (1,)), ((), ()))


def _bf16(a):
    return a.astype(jnp.bfloat16)


def _dot(a, b):
    return jnp.dot(a, b, preferred_element_type=jnp.float32)


def _dot_nt(a, b):
    return lax.dot_general(a, b, _NT, preferred_element_type=jnp.float32)


def _chunk_of(pos):
    return lax.shift_right_logical(pos, jnp.int32(CHUNK.bit_length() - 1))


def _sort_key(s):
    bits = pltpu.bitcast(s, jnp.int32)
    return jnp.where(bits < 0, bits ^ jnp.int32(0x7FFFFFFF), bits)


def _key_to_float(key):
    return pltpu.bitcast(jnp.where(key < 0, key ^ jnp.int32(0x7FFFFFFF), key), jnp.float32)


def _silu(z):
    return z / (1.0 + jnp.exp(-z))


def _gelu_exact(a):
    return 0.5 * a * (1.0 + lax.erf(a * np.float32(math.sqrt(0.5))))


def _proj_kernel(x_ref, pre_g_ref, w_all_ref, ln_g_ref, ln_b_ref, w_s_ref, b_s_ref,
                 qn_g_ref, kvn_g_ref, w_q_ref, w_uk_ref, cos_ref, sin_ref,
                 out_a_ref, gate_b_ref, qidx_ref, widx_ref, qcat_ref, kidx_ref, kcat_ref, vcat_ref):
    x = x_ref[0]
    h = x * lax.rsqrt(jnp.mean(x * x, axis=-1, keepdims=True) + EPS) * pre_g_ref[...]
    proj = _dot(_bf16(h), w_all_ref[...])

    uv = _gelu_exact(proj[:, _C_UV:_C_ZA])
    u, v = uv[:, :GMLP_WIDTH], uv[:, GMLP_WIDTH:]
    mu = jnp.mean(v, axis=-1, keepdims=True)
    vc = v - mu
    var = jnp.mean(vc * vc, axis=-1, keepdims=True)
    vn = _bf16(vc * lax.rsqrt(var + EPS) * ln_g_ref[...] + ln_b_ref[...])
    gate_a = _silu(proj[:, _C_ZA:_C_CQ])
    t_chunk = _chunk_of(lax.broadcasted_iota(jnp.int32, (GMLP_BLOCK, GMLP_BLOCK), 0))
    s_chunk = _chunk_of(lax.broadcasted_iota(jnp.int32, (GMLP_BLOCK, GMLP_BLOCK), 1))
    for g in range(GMLP_GROUPS):
        w_g = _bf16(jnp.where(s_chunk <= t_chunk, w_s_ref[g], 0.0))
        cols = slice(g * GMLP_GROUP_DIM, (g + 1) * GMLP_GROUP_DIM)
        for r in range(TM // GMLP_BLOCK):
            rows = slice(r * GMLP_BLOCK, (r + 1) * GMLP_BLOCK)
            y = _dot(w_g, vn[rows, cols]) + b_s_ref[g]
            out_a_ref[0, rows, cols] = _bf16(u[rows, cols] * y * gate_a[rows, cols])

    gate_b_ref[0] = _silu(proj[:, _C_ZB:_C_END])

    c_q = proj[:, _C_CQ:_C_CKV]
    c_q = c_q * lax.rsqrt(jnp.mean(c_q * c_q, axis=-1, keepdims=True) + EPS) * qn_g_ref[...]
    c_kv = proj[:, _C_CKV:_C_KR]
    c_kv = _bf16(c_kv * lax.rsqrt(jnp.mean(c_kv * c_kv, axis=-1, keepdims=True) + EPS) * kvn_g_ref[...])
    cos = cos_ref[...]
    sin = sin_ref[...]
    k_rope = proj[:, _C_KR:_C_KRS] * cos[:, :LANES] + proj[:, _C_KRS:_C_SLAB] * sin[:, :LANES]
    slab = proj[:, _C_SLAB:_C_ZB]
    kidx_ref[0] = _bf16(slab)
    widx_ref[0] = slab.T
    kcat_ref[0, :, :DSA_KV_RANK] = c_kv
    kcat_ref[0, :, DSA_KV_RANK:] = _bf16(k_rope)
    lane = lax.broadcasted_iota(jnp.int32, (TM, LANES), 1)
    vcat_ref[0, :, :DSA_KV_RANK] = c_kv
    vcat_ref[0, :, DSA_KV_RANK:] = jnp.where(lane == 0, 1.0, 0.0).astype(jnp.bfloat16)

    q_all = _dot(_bf16(c_q), w_q_ref[...])
    q_rope = q_all[:, _Q_ROPE:_Q_ROPES] * cos + q_all[:, _Q_ROPES:_Q_IDX] * sin
    scale = np.float32(1.0 / math.sqrt(DSA_NOPE_DIM + DSA_ROPE_DIM))
    for hd in range(DSA_HEADS):
        cols = slice(_Q_NOPE + hd * LANES, _Q_NOPE + (hd + 1) * LANES)
        q_lat = _dot(_bf16(q_all[:, cols]), w_uk_ref[hd])
        grp, sub = divmod(hd, ROPE_LANE_GROUPS)
        own = (lane >= sub * DSA_ROPE_DIM) & (lane < (sub + 1) * DSA_ROPE_DIM)
        q_r = jnp.where(own, q_rope[:, grp * LANES:(grp + 1) * LANES] * scale, 0.0)
        for t in range(TM // TQ):
            src = slice(t * TQ, (t + 1) * TQ)
            dst = slice(hd * TQ, (hd + 1) * TQ)
            qcat_ref[0, t, dst, :DSA_KV_RANK] = _bf16(q_lat[src] * scale)
            qcat_ref[0, t, dst, DSA_KV_RANK:] = _bf16(q_r[src])
        icol = slice(_Q_IDX + hd * LANES, _Q_IDX + (hd + 1) * LANES)
        qidx_ref[0, hd] = _bf16(q_all[:, icol])


def _dsa_kernel(qidx_ref, widx_ref, qcat_ref, kidx_ref, kcat_ref, vcat_ref, o_ref,
                sc_ref, gmax_ref, bias_ref, m_ref, acc_ref, kmax_ref, *, top_k):
    i = pl.program_id(1)
    n_blocks = (i * TQ) // KB + 1
    idx_scale = np.float32(IDX_HEADS ** -0.5 * IDX_DIM ** -0.5)
    f32 = jnp.float32

    q_chunk = _chunk_of(i * TQ + lax.broadcasted_iota(jnp.int32, (1, TQ), 1))
    key_t = lax.broadcasted_iota(jnp.int32, (KEY_TILE, Q_TILE), 0)

    def over_keys(op, a):
        return op(a, axis=0, keepdims=True)

    w = widx_ref[0] * idx_scale
    gmax_ref[...] = jnp.full(gmax_ref.shape, -jnp.inf, f32)

    def score_block(kb, carry, diagonal=False):
        for ks in range(KB // KEY_TILE):
            k = kidx_ref[0, pl.ds(pl.multiple_of(kb * KB + ks * KEY_TILE, KEY_TILE), KEY_TILE), :]
            krows = slice(ks * KEY_TILE, (ks + 1) * KEY_TILE)
            grows = slice((ks * KEY_TILE) % N_GROUPS, (ks * KEY_TILE) % N_GROUPS + KEY_TILE)
            for qt in range(TQ // Q_TILE):
                qs = slice(qt * Q_TILE, (qt + 1) * Q_TILE)
                if diagonal and ks * KEY_TILE // CHUNK > ((qt + 1) * Q_TILE - 1) // CHUNK:
                    sc_ref[kb, krows, qs] = jnp.full((KEY_TILE, Q_TILE), -jnp.inf, f32)
                    continue
                acc = jnp.zeros((KEY_TILE, Q_TILE), f32)
                for hd in range(IDX_HEADS):
                    logit = _dot_nt(k, qidx_ref[0, hd, qs, :])
                    acc = acc + w[W_IDX_LANE + hd:W_IDX_LANE + hd + 1, qs] * jnp.maximum(logit, 0.0)
                adm = _chunk_of(kb * KB + ks * KEY_TILE + key_t) <= q_chunk[:, qs]
                sc_ref[kb, krows, qs] = jnp.where(adm, acc, -jnp.inf)
                gmax_ref[grows, qs] = jnp.maximum(gmax_ref[grows, qs], jnp.where(adm, acc, -jnp.inf))
        return carry
    lax.fori_loop(0, n_blocks - 1, score_block, 0)
    score_block(n_blocks - 1, 0, diagonal=True)

    kf = np.float32(top_k)

    def count_ge(mid):
        mid_f = _key_to_float(mid)

        def body(kb, cnt):
            for r in range(KB // COUNT_ROWS):
                cnt = cnt + jnp.where(sc_ref[kb, r * COUNT_ROWS:(r + 1) * COUNT_ROWS, :] >= mid_f, 1.0, 0.0)
            return cnt
        cnt = lax.fori_loop(0, n_blocks - 1, body, jnp.zeros((COUNT_ROWS, TQ), f32))
        for r in range(KB // COUNT_ROWS):
            lane0 = (r * COUNT_ROWS // CHUNK) * CHUNK // LANES * LANES
            rows = slice(r * COUNT_ROWS, (r + 1) * COUNT_ROWS)
            hit = jnp.where(sc_ref[n_blocks - 1, rows, lane0:] >= mid_f[:, lane0:], 1.0, 0.0)
            cnt = jnp.concatenate([cnt[:, :lane0], cnt[:, lane0:] + hit], axis=1) if lane0 else cnt + hit
        return over_keys(jnp.sum, cnt)

    def canonical(key):
        return jnp.where((key > 0) & (key < KEY_TINY), KEY_TINY, jnp.where((key < 0) & (key >= -KEY_TINY), 0, key))

    def pivot(lo, hi):
        mid = canonical((lo & hi) + lax.shift_right_arithmetic(lo ^ hi, jnp.int32(1)))
        return jnp.where((mid <= lo) | (mid >= hi), lo, mid)

    def pivot_zero_first(lo, hi):
        return jnp.where((lo <= 0) & (hi > KEY_TINY), KEY_TINY,
                         jnp.where((lo < 0) & (hi == KEY_TINY), 0, pivot(lo, hi)))

    grp = gmax_ref[...]
    grp_min = over_keys(jnp.min, grp)
    select_all = (q_chunk + 1) * CHUNK <= top_k
    lo = jnp.where(select_all, KEY_LOWEST, canonical(_sort_key(grp_min)))
    hi = jnp.where(select_all, KEY_LOWEST, canonical(_sort_key(over_keys(jnp.max, grp)) + 1))

    def step(lo, hi, mid):
        cnt = count_ge(mid)
        ge = cnt >= kf
        return jnp.where(ge, mid, lo), jnp.where(cnt == kf, mid, jnp.where(ge, hi, mid))

    grp_mean = over_keys(jnp.sum, grp) / np.float32(N_GROUPS)
    grp_var = over_keys(jnp.sum, grp * grp) / np.float32(N_GROUPS) - grp_mean * grp_mean
    grp_std = jnp.sqrt(jnp.maximum(grp_var, 0.0))
    for below in FIRST_PIVOTS:
        guess = canonical(_sort_key(grp_mean - np.float32(below) * grp_std))
        usable = (grp_min > -jnp.inf) & (guess > lo) & (guess < hi)
        lo, hi = step(lo, hi, jnp.where(usable, guess, pivot(lo, hi)))
    for _ in range(2):
        lo, hi = step(lo, hi, pivot_zero_first(lo, hi))

    def bisect_cond(state):
        it, lo, _, mid = state
        return jnp.logical_and(it < MAX_BISECT, jnp.max(jnp.where(mid != lo, 1.0, 0.0)) > 0.0)

    def bisect_body(state):
        it, lo, hi, mid = state
        lo, hi = step(lo, hi, mid)
        return it + 1, lo, hi, pivot(lo, hi)

    _, lo, hi, _ = lax.while_loop(bisect_cond, bisect_body, (jnp.int32(0), lo, hi, pivot(lo, hi)))
    thr = _key_to_float(lo)
    tie = hi != lo

    @pl.when(jnp.max(jnp.where(tie, 1.0, 0.0)) > 0.0)
    def _():
        need = kf - count_ge(hi)
        upto = (lax.broadcasted_iota(jnp.int32, (KB, KB), 1)
                <= lax.broadcasted_iota(jnp.int32, (KB, KB), 0)).astype(jnp.bfloat16)

        for qt in range(TQ // Q_TILE):
            qs = slice(qt * Q_TILE, (qt + 1) * Q_TILE)

            @pl.when(jnp.max(jnp.where(tie[:, qs], 1.0, 0.0)) > 0.0)
            def _(qs=qs):
                def drop_body(kb, seen):
                    keys = sc_ref[kb, :, qs]
                    band = tie[:, qs] & (keys == thr[:, qs])
                    band_f = jnp.where(band, 1.0, 0.0)
                    rank = _dot(upto, _bf16(band_f))
                    late = band & (rank + seen > need[:, qs])
                    sc_ref[kb, :, qs] = jnp.where(late, -jnp.inf, keys)
                    return seen + over_keys(jnp.sum, band_f)
                lax.fori_loop(0, n_blocks, drop_body, jnp.zeros((1, Q_TILE), f32))

    @pl.when(i == 0)
    def _():
        def norm_block(kb, run):
            kc = kcat_ref[0, pl.ds(pl.multiple_of(kb * KB, KB), KB), :].astype(f32)
            return jnp.maximum(run, jnp.sum(kc * kc, axis=1, keepdims=True))
        run = lax.fori_loop(0, kcat_ref.shape[1] // KB, norm_block, jnp.zeros((KB, 1), f32))
        kmax_ref[...] = jnp.broadcast_to(jnp.max(run, axis=0, keepdims=True), kmax_ref.shape)

    k_norm2 = kmax_ref[0:1, :]
    q_all = qcat_ref[0, 0]
    q_norm2 = _dot(q_all * q_all, jnp.ones((Q_CAT, LANES), jnp.bfloat16))
    qk2 = jnp.maximum(q_norm2 * k_norm2, 1e-30)
    m_ref[...] = qk2 * lax.rsqrt(qk2) * BOUND_SLACK

    chunks = [slice(c * LANES, (c + 1) * LANES) for c in range(KB // LANES)]
    head_rows = [slice(hd * TQ, (hd + 1) * TQ) for hd in range(DSA_HEADS)]

    def set_bias(kb):
        for qt in range(TQ // LANES):
            qs = slice(qt * LANES, (qt + 1) * LANES)
            bias_ref[qs, :] = jnp.where(sc_ref[kb, :, qs] >= thr[:, qs], 0.0, NEG_BIG).T

    def attend():
        def attn_block(kb, first):
            rows = pl.ds(pl.multiple_of(kb * KB, KB), KB)
            kc = kcat_ref[0, rows, :]
            vc = vcat_ref[0, rows, :]
            set_bias(kb)
            for hrows in head_rows:
                s = _dot_nt(qcat_ref[0, 0, hrows, :], kc)
                m = m_ref[hrows, :]
                p = [_bf16(jnp.exp(s[:, cols] + bias_ref[:, cols] - m)) for cols in chunks]
                pv = _dot(jnp.concatenate(p, axis=1), vc)
                acc_ref[hrows, :] = pv if first else acc_ref[hrows, :] + pv

        attn_block(0, True)

        def more(kb, carry):
            attn_block(kb, False)
            return carry
        lax.fori_loop(1, n_blocks, more, 0)
        return jnp.min(acc_ref[:, DSA_KV_RANK:DSA_KV_RANK + 1])

    @pl.when(attend() < ROW_SUM_MIN)
    def _():
        m_ref[...] = jnp.full(m_ref.shape, NEG_BIG, f32)

        def max_block(kb, carry):
            kc = kcat_ref[0, pl.ds(pl.multiple_of(kb * KB, KB), KB), :]
            set_bias(kb)
            for hrows in head_rows:
                s = _dot_nt(qcat_ref[0, 0, hrows, :], kc)
                sm = [s[:, cols] + bias_ref[:, cols] for cols in chunks]
                m_ref[hrows, :] = jnp.maximum(m_ref[hrows, :], functools.reduce(jnp.maximum, sm))
            return carry
        lax.fori_loop(0, n_blocks, max_block, 0)
        m_all = m_ref[...]
        m_ref[...] = jnp.broadcast_to(jnp.max(m_all, axis=1, keepdims=True), m_all.shape)
        attend()

    for hd, hrows in enumerate(head_rows):
        a = acc_ref[hrows, :]
        denom = jnp.broadcast_to(a[:, DSA_KV_RANK:DSA_KV_RANK + 1], (TQ, DSA_KV_RANK))
        o_ref[0, :, hd * DSA_KV_RANK:(hd + 1) * DSA_KV_RANK] = _bf16(a[:, :DSA_KV_RANK] / denom)


def _out_kernel(x_ref, out_a_ref, o_lat_ref, gate_b_ref, w_uv_ref, w_out_ref, post_g_ref, y_ref):
    group = UV_GROUP * DSA_KV_RANK
    o = jnp.concatenate([_dot(o_lat_ref[0, :, g * group:(g + 1) * group], w_uv_ref[g])
                         for g in range(DSA_HEADS // UV_GROUP)], axis=1)
    out_b = _bf16(o * gate_b_ref[0])
    y = _dot(out_a_ref[0], w_out_ref[:GMLP_WIDTH, :]) + _dot(out_b, w_out_ref[GMLP_WIDTH:, :])
    y = y * lax.rsqrt(jnp.mean(y * y, axis=-1, keepdims=True) + EPS) * post_g_ref[...]
    y_ref[0] = x_ref[0] + y


def _full(shape):
    return pl.BlockSpec(shape, lambda b, t: (0,) * len(shape))


def _layer(x, w_in, pre_g, post_g, ln_g, ln_b, w_s, b_s, qn_g, kvn_g, w_uq, w_uk, w_uv, w_q_idx, w_out):
    B, S, D = x.shape
    assert D == D_MODEL and S % KB == 0 and S % TM == 0 and TM % TQ == 0 and KB == TQ
    top_k = min(TOPK_MAX, S // 4)
    assert top_k <= N_GROUPS <= KB and KB % N_GROUPS == 0 and N_GROUPS % KEY_TILE == 0
    f32, bf16 = jnp.float32, jnp.bfloat16

    c_kr = 3 * GMLP_WIDTH + DSA_Q_RANK + DSA_KV_RANK
    c_idx = c_kr + DSA_ROPE_DIM
    c_zb = c_idx + IDX_DIM + IDX_HEADS
    assert w_in.shape[1] == c_zb + DSA_WIDTH
    kr = w_in[:, c_kr:c_idx]
    kr_sw = jnp.concatenate([kr[:, ROPE_HALF:], kr[:, :ROPE_HALF]], axis=1)
    w_all = jnp.concatenate([
        w_in[:, :c_kr], jnp.tile(kr, (1, ROPE_LANE_GROUPS)), jnp.tile(kr_sw, (1, ROPE_LANE_GROUPS)),
        w_in[:, c_idx:c_zb], jnp.zeros((D, LANES - IDX_DIM - IDX_HEADS), f32), w_in[:, c_zb:]], axis=1).astype(bf16)
    assert w_all.shape[1] == _C_END
    wq3 = w_uq.reshape(DSA_Q_RANK, DSA_HEADS, DSA_NOPE_DIM + DSA_ROPE_DIM)
    nope = jnp.pad(wq3[:, :, :DSA_NOPE_DIM], ((0, 0), (0, 0), (0, LANES - DSA_NOPE_DIM)))
    rope = wq3[:, :, DSA_NOPE_DIM:]
    rope_sw = jnp.concatenate([rope[:, :, ROPE_HALF:], rope[:, :, :ROPE_HALF]], axis=2)
    wqi = jnp.pad(w_q_idx.reshape(DSA_Q_RANK, IDX_HEADS, IDX_DIM), ((0, 0), (0, 0), (0, LANES - IDX_DIM)))
    w_q = jnp.concatenate([nope.reshape(DSA_Q_RANK, -1), rope.reshape(DSA_Q_RANK, -1),
                           rope_sw.reshape(DSA_Q_RANK, -1), wqi.reshape(DSA_Q_RANK, -1)], axis=1).astype(bf16)
    w_uk_t = jnp.pad(jnp.transpose(w_uk, (1, 2, 0)), ((0, 0), (0, LANES - DSA_NOPE_DIM), (0, 0))).astype(bf16)
    eye = jnp.eye(UV_GROUP, dtype=f32)
    w_uv_g = jnp.transpose(w_uv, (1, 0, 2)).reshape(DSA_HEADS // UV_GROUP, UV_GROUP, DSA_KV_RANK, DSA_V_DIM)
    w_uv_bd = (w_uv_g[:, :, :, None, :] * eye[None, :, None, :, None]).reshape(
        DSA_HEADS // UV_GROUP, UV_GROUP * DSA_KV_RANK, UV_GROUP * DSA_V_DIM).astype(bf16)
    b_s_b = jnp.broadcast_to(b_s[:, :, None], (GMLP_GROUPS, GMLP_BLOCK, GMLP_GROUP_DIM))

    pos = jnp.arange(S, dtype=f32)
    inv_freq = ROPE_THETA ** (-jnp.arange(0, DSA_ROPE_DIM, 2, dtype=f32) / DSA_ROPE_DIM)
    ang = pos[:, None] * inv_freq[None, :]
    cos_t = jnp.tile(jnp.concatenate([jnp.cos(ang), jnp.cos(ang)], axis=1), (1, DSA_HEADS))
    sin_t = jnp.tile(jnp.concatenate([-jnp.sin(ang), jnp.sin(ang)], axis=1), (1, DSA_HEADS))

    row2 = lambda a: a.reshape(1, -1)
    tok = lambda width: pl.BlockSpec((1, TM, width), lambda b, t: (b, t, 0))
    hm = lambda width: pl.BlockSpec((1, DSA_HEADS, TM, width), lambda b, t: (b, 0, t, 0))

    out_a, gate_b, qidx, widx, qcat, kidx, kcat, vcat = pl.pallas_call(
        _proj_kernel,
        grid=(B, S // TM),
        in_specs=[tok(D), _full((1, D)), _full((D, _C_END)), _full((1, GMLP_WIDTH)), _full((1, GMLP_WIDTH)),
                  _full((GMLP_GROUPS, GMLP_BLOCK, GMLP_BLOCK)), _full((GMLP_GROUPS, GMLP_BLOCK, GMLP_GROUP_DIM)),
                  _full((1, DSA_Q_RANK)), _full((1, DSA_KV_RANK)), _full((DSA_Q_RANK, _Q_END)),
                  _full((DSA_HEADS, LANES, DSA_KV_RANK)),
                  pl.BlockSpec((TM, 2 * LANES), lambda b, t: (t, 0)),
                  pl.BlockSpec((TM, 2 * LANES), lambda b, t: (t, 0))],
        out_specs=[tok(GMLP_WIDTH), tok(DSA_WIDTH), hm(LANES),
                   pl.BlockSpec((1, LANES, TM), lambda b, t: (b, 0, t)),
                   pl.BlockSpec((1, TM // TQ, DSA_HEADS * TQ, Q_CAT), lambda b, t: (b, t, 0, 0)),
                   tok(LANES), tok(Q_CAT), tok(Q_CAT)],
        out_shape=[jax.ShapeDtypeStruct((B, S, GMLP_WIDTH), bf16),
                   jax.ShapeDtypeStruct((B, S, DSA_WIDTH), f32),
                   jax.ShapeDtypeStruct((B, DSA_HEADS, S, LANES), bf16),
                   jax.ShapeDtypeStruct((B, LANES, S), f32),
                   jax.ShapeDtypeStruct((B, S // TQ, DSA_HEADS * TQ, Q_CAT), bf16),
                   jax.ShapeDtypeStruct((B, S, LANES), bf16),
                   jax.ShapeDtypeStruct((B, S, Q_CAT), bf16),
                   jax.ShapeDtypeStruct((B, S, Q_CAT), bf16)],
        compiler_params=pltpu.CompilerParams(
            dimension_semantics=("arbitrary", "arbitrary"), vmem_limit_bytes=48 * 1024 * 1024),
        name="proj_gmlp_dsa_prep",
    )(x, row2(pre_g), w_all, row2(ln_g), row2(ln_b), w_s, b_s_b, row2(qn_g), row2(kvn_g), w_q, w_uk_t,
      cos_t, sin_t)

    qt = lambda width: pl.BlockSpec((1, DSA_HEADS, TQ, width), lambda b, t: (b, 0, t, 0))
    seq = lambda width: pl.BlockSpec((1, S, width), lambda b, t: (b, 0, 0), pipeline_mode=pl.Buffered(1))
    o_lat = pl.pallas_call(
        functools.partial(_dsa_kernel, top_k=top_k),
        grid=(B, S // TQ),
        in_specs=[qt(LANES), pl.BlockSpec((1, LANES, TQ), lambda b, t: (b, 0, t)),
                  pl.BlockSpec((1, 1, DSA_HEADS * TQ, Q_CAT), lambda b, t: (b, t, 0, 0)),
                  seq(LANES), seq(Q_CAT), seq(Q_CAT)],
        out_specs=pl.BlockSpec((1, TQ, DSA_HEADS * DSA_KV_RANK), lambda b, t: (b, t, 0)),
        out_shape=jax.ShapeDtypeStruct((B, S, DSA_HEADS * DSA_KV_RANK), bf16),
        scratch_shapes=[pltpu.VMEM((S // KB, KB, TQ), f32),
                        pltpu.VMEM((N_GROUPS, TQ), f32),
                        pltpu.VMEM((TQ, KB), f32),
                        pltpu.VMEM((DSA_HEADS * TQ, LANES), f32),
                        pltpu.VMEM((DSA_HEADS * TQ, Q_CAT), f32),
                        pltpu.VMEM((8, LANES), f32)],
        compiler_params=pltpu.CompilerParams(
            dimension_semantics=("arbitrary", "arbitrary"), vmem_limit_bytes=56 * 1024 * 1024),
        name="dsa_index_select_attend",
    )(qidx, widx, qcat, kidx, kcat, vcat)

    return pl.pallas_call(
        _out_kernel,
        grid=(B, S // TM),
        in_specs=[tok(D), tok(GMLP_WIDTH), tok(DSA_HEADS * DSA_KV_RANK), tok(DSA_WIDTH),
                  _full((DSA_HEADS // UV_GROUP, UV_GROUP * DSA_KV_RANK, UV_GROUP * DSA_V_DIM)),
                  _full((GMLP_WIDTH + DSA_WIDTH, D)), _full((1, D))],
        out_specs=tok(D),
        out_shape=jax.ShapeDtypeStruct((B, S, D), f32),
        compiler_params=pltpu.CompilerParams(
            dimension_semantics=("arbitrary", "arbitrary"), vmem_limit_bytes=32 * 1024 * 1024),
        name="out_proj_norm_residual",
    )(x, out_a, o_lat, gate_b, w_uv_bd, w_out.astype(bf16), row2(post_g))


def kernel(x, w_in, pre_norm_g, post_norm_g, gmlp_ln_g, gmlp_ln_b, gmlp_w_s, gmlp_b_s, dsa_q_norm_g, dsa_kv_norm_g, dsa_w_uq, dsa_w_uk, dsa_w_uv, dsa_w_q_idx, w_out):
    for l in range(w_in.shape[0]):
        x = _layer(x, w_in[l], pre_norm_g[l], post_norm_g[l], gmlp_ln_g[l], gmlp_ln_b[l], gmlp_w_s[l],
                   gmlp_b_s[l], dsa_q_norm_g[l], dsa_kv_norm_g[l], dsa_w_uq[l], dsa_w_uk[l], dsa_w_uv[l],
                   dsa_w_q_idx[l], w_out[l])
    return x
```
